```python
import math
import jax, jax.numpy as jnp
from jax import lax
import numpy as np

D_MODEL = 1024
BATCH = 16
SEQ = 256
DEPTH = 2
DEC_BATCH = 2
DEC_SEQ = 2048
PAST_LEN = 512

GRID_W = 64
HY_W = 256
HY_ORDER = 2
N_BANDS = 8
POS_EMB = 1 + 2 * N_BANDS
FILT_HID = 64
N_HEADS = 8
Q_LORA = 256
KV_LORA = 128
NOPE_DIM = 64
ROPE_DIM = 32
V_DIM = 64
ROPE_BASE = 10000.0
Q_BLOCK = 128
FN_GROUPS = 4
FN_GROUP_W = 64
FN_W = FN_GROUPS * FN_GROUP_W
N_BRANCH = 3
D_FF = 2816
EPS = 1e-6
IN_COLS = 3 * HY_W + Q_LORA + KV_LORA + ROPE_DIM + FN_W + N_BRANCH * D_MODEL

kernel_name = 'hybrid_hyena_mla_fnet_diffusion_step'


def rmsnorm(x, g):
    x32 = x.astype(jnp.float32)
    y = x32 * lax.rsqrt(jnp.mean(x32 * x32, axis=-1, keepdims=True) + EPS)
    return (y * g.astype(jnp.float32)).astype(x.dtype)


def swiglu(x, w_up, w_down):
    g, u = jnp.split(x @ w_up, 2, axis=-1)
    return (jax.nn.silu(g) * u) @ w_down


def short_conv(u, w, b):
    up = jnp.pad(u, ((0, 0), (1, 1), (0, 0)))
    return up[:, :-2] * w[0] + up[:, 1:-1] * w[1] + up[:, 2:] * w[2] + b


def hyena_filters(L, w1, b1, w2, b2, w3):
    f32 = jnp.float32
    t_idx = jnp.arange(L, dtype=f32)
    t_norm = t_idx / (L - 1)
    w = 2.0 * math.pi * t_idx / L
    bands = jnp.linspace(1e-4, N_BANDS - 1, N_BANDS, dtype=f32)
    ang = w[:, None] * bands[None, :]
    feats = jnp.concatenate([t_norm[:, None], jnp.cos(ang), -jnp.sin(ang)], axis=-1)
    h = jnp.sin(feats @ w1.astype(f32) + b1.astype(f32))
    h = jnp.sin(h @ w2.astype(f32) + b2.astype(f32))
    h = (h @ w3.astype(f32)).reshape(L, 2, HY_ORDER, HY_W)
    deltas = jnp.linspace(math.log(1e-2) / 1.5, math.log(1e-2) / 0.3, HY_W, dtype=f32)
    decay = jnp.exp(-t_norm[:, None] * jnp.abs(deltas)[None, :])
    h = h * decay[:, None, None, :]
    h_fwd, h_bwd = h[:, 0], h[:, 1]
    filt = jnp.concatenate([h_fwd, jnp.zeros((1, HY_ORDER, HY_W), f32), h_bwd[1:][::-1]], axis=0)
    filt = filt / (jnp.sum(jnp.abs(filt), axis=0, keepdims=True) + EPS)
    return jnp.fft.rfft(filt, axis=0)


def fftconv(u, filt_f, skip):
    L = u.shape[1]
    u32 = u.astype(jnp.float32)
    U = jnp.fft.rfft(u32, n=2 * L, axis=1)
    y = jnp.fft.irfft(U * filt_f[None], n=2 * L, axis=1)[:, :L]
    return (y + u32 * skip.astype(jnp.float32)).astype(u.dtype)


def hyena_mixer(u, conv_w, conv_b, w1, b1, w2, b2, w3, skip):
    L = u.shape[1]
    u = short_conv(u, conv_w, conv_b)
    x1, x2, v = jnp.split(u, 3, axis=-1)
    filt_f = hyena_filters(L, w1, b1, w2, b2, w3)
    z = x1 * fftconv(v, filt_f[:, 0], skip[0])
    z = x2 * fftconv(z, filt_f[:, 1], skip[1])
    return z


def axial_angles(rows_n):
    f32 = jnp.float32
    rows = jnp.repeat(jnp.arange(rows_n, dtype=f32), GRID_W)
    cols = jnp.tile(jnp.arange(GRID_W, dtype=f32), rows_n)
    n_freq = ROPE_DIM // 4
    inv = ROPE_BASE ** (-jnp.arange(n_freq, dtype=f32) / n_freq)
    return rows[:, None] * inv[None, :], cols[:, None] * inv[None, :]


def rotate_block(x, ang):
    c = jnp.cos(ang)[:, None, :]
    s = jnp.sin(ang)[:, None, :]
    x1, x2 = jnp.split(x, 2, axis=-1)
    return jnp.concatenate([x1 * c - x2 * s, x2 * c + x1 * s], axis=-1)


def axial_rope(x, ang_r, ang_c):
    xr, xc = jnp.split(x, 2, axis=-1)
    return jnp.concatenate([rotate_block(xr, ang_r), rotate_block(xc, ang_c)], axis=-1).astype(x.dtype)


def blocked_attention(q, k, v):
    B, Lq, H, dq = q.shape
    nb = Lq // Q_BLOCK
    scale = 1.0 / math.sqrt(dq)
    qb = q.reshape(B, nb, Q_BLOCK, H, dq).transpose(1, 0, 2, 3, 4)

    def one_block(qi):
        s = jnp.einsum('bqhd,bkhd->bhqk', qi, k).astype(jnp.float32) * scale
        p = jax.nn.softmax(s, axis=-1).astype(v.dtype)
        return jnp.einsum('bhqk,bkhd->bqhd', p, v)

    o = lax.map(one_block, qb)
    return o.transpose(1, 0, 2, 3, 4).reshape(B, Lq, H * v.shape[-1])


def token_mix(n, l, P, rope_ang, ctx_ckv, ctx_krope):
    B, L, _ = n.shape
    proj = n @ P['w_in'][l]
    i0 = 3 * HY_W
    i1 = i0 + Q_LORA
    i2 = i1 + KV_LORA
    i3 = i2 + ROPE_DIM
    i4 = i3 + FN_W
    hy, q_a, kv_a, k_r, fn, gates = jnp.split(proj, [i0, i1, i2, i3, i4], axis=-1)

    y_a = hyena_mixer(hy, P['hy_conv_w'][l], P['hy_conv_b'][l], P['hy_filt_w1'][l], P['hy_filt_b1'][l],
                      P['hy_filt_w2'][l], P['hy_filt_b2'][l], P['hy_filt_w3'][l], P['hy_skip'][l]) @ P['w_hy_out'][l]

    q = (rmsnorm(q_a, P['q_norm_g'][l]) @ P['w_qb'][l]).reshape(B, L, N_HEADS, NOPE_DIM + ROPE_DIM)
    q_nope, q_rope = jnp.split(q, [NOPE_DIM], axis=-1)
    ckv = rmsnorm(kv_a, P['kv_norm_g'][l])
    k_rope = k_r[:, :, None, :]
    if rope_ang is not None:
        q_rope = axial_rope(q_rope, rope_ang[0], rope_ang[1])
        k_rope = axial_rope(k_rope, rope_ang[0], rope_ang[1])
    kv = (ckv @ P['w_kvb'][l]).reshape(B, L, N_HEADS, NOPE_DIM + V_DIM)
    k_nope, v = jnp.split(kv, [NOPE_DIM], axis=-1)
    if ctx_ckv is not None:
        Lc = ctx_ckv.shape[1]
        kv_c = (ctx_ckv @ P['w_kvb'][l]).reshape(B, Lc, N_HEADS, NOPE_DIM + V_DIM)
        k_nope = jnp.concatenate([k_nope, kv_c[..., :NOPE_DIM]], axis=1)
        v = jnp.concatenate([v, kv_c[..., NOPE_DIM:]], axis=1)
        k_rope = jnp.concatenate([k_rope, ctx_krope[:, :, None, :]], axis=1)
    k = jnp.concatenate([k_nope, jnp.broadcast_to(k_rope, k_nope.shape[:-1] + (ROPE_DIM,))], axis=-1)
    q = jnp.concatenate([q_nope, q_rope], axis=-1)
    y_b = blocked_attention(q, k, v) @ P['w_mla_o'][l]

    f = fn.reshape(B, L, FN_GROUPS, FN_GROUP_W).astype(jnp.float32)
    f = jnp.fft.fft2(f, axes=(1, 3), norm='ortho').real.reshape(B, L, FN_W).astype(n.dtype)
    y_c = f @ P['w_fnet'][l]

    g_a, g_b, g_c = jnp.split(jax.nn.sigmoid(gates), N_BRANCH, axis=-1)
    y = (g_a * y_a + g_b * y_b + g_c * y_c) @ P['w_out'][l]
    return y, ckv, k_r


def trunk_layer(h, ada, l, P, rope_ang, ctx_ckv, ctx_krope):
    s0, sc0, g0, s1, sc1, g1, s2, sc2, g2 = [m[:, None, :] for m in jnp.split(ada, 9, axis=-1)]
    n = rmsnorm(h, P['norm_g'][l, 0]) * (1 + sc0) + s0
    h = h + 0.5 * g0 * swiglu(n, P['w_ffn_up'][l, 0], P['w_ffn_down'][l, 0])
    n = rmsnorm(h, P['norm_g'][l, 1]) * (1 + sc1) + s1
    y, ckv, k_r = token_mix(n, l, P, rope_ang, ctx_ckv, ctx_krope)
    h = h + g1 * y
    n = rmsnorm(h, P['norm_g'][l, 2]) * (1 + sc2) + s2
    h = h + 0.5 * g2 * swiglu(n, P['w_ffn_up'][l, 1], P['w_ffn_down'][l, 1])
    return h, ckv, k_r


def setup_inputs(seed: int = 0) -> dict:
    key = jax.random.key(seed)
    ks = jax.random.split(key, 32)
    f32 = jnp.float32

    def nrm(k, shape, scale=1.0):
        return jax.random.normal(k, shape, f32) * scale

    D = D_MODEL
    return {
        'x_prompt': nrm(ks[0], (BATCH, SEQ, D)),
        'x_sample': nrm(ks[1], (DEC_BATCH, DEC_SEQ, D)),
        'cache_ckv': nrm(ks[2], (DEC_BATCH, DEPTH, PAST_LEN, KV_LORA)),
        'cache_krope': nrm(ks[3], (DEC_BATCH, DEPTH, PAST_LEN, ROPE_DIM)),
        'c': nrm(ks[4], (DEC_BATCH, D)),
        'c_ctx': nrm(ks[5], (D,)),
        'w_ada': nrm(ks[6], (DEPTH, D, 9 * D), 0.5 * D ** -0.5),
        'b_ada': nrm(ks[7], (DEPTH, 9 * D), 0.01),
        'norm_g': 1.0 + nrm(ks[8], (DEPTH, 3, D), 0.05),
        'w_ffn_up': nrm(ks[9], (DEPTH, 2, D, 2 * D_FF), D ** -0.5),
        'w_ffn_down': nrm(ks[10], (DEPTH, 2, D_FF, D), D_FF ** -0.5),
        'w_in': nrm(ks[11], (DEPTH, D, IN_COLS), D ** -0.5),
        'hy_conv_w': nrm(ks[12], (DEPTH, 3, 3 * HY_W), 3 ** -0.5),
        'hy_conv_b': nrm(ks[13], (DEPTH, 3 * HY_W), 0.01),
        'hy_filt_w1': nrm(ks[14], (DEPTH, POS_EMB, FILT_HID), POS_EMB ** -0.5),
        'hy_filt_b1': nrm(ks[15], (DEPTH, FILT_HID), 0.1),
        'hy_filt_w2': nrm(ks[16], (DEPTH, FILT_HID, FILT_HID), FILT_HID ** -0.5),
        'hy_filt_b2': nrm(ks[17], (DEPTH, FILT_HID), 0.1),
        'hy_filt_w3': nrm(ks[18], (DEPTH, FILT_HID, 2 * HY_ORDER * HY_W), FILT_HID ** -0.5),
        'hy_skip': nrm(ks[19], (DEPTH, HY_ORDER, HY_W), 0.5),
        'w_hy_out': nrm(ks[20], (DEPTH, HY_W, D), HY_W ** -0.5),
        'q_norm_g': 1.0 + nrm(ks[21], (DEPTH, Q_LORA), 0.05),
        'w_qb': nrm(ks[22], (DEPTH, Q_LORA, N_HEADS * (NOPE_DIM + ROPE_DIM)), Q_LORA ** -0.5),
        'kv_norm_g': 1.0 + nrm(ks[23], (DEPTH, KV_LORA), 0.05),
        'w_kvb': nrm(ks[24], (DEPTH, KV_LORA, N_HEADS * (NOPE_DIM + V_DIM)), KV_LORA ** -0.5),
        'w_mla_o': nrm(ks[25], (DEPTH, N_HEADS * V_DIM, D), (N_HEADS * V_DIM) ** -0.5),
        'w_fnet': nrm(ks[26], (DEPTH, FN_W, D), FN_W ** -0.5),
        'w_out': nrm(ks[27], (DEPTH, D, D), D ** -0.5),
        'final_g': 1.0 + nrm(ks[28], (D,), 0.05),
    }


def reference(x_prompt, x_sample, cache_ckv, cache_krope, c, c_ctx, w_ada, b_ada, norm_g, w_ffn_up, w_ffn_down,
              w_in, hy_conv_w, hy_conv_b, hy_filt_w1, hy_filt_b1, hy_filt_w2, hy_filt_b2, hy_filt_w3, hy_skip,
              w_hy_out, q_norm_g, w_qb, kv_norm_g, w_kvb, w_mla_o, w_fnet, w_out, final_g):
    P = {
        'norm_g': norm_g, 'w_ffn_up': w_ffn_up, 'w_ffn_down': w_ffn_down, 'w_in': w_in,
        'hy_conv_w': hy_conv_w, 'hy_conv_b': hy_conv_b, 'hy_filt_w1': hy_filt_w1, 'hy_filt_b1': hy_filt_b1,
        'hy_filt_w2': hy_filt_w2, 'hy_filt_b2': hy_filt_b2, 'hy_filt_w3': hy_filt_w3, 'hy_skip': hy_skip,
        'w_hy_out': w_hy_out, 'q_norm_g': q_norm_g, 'w_qb': w_qb, 'kv_norm_g': kv_norm_g, 'w_kvb': w_kvb,
        'w_mla_o': w_mla_o, 'w_fnet': w_fnet, 'w_out': w_out,
    }
    rows_n = x_sample.shape[1] // GRID_W
    rope_ang = axial_angles(rows_n)
    silu_ctx = jax.nn.silu(c_ctx)[None, :]
    silu_c = jax.nn.silu(c)
    hp = x_prompt
    hs = x_sample
    ckv_list = []
    kr_list = []
    for l in range(DEPTH):
        ada_ctx = silu_ctx @ w_ada[l] + b_ada[l]
        hp, ckv, kr = trunk_layer(hp, ada_ctx, l, P, None, None, None)
        ckv_list.append(ckv)
        kr_list.append(kr)
        ada_lat = silu_c @ w_ada[l] + b_ada[l]
        hs, _, _ = trunk_layer(hs, ada_lat, l, P, rope_ang, cache_ckv[:, l], cache_krope[:, l])
    y_prompt = rmsnorm(hp, final_g)
    y_sample = rmsnorm(hs, final_g)
    new_ckv = jnp.stack(ckv_list, axis=1)
    new_krope = jnp.stack(kr_list, axis=1)
    return (y_prompt, y_sample, new_ckv, new_krope)
```

```python
import functools
import math

import numpy as np
import jax
import jax.numpy as jnp
from jax import lax
from jax.experimental import pallas as pl
from jax.experimental.pallas import tpu as pltpu

F32 = jnp.float32
BF16 = jnp.bfloat16
HIGHEST = lax.Precision.HIGHEST

D_MODEL = 1024
BATCH = 16
SEQ = 256
DEPTH = 2
DEC_BATCH = 2
DEC_SEQ = 2048
PAST_LEN = 512
GRID_W = 64
HY_W = 256
HY_ORDER = 2
N_BANDS = 8
POS_EMB = 1 + 2 * N_BANDS
FILT_HID = 64
N_HEADS = 8
Q_LORA = 256
KV_LORA = 128
NOPE_DIM = 64
ROPE_DIM = 32
V_DIM = 64
ROPE_BASE = 10000.0
FN_GROUPS = 4
FN_GROUP_W = 64
FN_W = FN_GROUPS * FN_GROUP_W
N_BRANCH = 3
D_FF = 2816
EPS = 1e-6

T_CTX = BATCH * SEQ
T_LAT = DEC_BATCH * DEC_SEQ
T_ALL = T_CTX + T_LAT
N_GROUPS = 1 + DEC_BATCH
HEAD_SLOT = 128
QK_W = N_HEADS * HEAD_SLOT
POS_PAD = 128

VMEM_LIMIT = 56 * 1024 * 1024

TM_FFN = 1024
TF_FFN = 256
TM_MIX = 512

MIX_HY = 0
MIX_QA = 3 * HY_W
MIX_KVA = MIX_QA + Q_LORA
MIX_FN = MIX_KVA + KV_LORA
MIX_KR = MIX_FN + FN_W
MIX_KRS = MIX_KR + 128
MIX_W = MIX_KRS + 128

_ROPE_PERM = np.array(list(range(8, 16)) + list(range(0, 8)) + list(range(24, 32)) + list(range(16, 24)))
_ROPE_SIGN = np.array([-1.0] * 8 + [1.0] * 8 + [-1.0] * 8 + [1.0] * 8, np.float32)


def _dft_tables(L, half):
    k = np.arange(L, dtype=np.int64)
    period = 2 * L if half else L
    m = (k[:, None] * k[None, :]) % period
    ang = 2.0 * np.pi * m.astype(np.float64) / period
    return np.cos(ang).astype(np.float32), np.sin(ang).astype(np.float32)


def _filter_tables(L):
    t = np.arange(L, dtype=np.float64)
    t_norm = t / (L - 1)
    w = 2.0 * np.pi * t / L
    bands = np.linspace(1e-4, N_BANDS - 1, N_BANDS)
    ang = w[:, None] * bands[None, :]
    feats = np.concatenate([t_norm[:, None], np.cos(ang), -np.sin(ang)], axis=-1)
    feats = np.pad(feats, ((0, 0), (0, POS_PAD - POS_EMB)))
    deltas = np.linspace(math.log(1e-2) / 1.5, math.log(1e-2) / 0.3, HY_W)
    decay = np.exp(-t_norm[:, None] * np.abs(deltas)[None, :])
    decay = np.concatenate([decay, decay], axis=1)
    alt = np.where(np.arange(L) % 2 == 0, 1.0, -1.0)[:, None]
    return feats.astype(np.float32), decay.astype(np.float32), alt.astype(np.float32)


def _rope_tables():
    t = np.arange(DEC_SEQ)
    row = (t // GRID_W).astype(np.float64)
    col = (t % GRID_W).astype(np.float64)
    nf = ROPE_DIM // 4
    inv = ROPE_BASE ** (-np.arange(nf, dtype=np.float64) / nf)
    ar = row[:, None] * inv[None, :]
    ac = col[:, None] * inv[None, :]
    cos32 = np.concatenate([np.cos(ar), np.cos(ar), np.cos(ac), np.cos(ac)], axis=1)
    sin32 = np.concatenate([np.sin(ar), np.sin(ar), np.sin(ac), np.sin(ac)], axis=1)
    cos32 = np.concatenate([cos32, np.ones((TM_MIX, ROPE_DIM))], axis=0)
    sin32 = np.concatenate([sin32, np.zeros((TM_MIX, ROPE_DIM))], axis=0)
    n = cos32.shape[0]
    cosq = np.ones((n, N_HEADS, HEAD_SLOT))
    sinq = np.zeros((n, N_HEADS, HEAD_SLOT))
    cosq[:, :, NOPE_DIM:NOPE_DIM + ROPE_DIM] = cos32[:, None, :]
    sinq[:, :, NOPE_DIM:NOPE_DIM + ROPE_DIM] = sin32[:, None, :]
    return (cosq.reshape(n, QK_W).astype(np.float32), sinq.reshape(n, QK_W).astype(np.float32),
            cos32.astype(np.float32), sin32.astype(np.float32))


def _fnet_channel_table():
    j = np.arange(FN_GROUP_W)
    ang = 2.0 * np.pi * ((j[:, None] * j[None, :]) % FN_GROUP_W) / FN_GROUP_W
    out = np.zeros((FN_W, 2 * FN_W))
    for g in range(FN_GROUPS):
        sl = slice(g * FN_GROUP_W, (g + 1) * FN_GROUP_W)
        out[sl, sl] = np.cos(ang)
        out[sl, FN_W + g * FN_GROUP_W:FN_W + (g + 1) * FN_GROUP_W] = np.sin(ang)
    return out.astype(np.float32)


def _rope_expand_table():
    e = np.zeros((ROPE_DIM, N_HEADS, HEAD_SLOT), np.float32)
    for j in range(ROPE_DIM):
        e[j, :, NOPE_DIM + j] = 1.0
    return e.reshape(ROPE_DIM, QK_W)


_TABLES = {}


def _tables():
    if not _TABLES:
        for L in (SEQ, DEC_SEQ):
            _TABLES[("hy", L)] = _dft_tables(L, True)
            _TABLES[("fn", L)] = _dft_tables(L, False)
            _TABLES[("filt", L)] = _filter_tables(L)
        _TABLES["rope"] = _rope_tables()
        _TABLES["fnch"] = _fnet_channel_table()
        _TABLES["ropeexp"] = _rope_expand_table()
    return _TABLES


def _rms(x, g):
    ms = jnp.mean(x * x, axis=-1, keepdims=True)
    return x * lax.rsqrt(ms + EPS) * g


def _sigmoid(x):
    return 1.0 / (1.0 + jnp.exp(-x))


def _dot(a, b):
    return jnp.dot(a, b, preferred_element_type=F32)


def _dot_hi(a, b):
    return jnp.dot(a, b, precision=HIGHEST, preferred_element_type=F32)


def _dot_nt(a, b):
    return lax.dot_general(a, b, (((1,), (1,)), ((), ())), preferred_element_type=F32)


def _group_of(tok0):
    return jnp.where(tok0 < T_CTX, 0, 1 + (tok0 - T_CTX) // DEC_SEQ)


def _params(sem):
    return pltpu.CompilerParams(dimension_semantics=sem, vmem_limit_bytes=VMEM_LIMIT)


def _ada_kernel(c_ref, w_ref, b_ref, o_ref):
    x = c_ref[...]
    s = x * _sigmoid(x)
    o_ref[...] = _dot_hi(s, w_ref[...]) + b_ref[...]


def _ada_call(cvec, w_ada, b_ada):
    tn = 1024
    n_out = 9 * D_MODEL
    return pl.pallas_call(
        _ada_kernel,
        out_shape=jax.ShapeDtypeStruct((DEPTH, 8, n_out), F32),
        grid=(DEPTH, n_out // tn),
        in_specs=[
            pl.BlockSpec((8, D_MODEL), lambda l, j: (0, 0)),
            pl.BlockSpec((None, D_MODEL, tn), lambda l, j: (l, 0, j)),
            pl.BlockSpec((None, 1, tn), lambda l, j: (l, 0, j)),
        ],
        out_specs=pl.BlockSpec((None, 8, tn), lambda l, j: (l, 0, j)),
        compiler_params=_params(("arbitrary", "arbitrary")),
        name="ada_modulation",
    )(cvec, w_ada, b_ada.reshape(DEPTH, 1, n_out))


def _ffn_kernel(x_ref, mod_ref, ng_ref, wg_ref, wu_ref, wd_ref, *rest, j0, nk, final):
    if final:
        fg_ref, o_ref, n_sc, acc_sc = rest
    else:
        o_ref, n_sc, acc_sc = rest
    i = pl.program_id(0)
    k = pl.program_id(1)
    m = mod_ref[_group_of(i * TM_FFN)]

    @pl.when(k == 0)
    def _():
        n = _rms(x_ref[...], ng_ref[...]) * (1.0 + m[j0 + 1:j0 + 2]) + m[j0:j0 + 1]
        n_sc[...] = n.astype(BF16)
        acc_sc[...] = jnp.zeros_like(acc_sc)

    n = n_sc[...]
    g = _dot(n, wg_ref[...].astype(BF16))
    u = _dot(n, wu_ref[...].astype(BF16))
    hidden = (g * _sigmoid(g) * u).astype(BF16)
    acc_sc[...] += _dot(hidden, wd_ref[...].astype(BF16))

    @pl.when(k == nk - 1)
    def _():
        y = x_ref[...] + 0.5 * m[j0 + 2:j0 + 3] * acc_sc[...]
        if final:
            y = _rms(y, fg_ref[...])
        o_ref[...] = y


def _ffn_call(x, mods, ng, w_up, w_down, l, f, j0, final_g=None):
    nk = D_FF // TF_FFN
    final = final_g is not None
    in_specs = [
        pl.BlockSpec((TM_FFN, D_MODEL), lambda i, k: (i, 0)),
        pl.BlockSpec((N_GROUPS, 9, D_MODEL), lambda i, k: (0, 0, 0)),
        pl.BlockSpec((1, D_MODEL), lambda i, k: (0, 0)),
        pl.BlockSpec((None, None, D_MODEL, TF_FFN), lambda i, k: (l, f, 0, k)),
        pl.BlockSpec((None, None, D_MODEL, TF_FFN), lambda i, k: (l, f, 0, k + nk)),
        pl.BlockSpec((None, None, TF_FFN, D_MODEL), lambda i, k: (l, f, k, 0)),
    ]
    args = [x, mods, ng, w_up, w_up, w_down]
    if final:
        in_specs.append(pl.BlockSpec((1, D_MODEL), lambda i, k: (0, 0)))
        args.append(final_g)
    return pl.pallas_call(
        functools.partial(_ffn_kernel, j0=j0, nk=nk, final=final),
        out_shape=jax.ShapeDtypeStruct((T_ALL, D_MODEL), F32),
        grid=(T_ALL // TM_FFN, nk),
        in_specs=in_specs,
        out_specs=pl.BlockSpec((TM_FFN, D_MODEL), lambda i, k: (i, 0)),
        scratch_shapes=[pltpu.VMEM((TM_FFN, D_MODEL), BF16), pltpu.VMEM((TM_FFN, D_MODEL), F32)],
        compiler_params=_params(("parallel", "arbitrary")),
        name="swiglu_half_step",
    )(*args)


def _mixin_kernel(h_ref, mod_ref, ng_ref, w_ref, qg_ref, kvg_ref, wq_ref, wqs_ref, wk_ref, wv_ref, ek_ref,
                  fch_ref, cq_ref, sq_ref, ck_ref, sk_ref,
                  hy_ref, q_ref, k_ref, v_ref, ckv_ref, kr_ref, fcs_ref):
    i = pl.program_id(0)
    m = mod_ref[_group_of(i * TM_MIX)]
    n = (_rms(h_ref[...], ng_ref[...]) * (1.0 + m[4:5]) + m[3:4]).astype(BF16)
    proj = _dot(n, w_ref[...])
    hy_ref[...] = proj[:, MIX_HY:MIX_QA]
    q_a = proj[:, MIX_QA:MIX_KVA]
    kv_a = proj[:, MIX_KVA:MIX_FN]
    fn = proj[:, MIX_FN:MIX_KR]
    k_r = proj[:, MIX_KR:MIX_KR + ROPE_DIM]
    k_r_swapped = proj[:, MIX_KRS:MIX_KRS + ROPE_DIM]

    qn = _rms(q_a, qg_ref[...]).astype(BF16)
    q = _dot(qn, wq_ref[...]) * cq_ref[...] + _dot(qn, wqs_ref[...]) * sq_ref[...]
    q_ref[...] = q.astype(BF16)

    ckv = _rms(kv_a, kvg_ref[...])
    ckv_ref[...] = ckv
    kr_ref[...] = k_r
    ckv_b = ckv.astype(BF16)
    k_rot = (k_r * ck_ref[...] + k_r_swapped * sk_ref[...]).astype(BF16)
    k_ref[...] = (_dot(ckv_b, wk_ref[...]) + _dot(k_rot, ek_ref[...])).astype(BF16)
    v_ref[...] = _dot(ckv_b, wv_ref[...]).astype(BF16)
    fcs_ref[...] = _dot(fn.astype(BF16), fch_ref[...]).astype(BF16)


def _mixin_call(h, mods, ng, w_mix, qg, kvg, wq, wqs, wk, wv, ek, fch, cosq, sinq, cosk, sink):
    tm = TM_MIX

    def pos_block(i):
        tok0 = i * tm
        return jnp.where(tok0 < T_CTX, DEC_SEQ // tm, ((tok0 - T_CTX) % DEC_SEQ) // tm)

    def full(shape):
        return pl.BlockSpec(shape, lambda i: (0,) * len(shape))

    def tok(width):
        return pl.BlockSpec((tm, width), lambda i: (i, 0))

    def pos(width):
        return pl.BlockSpec((tm, width), lambda i: (pos_block(i), 0))

    out_shape = (
        jax.ShapeDtypeStruct((T_ALL, 3 * HY_W), F32),
        jax.ShapeDtypeStruct((T_ALL, QK_W), BF16),
        jax.ShapeDtypeStruct((T_ALL, QK_W), BF16),
        jax.ShapeDtypeStruct((T_ALL, N_HEADS * V_DIM), BF16),
        jax.ShapeDtypeStruct((T_ALL, KV_LORA), F32),
        jax.ShapeDtypeStruct((T_ALL, ROPE_DIM), F32),
        jax.ShapeDtypeStruct((T_ALL, 2 * FN_W), BF16),
    )
    return pl.pallas_call(
        _mixin_kernel,
        out_shape=out_shape,
        grid=(T_ALL // tm,),
        in_specs=[
            tok(D_MODEL), full((N_GROUPS, 9, D_MODEL)), full((1, D_MODEL)), full((D_MODEL, MIX_W)),
            full((1, Q_LORA)), full((1, KV_LORA)), full((Q_LORA, QK_W)), full((Q_LORA, QK_W)),
            full((KV_LORA, QK_W)), full((KV_LORA, N_HEADS * V_DIM)), full((ROPE_DIM, QK_W)),
            full((FN_W, 2 * FN_W)), pos(QK_W), pos(QK_W), pos(ROPE_DIM), pos(ROPE_DIM),
        ],
        out_specs=tuple(tok(s.shape[1]) for s in out_shape),
        compiler_params=_params(("parallel",)),
        name="mixer_input_proj",
    )(h, mods, ng, w_mix, qg, kvg, wq, wqs, wk, wv, ek, fch, cosq, sinq, cosk, sink)


def _cachekv_kernel(ckv_ref, kr_ref, wk_ref, wv_ref, ek_ref, k_ref, v_ref):
    ckv_b = ckv_ref[...].astype(BF16)
    k_ref[...] = (_dot(ckv_b, wk_ref[...]) + _dot(kr_ref[...].astype(BF16), ek_ref[...])).astype(BF16)
    v_ref[...] = _dot(ckv_b, wv_ref[...]).astype(BF16)


def _cachekv_call(cache_ckv, cache_krope, wk, wv, ek, l):
    def full(shape):
        return pl.BlockSpec(shape, lambda b: (0,) * len(shape))

    return pl.pallas_call(
        _cachekv_kernel,
        out_shape=(jax.ShapeDtypeStruct((DEC_BATCH, PAST_LEN, QK_W), BF16),
                   jax.ShapeDtypeStruct((DEC_BATCH, PAST_LEN, N_HEADS * V_DIM), BF16)),
        grid=(DEC_BATCH,),
        in_specs=[
            pl.BlockSpec((None, None, PAST_LEN, KV_LORA), lambda b: (b, l, 0, 0)),
            pl.BlockSpec((None, None, PAST_LEN, ROPE_DIM), lambda b: (b, l, 0, 0)),
            full((KV_LORA, QK_W)), full((KV_LORA, N_HEADS * V_DIM)), full((ROPE_DIM, QK_W)),
        ],
        out_specs=(pl.BlockSpec((None, PAST_LEN, QK_W), lambda b: (b, 0, 0)),
                   pl.BlockSpec((None, PAST_LEN, N_HEADS * V_DIM), lambda b: (b, 0, 0))),
        compiler_params=_params(("parallel",)),
        name="cached_context_kv",
    )(cache_ckv, cache_krope, wk, wv, ek)


def _attn_kernel(q_ref, k_ref, v_ref, *rest, has_cache):
    if has_cache:
        kc_ref, vc_ref, o_ref = rest
    else:
        (o_ref,) = rest
    scale = 1.0 / math.sqrt(NOPE_DIM + ROPE_DIM)
    first_half = lax.broadcasted_iota(jnp.int32, (1, 2 * V_DIM), 1) < V_DIM
    for pair in range(N_HEADS // 2):
        vs = slice(pair * 2 * V_DIM, (pair + 1) * 2 * V_DIM)
        outs = []
        for e in range(2):
            hs = slice((2 * pair + e) * HEAD_SLOT, (2 * pair + e + 1) * HEAD_SLOT)
            qh = q_ref[:, hs]
            s = _dot_nt(qh, k_ref[:, hs])
            mx = jnp.max(s, axis=-1, keepdims=True)
            if has_cache:
                sc = _dot_nt(qh, kc_ref[:, hs])
                mx = jnp.maximum(mx, jnp.max(sc, axis=-1, keepdims=True))
            p = jnp.exp((s - mx) * scale)
            denom = jnp.sum(p, axis=-1, keepdims=True)
            o = _dot(p.astype(BF16), v_ref[:, vs])
            if has_cache:
                pc = jnp.exp((sc - mx) * scale)
                denom = denom + jnp.sum(pc, axis=-1, keepdims=True)
                o = o + _dot(pc.astype(BF16), vc_ref[:, vs])
            outs.append(o / denom)
        o_ref[:, vs] = jnp.where(first_half, outs[0], outs[1]).astype(BF16)


def _attn_call(q, k, v, kc, vc, *, n_seq, seq_len, tok_off, tq):
    has_cache = kc is not None
    nq = seq_len // tq
    seq_blk0 = tok_off // seq_len
    q_blk0 = tok_off // tq
    vw = N_HEADS * V_DIM
    in_specs = [
        pl.BlockSpec((tq, QK_W), lambda b, j: (q_blk0 + b * nq + j, 0)),
        pl.BlockSpec((seq_len, QK_W), lambda b, j: (seq_blk0 + b, 0)),
        pl.BlockSpec((seq_len, vw), lambda b, j: (seq_blk0 + b, 0)),
    ]
    args = [q, k, v]
    if has_cache:
        in_specs += [pl.BlockSpec((None, PAST_LEN, QK_W), lambda b, j: (b, 0, 0)),
                     pl.BlockSpec((None, PAST_LEN, vw), lambda b, j: (b, 0, 0))]
        args += [kc, vc]
    return pl.pallas_call(
        functools.partial(_attn_kernel, has_cache=has_cache),
        out_shape=jax.ShapeDtypeStruct((n_seq * seq_len, vw), BF16),
        grid=(n_seq, nq),
        in_specs=in_specs,
        out_specs=pl.BlockSpec((tq, vw), lambda b, j: (b * nq + j, 0)),
        compiler_params=_params(("parallel", "parallel")),
        name="mla_attention_cache" if has_cache else "mla_attention",
    )(*args)


def _hyfilt_kernel(feat_ref, w1_ref, b1_ref, w2_ref, b2_ref, w3_ref, dec_ref, alt_ref, c_ref, s_ref,
                   hre_ref, him_ref, nyq_ref, a_sc, b_sc, *, L):
    j = pl.program_id(0)
    half = HY_ORDER * HY_W

    @pl.when(j == 0)
    def _():
        h = jnp.sin(_dot_hi(feat_ref[...], w1_ref[...]) + b1_ref[...])
        h = jnp.sin(_dot_hi(h, w2_ref[...]) + b2_ref[...])
        h = _dot_hi(h, w3_ref[...])
        dec = dec_ref[...]
        rows = lax.broadcasted_iota(jnp.int32, (L, 1), 0)
        h_fwd = h[:, :half] * dec
        h_bwd = jnp.where(rows > 0, h[:, half:] * dec, 0.0)
        norm = (jnp.sum(jnp.abs(h_fwd), axis=0, keepdims=True)
                + jnp.sum(jnp.abs(h_bwd), axis=0, keepdims=True) + EPS)
        inv = 1.0 / norm
        a = (h_fwd + h_bwd) * inv
        b = (h_fwd - h_bwd) * inv
        a_sc[...] = a.astype(BF16)
        b_sc[...] = b.astype(BF16)
        nyq_ref[...] = jnp.sum(a * alt_ref[...], axis=0, keepdims=True)

    hre_ref[...] = _dot(c_ref[...], a_sc[...])
    him_ref[...] = -_dot(s_ref[...], b_sc[...])


def _hyfilt_call(L, feats, w1, b1, w2, b2, w3, decay, alt, ctab, stab):
    tk = min(L, 512)
    half = HY_ORDER * HY_W

    def full(shape):
        return pl.BlockSpec(shape, lambda j: (0,) * len(shape))

    return pl.pallas_call(
        functools.partial(_hyfilt_kernel, L=L),
        out_shape=(jax.ShapeDtypeStruct((L, half), F32), jax.ShapeDtypeStruct((L, half), F32),
                   jax.ShapeDtypeStruct((1, half), F32)),
        grid=(L // tk,),
        in_specs=[
            full((L, POS_PAD)), full((POS_PAD, FILT_HID)), full((1, FILT_HID)), full((FILT_HID, FILT_HID)),
            full((1, FILT_HID)), full((FILT_HID, 2 * half)), full((L, half)), full((L, 1)),
            pl.BlockSpec((tk, L), lambda j: (j, 0)), pl.BlockSpec((tk, L), lambda j: (j, 0)),
        ],
        out_specs=(pl.BlockSpec((tk, half), lambda j: (j, 0)), pl.BlockSpec((tk, half), lambda j: (j, 0)),
                   full((1, half))),
        scratch_shapes=[pltpu.VMEM((L, half), BF16), pltpu.VMEM((L, half), BF16)],
        compiler_params=_params(("arbitrary",)),
        name="hyena_filter_spectrum_%d" % L,
    )(feats, w1, b1, w2, b2, w3, decay, alt, ctab, stab)


def _hyconv_kernel(hy_ref, cw_ref, cb_ref, skip_ref, hre_ref, him_ref, nyq_ref, alt_ref,
                   cr_ref, sr_ref, cc_ref, sc_ref, o_ref,
                   x1_sc, x2_sc, v_sc, vb_sc, acc_sc, nv_sc, *, L, tk, nk):
    p = pl.program_id(1)
    j = pl.program_id(2)

    @pl.when((p == 0) & (j == 0))
    def _():
        x = hy_ref[...]
        rows = lax.broadcasted_iota(jnp.int32, (L, 1), 0)
        prev = jnp.where(rows > 0, pltpu.roll(x, 1, 0), 0.0)
        nxt = jnp.where(rows < L - 1, pltpu.roll(x, L - 1, 0), 0.0)
        w = cw_ref[...]
        u = prev * w[0:1] + x * w[1:2] + nxt * w[2:3] + cb_ref[...]
        x1_sc[...] = u[:, :HY_W]
        x2_sc[...] = u[:, HY_W:2 * HY_W]
        v_sc[...] = u[:, 2 * HY_W:]

    @pl.when(j == 0)
    def _():
        v = v_sc[...]
        vb_sc[...] = v.astype(BF16)
        acc_sc[...] = jnp.zeros_like(acc_sc)
        nv_sc[...] = jnp.sum(v * alt_ref[...], axis=0, keepdims=True)

    vb = vb_sc[...]
    v_re = _dot(cr_ref[...], vb)
    v_im = -_dot(sr_ref[...], vb)
    h_re = hre_ref[...]
    h_im = him_ref[...]
    freq = j * tk + lax.broadcasted_iota(jnp.int32, (tk, 1), 0)
    wk = jnp.where(freq == 0, 0.5, 1.0)
    y_re = ((v_re * h_re - v_im * h_im) * wk).astype(BF16)
    y_im = ((v_re * h_im + v_im * h_re) * wk).astype(BF16)
    acc_sc[...] += _dot(cc_ref[...], y_re) - _dot(sc_ref[...], y_im)

    @pl.when(j == nk - 1)
    def _():
        v = v_sc[...]
        nyq = nv_sc[...] * nyq_ref[...]
        y = acc_sc[...] * (1.0 / L) + (0.5 / L) * alt_ref[...] * nyq + v * skip_ref[...]

        @pl.when(p == 0)
        def _():
            v_sc[...] = x1_sc[...] * y

        @pl.when(p == 1)
        def _():
            o_ref[...] = (x2_sc[...] * y).astype(BF16)


def _hyconv_call(hy, conv_w, conv_b, skip, hre, him, nyq, alt, ctab, stab, *, n_seq, L, tok_off):
    tk = min(L, 256)
    nk = L // tk
    seq_blk0 = tok_off // L

    def full(shape):
        return pl.BlockSpec(shape, lambda b, p, j: (0,) * len(shape))

    vm = lambda: pltpu.VMEM((L, HY_W), F32)
    return pl.pallas_call(
        functools.partial(_hyconv_kernel, L=L, tk=tk, nk=nk),
        out_shape=jax.ShapeDtypeStruct((n_seq * L, HY_W), BF16),
        grid=(n_seq, HY_ORDER, nk),
        in_specs=[
            pl.BlockSpec((L, 3 * HY_W), lambda b, p, j: (seq_blk0 + b, 0)),
            full((3, 3 * HY_W)), full((1, 3 * HY_W)),
            pl.BlockSpec((None, 1, HY_W), lambda b, p, j: (p, 0, 0)),
            pl.BlockSpec((tk, HY_W), lambda b, p, j: (j, p)),
            pl.BlockSpec((tk, HY_W), lambda b, p, j: (j, p)),
            pl.BlockSpec((None, 1, HY_W), lambda b, p, j: (p, 0, 0)),
            full((L, 1)),
            pl.BlockSpec((tk, L), lambda b, p, j: (j, 0)), pl.BlockSpec((tk, L), lambda b, p, j: (j, 0)),
            pl.BlockSpec((L, tk), lambda b, p, j: (0, j)), pl.BlockSpec((L, tk), lambda b, p, j: (0, j)),
        ],
        out_specs=pl.BlockSpec((L, HY_W), lambda b, p, j: (b, 0)),
        scratch_shapes=[vm(), vm(), vm(), pltpu.VMEM((L, HY_W), BF16), vm(), pltpu.VMEM((1, HY_W), F32)],
        compiler_params=_params(("arbitrary", "arbitrary", "arbitrary")),
        name="hyena_long_conv_%d" % L,
    )(hy, conv_w, conv_b, skip, hre, him, nyq, alt, ctab, stab, ctab, stab)


def _fnet_kernel(fcs_ref, c_ref, s_ref, o_ref, *, scale):
    y = _dot(c_ref[...], fcs_ref[:, :FN_W]) - _dot(s_ref[...], fcs_ref[:, FN_W:])
    o_ref[...] = (y * scale).astype(BF16)


def _fnet_call(fcs, ctab, stab, *, n_seq, L, tok_off):
    tk = min(L, 512)
    nk = L // tk
    seq_blk0 = tok_off // L
    return pl.pallas_call(
        functools.partial(_fnet_kernel, scale=1.0 / math.sqrt(L * FN_GROUP_W)),
        out_shape=jax.ShapeDtypeStruct((n_seq * L, FN_W), BF16),
        grid=(n_seq, nk),
        in_specs=[
            pl.BlockSpec((L, 2 * FN_W), lambda b, j: (seq_blk0 + b, 0)),
            pl.BlockSpec((tk, L), lambda b, j: (j, 0)), pl.BlockSpec((tk, L), lambda b, j: (j, 0)),
        ],
        out_specs=pl.BlockSpec((tk, FN_W), lambda b, j: (b * nk + j, 0)),
        compiler_params=_params(("parallel", "parallel")),
        name="fnet_position_dft_%d" % L,
    )(fcs, ctab, stab)


def _mixout_kernel(h_ref, mod_ref, ng_ref, wg_ref, z_ref, a_ref, f_ref, wa_ref, wb_ref, wc_ref, wo_ref, o_ref):
    i = pl.program_id(0)
    m = mod_ref[_group_of(i * TM_MIX)]
    h = h_ref[...]
    n = (_rms(h, ng_ref[...]) * (1.0 + m[4:5]) + m[3:4]).astype(BF16)
    acc = _sigmoid(_dot(n, wg_ref[:, 0:D_MODEL])) * _dot(z_ref[...], wa_ref[...])
    acc = acc + _sigmoid(_dot(n, wg_ref[:, D_MODEL:2 * D_MODEL])) * _dot(a_ref[...], wb_ref[...])
    acc = acc + _sigmoid(_dot(n, wg_ref[:, 2 * D_MODEL:])) * _dot(f_ref[...], wc_ref[...])
    y = _dot(acc.astype(BF16), wo_ref[...])
    o_ref[...] = h + m[5:6] * y


def _mixout_call(h, mods, ng, w_gate, z, a, f, wa, wb, wc, wo):
    tm = TM_MIX

    def full(shape):
        return pl.BlockSpec(shape, lambda i: (0,) * len(shape))

    def tok(width):
        return pl.BlockSpec((tm, width), lambda i: (i, 0))

    return pl.pallas_call(
        _mixout_kernel,
        out_shape=jax.ShapeDtypeStruct((T_ALL, D_MODEL), F32),
        grid=(T_ALL // tm,),
        in_specs=[
            tok(D_MODEL), full((N_GROUPS, 9, D_MODEL)), full((1, D_MODEL)), full((D_MODEL, N_BRANCH * D_MODEL)),
            tok(HY_W), tok(N_HEADS * V_DIM), tok(FN_W),
            full((HY_W, D_MODEL)), full((N_HEADS * V_DIM, D_MODEL)), full((FN_W, D_MODEL)),
            full((D_MODEL, D_MODEL)),
        ],
        out_specs=tok(D_MODEL),
        compiler_params=_params(("parallel",)),
        name="gated_merge_out_proj",
    )(h, mods, ng, w_gate, z, a, f, wa, wb, wc, wo)


def _layer_weights(l, w_in, w_qb, w_kvb):
    w = w_in[l]
    i0 = 3 * HY_W
    i1 = i0 + Q_LORA
    i2 = i1 + KV_LORA
    i3 = i2 + ROPE_DIM
    i4 = i3 + FN_W
    sign = jnp.asarray(_ROPE_SIGN)
    w_kr = w[:, i2:i3]
    w_kr_sw = w_kr[:, _ROPE_PERM] * sign
    pad = ((0, 0), (0, 128 - ROPE_DIM))
    w_mix = jnp.concatenate([w[:, :i2], w[:, i3:i4], jnp.pad(w_kr, pad), jnp.pad(w_kr_sw, pad)], axis=1).astype(BF16)
    w_gate = w[:, i4:].astype(BF16)

    wq3 = w_qb[l].reshape(Q_LORA, N_HEADS, NOPE_DIM + ROPE_DIM)
    slot_pad = HEAD_SLOT - NOPE_DIM - ROPE_DIM
    wq = jnp.pad(wq3, ((0, 0), (0, 0), (0, slot_pad))).reshape(Q_LORA, QK_W).astype(BF16)
    wq_sw = wq3[:, :, NOPE_DIM:][:, :, _ROPE_PERM] * sign
    wqs = jnp.pad(wq_sw, ((0, 0), (0, 0), (NOPE_DIM, slot_pad))).reshape(Q_LORA, QK_W).astype(BF16)

    wkv3 = w_kvb[l].reshape(KV_LORA, N_HEADS, NOPE_DIM + V_DIM)
    wk = jnp.pad(wkv3[:, :, :NOPE_DIM], ((0, 0), (0, 0), (0, HEAD_SLOT - NOPE_DIM))).reshape(KV_LORA, QK_W).astype(BF16)
    wv = wkv3[:, :, NOPE_DIM:].reshape(KV_LORA, N_HEADS * V_DIM).astype(BF16)
    return w_mix, w_gate, wq, wqs, wk, wv


def kernel(x_prompt, x_sample, cache_ckv, cache_krope, c, c_ctx, w_ada, b_ada, norm_g, w_ffn_up, w_ffn_down,
           w_in, hy_conv_w, hy_conv_b, hy_filt_w1, hy_filt_b1, hy_filt_w2, hy_filt_b2, hy_filt_w3, hy_skip,
           w_hy_out, q_norm_g, w_qb, kv_norm_g, w_kvb, w_mla_o, w_fnet, w_out, final_g):
    tabs = _tables()
    cosq, sinq, cosk, sink = (jnp.asarray(t) for t in tabs["rope"])
    fch = jnp.asarray(tabs["fnch"]).astype(BF16)
    ek = jnp.asarray(tabs["ropeexp"]).astype(BF16)
    dft = {}
    for L in (SEQ, DEC_SEQ):
        dft[("hy", L)] = tuple(jnp.asarray(t).astype(BF16) for t in tabs[("hy", L)])
        dft[("fn", L)] = tuple(jnp.asarray(t).astype(BF16) for t in tabs[("fn", L)])

    cvec = jnp.concatenate([c_ctx[None, :], c, jnp.zeros((8 - N_GROUPS, D_MODEL), F32)], axis=0)
    ada = _ada_call(cvec, w_ada, b_ada)
    mods_all = ada[:, :N_GROUPS].reshape(DEPTH, N_GROUPS, 9, D_MODEL)

    h = jnp.concatenate([x_prompt.reshape(T_CTX, D_MODEL), x_sample.reshape(T_LAT, D_MODEL)], axis=0)
    segs = ((BATCH, SEQ, 0), (DEC_BATCH, DEC_SEQ, T_CTX))
    ckv_out = []
    kr_out = []
    for l in range(DEPTH):
        mods = mods_all[l]
        ng = norm_g[l]
        w_mix, w_gate, wq, wqs, wk, wv = _layer_weights(l, w_in, w_qb, w_kvb)

        h = _ffn_call(h, mods, ng[0:1], w_ffn_up, w_ffn_down, l, 0, 0)

        hy, q, k, v, ckv, k_r, fcs = _mixin_call(
            h, mods, ng[1:2], w_mix, q_norm_g[l][None, :], kv_norm_g[l][None, :], wq, wqs, wk, wv, ek, fch,
            cosq, sinq, cosk, sink)
        ckv_out.append(ckv[:T_CTX].reshape(BATCH, SEQ, KV_LORA))
        kr_out.append(k_r[:T_CTX].reshape(BATCH, SEQ, ROPE_DIM))

        kc, vc = _cachekv_call(cache_ckv, cache_krope, wk, wv, ek, l)

        w1 = jnp.pad(hy_filt_w1[l], ((0, POS_PAD - POS_EMB), (0, 0)))
        skip = hy_skip[l].reshape(HY_ORDER, 1, HY_W)
        z_parts, a_parts, f_parts = [], [], []
        for n_seq, L, off in segs:
            feats, decay, alt = (jnp.asarray(t) for t in tabs[("filt", L)])
            c_hy, s_hy = dft[("hy", L)]
            hre, him, nyq = _hyfilt_call(L, feats, w1, hy_filt_b1[l][None, :], hy_filt_w2[l],
                                         hy_filt_b2[l][None, :], hy_filt_w3[l], decay, alt, c_hy, s_hy)
            z_parts.append(_hyconv_call(hy, hy_conv_w[l], hy_conv_b[l][None, :], skip, hre, him,
                                        nyq.reshape(HY_ORDER, 1, HY_W), alt, c_hy, s_hy,
                                        n_seq=n_seq, L=L, tok_off=off))
            cache = (kc, vc) if off else (None, None)
            a_parts.append(_attn_call(q, k, v, *cache, n_seq=n_seq, seq_len=L, tok_off=off, tq=min(L, 512)))
            c_fn, s_fn = dft[("fn", L)]
            f_parts.append(_fnet_call(fcs, c_fn, s_fn, n_seq=n_seq, L=L, tok_off=off))
        z = jnp.concatenate(z_parts, axis=0)
        a = jnp.concatenate(a_parts, axis=0)
        f = jnp.concatenate(f_parts, axis=0)

        h = _mixout_call(h, mods, ng[1:2], w_gate, z, a, f, w_hy_out[l].astype(BF16), w_mla_o[l].astype(BF16),
                         w_fnet[l].astype(BF16), w_out[l].astype(BF16))

        last = l == DEPTH - 1
        h = _ffn_call(h, mods, ng[2:3], w_ffn_up, w_ffn_down, l, 1, 6, final_g[None, :] if last else None)

    y_prompt = h[:T_CTX].reshape(BATCH, SEQ, D_MODEL)
    y_sample = h[T_CTX:].reshape(DEC_BATCH, DEC_SEQ, D_MODEL)
    return y_prompt, y_sample, jnp.stack(ckv_out, axis=1), jnp.stack(kr_out, axis=1)
```

```python
import functools
import math

import numpy as np
import jax
import jax.numpy as jnp
from jax import lax
from jax.experimental import pallas as pl
from jax.experimental.pallas import tpu as pltpu

F32 = jnp.float32
BF16 = jnp.bfloat16
HIGHEST = lax.Precision.HIGHEST

D_MODEL = 1024
BATCH = 16
SEQ = 256
DEPTH = 2
DEC_BATCH = 2
DEC_SEQ = 2048
PAST_LEN = 512
GRID_W = 64
HY_W = 256
HY_ORDER = 2
N_BANDS = 8
POS_EMB = 1 + 2 * N_BANDS
FILT_HID = 64
N_HEADS = 8
Q_LORA = 256
KV_LORA = 128
NOPE_DIM = 64
ROPE_DIM = 32
V_DIM = 64
ROPE_BASE = 10000.0
FN_GROUPS = 4
FN_GROUP_W = 64
FN_W = FN_GROUPS * FN_GROUP_W
N_BRANCH = 3
D_FF = 2816
EPS = 1e-6

T_CTX = BATCH * SEQ
T_LAT = DEC_BATCH * DEC_SEQ
T_ALL = T_CTX + T_LAT
N_GROUPS = 1 + DEC_BATCH
HEAD_SLOT = 128
QK_W = N_HEADS * HEAD_SLOT
POS_PAD = 128

VMEM_LIMIT = 56 * 1024 * 1024

TM_FFN = 512
TF_FFN = 256
FFN_LOAD = D_FF // TF_FFN
TM_MIX = 512
TR_PREP = 128

IN_KR = 3 * HY_W + Q_LORA + KV_LORA
IN_FN = IN_KR + ROPE_DIM
IN_GATE = IN_FN + FN_W
IN_COLS = IN_GATE + N_BRANCH * D_MODEL

MIX_HY = 0
MIX_QA = 3 * HY_W
MIX_KVA = MIX_QA + Q_LORA
MIX_FN = MIX_KVA + KV_LORA
MIX_KR = MIX_FN + FN_W
MIX_KRS = MIX_KR + 128
MIX_W = MIX_KRS + 128

_ROPE_PERM = np.array(list(range(8, 16)) + list(range(0, 8)) + list(range(24, 32)) + list(range(16, 24)))
_ROPE_SIGN = np.array([-1.0] * 8 + [1.0] * 8 + [-1.0] * 8 + [1.0] * 8, np.float32)


def _dft_tables(L, half):
    k = np.arange(L, dtype=np.int64)
    period = 2 * L if half else L
    m = (k[:, None] * k[None, :]) % period
    ang = 2.0 * np.pi * m.astype(np.float64) / period
    return np.cos(ang).astype(np.float32), np.sin(ang).astype(np.float32)


def _filter_tables(L):
    t = np.arange(L, dtype=np.float64)
    t_norm = t / (L - 1)
    w = 2.0 * np.pi * t / L
    bands = np.linspace(1e-4, N_BANDS - 1, N_BANDS)
    ang = w[:, None] * bands[None, :]
    feats = np.concatenate([t_norm[:, None], np.cos(ang), -np.sin(ang)], axis=-1)
    feats = np.pad(feats, ((0, 0), (0, POS_PAD - POS_EMB)))
    deltas = np.linspace(math.log(1e-2) / 1.5, math.log(1e-2) / 0.3, HY_W)
    decay = np.exp(-t_norm[:, None] * np.abs(deltas)[None, :])
    decay = np.concatenate([decay, decay], axis=1)
    alt = np.where(np.arange(L) % 2 == 0, 1.0, -1.0)[:, None]
    return feats.astype(np.float32), decay.astype(np.float32), alt.astype(np.float32)


def _rope_tables():
    t = np.arange(DEC_SEQ)
    row = (t // GRID_W).astype(np.float64)
    col = (t % GRID_W).astype(np.float64)
    nf = ROPE_DIM // 4
    inv = ROPE_BASE ** (-np.arange(nf, dtype=np.float64) / nf)
    ar = row[:, None] * inv[None, :]
    ac = col[:, None] * inv[None, :]
    cos32 = np.concatenate([np.cos(ar), np.cos(ar), np.cos(ac), np.cos(ac)], axis=1)
    sin32 = np.concatenate([np.sin(ar), np.sin(ar), np.sin(ac), np.sin(ac)], axis=1)
    cos32 = np.concatenate([cos32, np.ones((TM_MIX, ROPE_DIM))], axis=0)
    sin32 = np.concatenate([sin32, np.zeros((TM_MIX, ROPE_DIM))], axis=0)
    n = cos32.shape[0]
    cosq = np.ones((n, N_HEADS, HEAD_SLOT))
    sinq = np.zeros((n, N_HEADS, HEAD_SLOT))
    cosq[:, :, NOPE_DIM:NOPE_DIM + ROPE_DIM] = cos32[:, None, :]
    sinq[:, :, NOPE_DIM:NOPE_DIM + ROPE_DIM] = sin32[:, None, :]
    return (cosq.reshape(n, QK_W).astype(np.float32), sinq.reshape(n, QK_W).astype(np.float32),
            cos32.astype(np.float32), sin32.astype(np.float32))


def _fnet_channel_table():
    j = np.arange(FN_GROUP_W)
    ang = 2.0 * np.pi * ((j[:, None] * j[None, :]) % FN_GROUP_W) / FN_GROUP_W
    out = np.zeros((FN_W, 2 * FN_W))
    for g in range(FN_GROUPS):
        sl = slice(g * FN_GROUP_W, (g + 1) * FN_GROUP_W)
        out[sl, sl] = np.cos(ang)
        out[sl, FN_W + g * FN_GROUP_W:FN_W + (g + 1) * FN_GROUP_W] = np.sin(ang)
    return out.astype(np.float32)


def _rope_expand_table():
    e = np.zeros((ROPE_DIM, N_HEADS, HEAD_SLOT), np.float32)
    for j in range(ROPE_DIM):
        e[j, :, NOPE_DIM + j] = 1.0
    return e.reshape(ROPE_DIM, QK_W)


_TABLES = {}


def _tables():
    if not _TABLES:
        for L in (SEQ, DEC_SEQ):
            _TABLES[("hy", L)] = _dft_tables(L, True)
            _TABLES[("fn", L)] = _dft_tables(L, False)
            _TABLES[("filt", L)] = _filter_tables(L)
        _TABLES["rope"] = _rope_tables()
        _TABLES["fnch"] = _fnet_channel_table()
        _TABLES["ropeexp"] = _rope_expand_table()
    return _TABLES


def _rms(x, g):
    ms = jnp.mean(x * x, axis=-1, keepdims=True)
    return x * lax.rsqrt(ms + EPS) * g


def _sigmoid(x):
    return 1.0 / (1.0 + jnp.exp(-x))


def _dot(a, b):
    return jnp.dot(a, b, preferred_element_type=F32)


def _dot_hi(a, b):
    return jnp.dot(a, b, precision=HIGHEST, preferred_element_type=F32)


def _dot_nt(a, b):
    return lax.dot_general(a, b, (((1,), (1,)), ((), ())), preferred_element_type=F32)


def _group_of(tok0):
    return jnp.where(tok0 < T_CTX, 0, 1 + (tok0 - T_CTX) // DEC_SEQ)


def _params(sem):
    return pltpu.CompilerParams(dimension_semantics=sem, vmem_limit_bytes=VMEM_LIMIT)


def _ada_kernel(c_ref, w_ref, b_ref, o_ref):
    x = c_ref[...]
    s = x * _sigmoid(x)
    o_ref[...] = _dot_hi(s, w_ref[...]) + b_ref[...]


def _ada_call(cvec, w_ada, b_ada):
    tn = 1024
    n_out = 9 * D_MODEL
    return pl.pallas_call(
        _ada_kernel,
        out_shape=jax.ShapeDtypeStruct((DEPTH, 8, n_out), F32),
        grid=(DEPTH, n_out // tn),
        in_specs=[
            pl.BlockSpec((8, D_MODEL), lambda l, j: (0, 0)),
            pl.BlockSpec((None, D_MODEL, tn), lambda l, j: (l, 0, j)),
            pl.BlockSpec((None, 1, tn), lambda l, j: (l, 0, j)),
        ],
        out_specs=pl.BlockSpec((None, 8, tn), lambda l, j: (l, 0, j)),
        compiler_params=_params(("arbitrary", "arbitrary")),
        name="ada_modulation",
    )(cvec, w_ada, b_ada.reshape(DEPTH, 1, n_out))


def _ffn_kernel(*refs, j0, split_in, final):
    refs = list(refs)
    x_refs = [refs.pop(0) for _ in range(2 if split_in else 1)]
    mod_ref, ng_ref, wg_ref, wu_ref, wd_ref = refs[:5]
    refs = refs[5:]
    fg_ref = refs.pop(0) if final else None
    o_refs = [refs.pop(0) for _ in range(2 if final else 1)]
    wg_sc, wu_sc, wd_sc, hid_sc = refs
    s = pl.program_id(0)
    n_ctx_tiles = T_CTX // TM_FFN

    @pl.when(s < FFN_LOAD)
    def _():
        wg_sc[s] = wg_ref[...].astype(BF16)
        wu_sc[s] = wu_ref[...].astype(BF16)
        wd_sc[pl.ds(pl.multiple_of(s * TF_FFN, TF_FFN), TF_FFN), :] = wd_ref[...].astype(BF16)

    @pl.when(s >= FFN_LOAD)
    def _():
        t = s - FFN_LOAD
        if split_in:
            x = jnp.where(t < n_ctx_tiles, x_refs[0][...], x_refs[1][...])
        else:
            x = x_refs[0][...]
        m = mod_ref[_group_of(t * TM_FFN)]
        n = (_rms(x, ng_ref[...]) * (1.0 + m[j0 + 1:j0 + 2]) + m[j0:j0 + 1]).astype(BF16)
        for c in range(FFN_LOAD):
            g = _dot(n, wg_sc[c])
            u = _dot(n, wu_sc[c])
            hid_sc[:, c * TF_FFN:(c + 1) * TF_FFN] = (g * _sigmoid(g) * u).astype(BF16)
        y = x + 0.5 * m[j0 + 2:j0 + 3] * _dot(hid_sc[...], wd_sc[...])
        if final:
            y = _rms(y, fg_ref[...])

            @pl.when(t < n_ctx_tiles)
            def _():
                o_refs[0][...] = y

            @pl.when(t >= n_ctx_tiles)
            def _():
                o_refs[1][...] = y
        else:
            o_refs[0][...] = y


def _ffn_call(xs, mods, ng, w_up, w_down, l, f, j0, final_g=None):
    split_in = len(xs) == 2
    final = final_g is not None
    n_ctx_tiles = T_CTX // TM_FFN
    tile = lambda s: jnp.maximum(s - FFN_LOAD, 0)
    chunk = lambda s: jnp.minimum(s, FFN_LOAD - 1)
    ctx_blk = lambda s: (jnp.minimum(tile(s), n_ctx_tiles - 1), 0)
    lat_blk = lambda s: (jnp.maximum(tile(s) - n_ctx_tiles, 0), 0)
    row = pl.BlockSpec((1, D_MODEL), lambda s: (0, 0))
    tok = lambda index_map: pl.BlockSpec((TM_FFN, D_MODEL), index_map)
    if split_in:
        in_specs = [tok(ctx_blk), tok(lat_blk)]
    else:
        in_specs = [tok(lambda s: (tile(s), 0))]
    in_specs += [
        pl.BlockSpec((N_GROUPS, 9, D_MODEL), lambda s: (0, 0, 0)),
        row,
        pl.BlockSpec((None, None, D_MODEL, TF_FFN), lambda s: (l, f, 0, chunk(s))),
        pl.BlockSpec((None, None, D_MODEL, TF_FFN), lambda s: (l, f, 0, chunk(s) + FFN_LOAD)),
        pl.BlockSpec((None, None, TF_FFN, D_MODEL), lambda s: (l, f, chunk(s), 0)),
    ]
    args = list(xs) + [mods, ng, w_up, w_up, w_down]
    if final:
        in_specs.append(row)
        args.append(final_g)
        out_shape = (jax.ShapeDtypeStruct((T_CTX, D_MODEL), F32), jax.ShapeDtypeStruct((T_LAT, D_MODEL), F32))
        out_specs = (tok(ctx_blk), tok(lat_blk))
    else:
        out_shape = jax.ShapeDtypeStruct((T_ALL, D_MODEL), F32)
        out_specs = tok(lambda s: (tile(s), 0))
    return pl.pallas_call(
        functools.partial(_ffn_kernel, j0=j0, split_in=split_in, final=final),
        out_shape=out_shape,
        grid=(FFN_LOAD + T_ALL // TM_FFN,),
        in_specs=in_specs,
        out_specs=out_specs,
        scratch_shapes=[pltpu.VMEM((FFN_LOAD, D_MODEL, TF_FFN), BF16), pltpu.VMEM((FFN_LOAD, D_MODEL, TF_FFN), BF16),
                        pltpu.VMEM((D_FF, D_MODEL), BF16), pltpu.VMEM((TM_FFN, D_FF), BF16)],
        compiler_params=_params(("arbitrary",)),
        name="swiglu_half_step",
    )(*args)


def _mixin_kernel(h_ref, mod_ref, ng_ref, w_ref, qg_ref, kvg_ref, wq_ref, wqs_ref, wk_ref, wv_ref, ek_ref,
                  fch_ref, cq_ref, sq_ref, ck_ref, sk_ref,
                  hy_ref, q_ref, k_ref, v_ref, ckv_ref, kr_ref, fcs_ref):
    i = pl.program_id(0)
    m = mod_ref[_group_of(i * TM_MIX)]
    n = (_rms(h_ref[...], ng_ref[...]) * (1.0 + m[4:5]) + m[3:4]).astype(BF16)
    proj = _dot(n, w_ref[...])
    hy_ref[...] = proj[:, MIX_HY:MIX_QA]
    q_a = proj[:, MIX_QA:MIX_KVA]
    kv_a = proj[:, MIX_KVA:MIX_FN]
    fn = proj[:, MIX_FN:MIX_KR]
    k_r = proj[:, MIX_KR:MIX_KR + ROPE_DIM]
    k_r_swapped = proj[:, MIX_KRS:MIX_KRS + ROPE_DIM]

    qn = _rms(q_a, qg_ref[...]).astype(BF16)
    q = _dot(qn, wq_ref[...]) * cq_ref[...] + _dot(qn, wqs_ref[...]) * sq_ref[...]
    q_ref[...] = q.astype(BF16)

    ckv = _rms(kv_a, kvg_ref[...])
    ckv_ref[...] = ckv
    kr_ref[...] = k_r
    ckv_b = ckv.astype(BF16)
    k_rot = (k_r * ck_ref[...] + k_r_swapped * sk_ref[...]).astype(BF16)
    k_ref[...] = (_dot(ckv_b, wk_ref[...]) + _dot(k_rot, ek_ref[...])).astype(BF16)
    v_ref[...] = _dot(ckv_b, wv_ref[...]).astype(BF16)
    fcs_ref[...] = _dot(fn.astype(BF16), fch_ref[...]).astype(BF16)


def _mixin_call(h, mods, ng, w_mix, l, qg, kvg, wq, wqs, wk, wv, ek, fch, cosq, sinq, cosk, sink):
    tm = TM_MIX

    def pos_block(i):
        tok0 = i * tm
        return jnp.where(tok0 < T_CTX, DEC_SEQ // tm, ((tok0 - T_CTX) % DEC_SEQ) // tm)

    def full(shape):
        return pl.BlockSpec(shape, lambda i: (0,) * len(shape))

    def tok(width):
        return pl.BlockSpec((tm, width), lambda i: (i, 0))

    def pos(width):
        return pl.BlockSpec((tm, width), lambda i: (pos_block(i), 0))

    out_shape = (
        jax.ShapeDtypeStruct((T_ALL, 3 * HY_W), F32),
        jax.ShapeDtypeStruct((T_ALL, QK_W), BF16),
        jax.ShapeDtypeStruct((T_ALL, QK_W), BF16),
        jax.ShapeDtypeStruct((T_ALL, N_HEADS * V_DIM), BF16),
        jax.ShapeDtypeStruct((T_ALL, KV_LORA), F32),
        jax.ShapeDtypeStruct((T_ALL, ROPE_DIM), F32),
        jax.ShapeDtypeStruct((T_ALL, 2 * FN_W), BF16),
    )
    return pl.pallas_call(
        _mixin_kernel,
        out_shape=out_shape,
        grid=(T_ALL // tm,),
        in_specs=[
            tok(D_MODEL), full((N_GROUPS, 9, D_MODEL)), full((1, D_MODEL)),
            pl.BlockSpec((None, D_MODEL, MIX_W), lambda i: (l, 0, 0)),
            full((1, Q_LORA)), full((1, KV_LORA)), full((Q_LORA, QK_W)), full((Q_LORA, QK_W)),
            full((KV_LORA, QK_W)), full((KV_LORA, N_HEADS * V_DIM)), full((ROPE_DIM, QK_W)),
            full((FN_W, 2 * FN_W)), pos(QK_W), pos(QK_W), pos(ROPE_DIM), pos(ROPE_DIM),
        ],
        out_specs=tuple(tok(s.shape[1]) for s in out_shape),
        compiler_params=_params(("parallel",)),
        name="mixer_input_proj",
    )(h, mods, ng, w_mix, qg, kvg, wq, wqs, wk, wv, ek, fch, cosq, sinq, cosk, sink)


def _cachekv_kernel(ckv_ref, kr_ref, wk_ref, wv_ref, ek_ref, k_ref, v_ref):
    ckv_b = ckv_ref[...].astype(BF16)
    k_ref[...] = (_dot(ckv_b, wk_ref[...]) + _dot(kr_ref[...].astype(BF16), ek_ref[...])).astype(BF16)
    v_ref[...] = _dot(ckv_b, wv_ref[...]).astype(BF16)


def _cachekv_call(cache_ckv, cache_krope, wk, wv, ek, l):
    def full(shape):
        return pl.BlockSpec(shape, lambda b: (0,) * len(shape))

    return pl.pallas_call(
        _cachekv_kernel,
        out_shape=(jax.ShapeDtypeStruct((DEC_BATCH, PAST_LEN, QK_W), BF16),
                   jax.ShapeDtypeStruct((DEC_BATCH, PAST_LEN, N_HEADS * V_DIM), BF16)),
        grid=(DEC_BATCH,),
        in_specs=[
            pl.BlockSpec((None, None, PAST_LEN, KV_LORA), lambda b: (b, l, 0, 0)),
            pl.BlockSpec((None, None, PAST_LEN, ROPE_DIM), lambda b: (b, l, 0, 0)),
            full((KV_LORA, QK_W)), full((KV_LORA, N_HEADS * V_DIM)), full((ROPE_DIM, QK_W)),
        ],
        out_specs=(pl.BlockSpec((None, PAST_LEN, QK_W), lambda b: (b, 0, 0)),
                   pl.BlockSpec((None, PAST_LEN, N_HEADS * V_DIM), lambda b: (b, 0, 0))),
        compiler_params=_params(("parallel",)),
        name="cached_context_kv",
    )(cache_ckv, cache_krope, wk, wv, ek)


def _attn_kernel(q_ref, k_ref, v_ref, *rest, has_cache):
    if has_cache:
        kc_ref, vc_ref, o_ref = rest
    else:
        (o_ref,) = rest
    scale = 1.0 / math.sqrt(NOPE_DIM + ROPE_DIM)
    first_half = lax.broadcasted_iota(jnp.int32, (1, 2 * V_DIM), 1) < V_DIM
    for pair in range(N_HEADS // 2):
        vs = slice(pair * 2 * V_DIM, (pair + 1) * 2 * V_DIM)
        outs = []
        for e in range(2):
            hs = slice((2 * pair + e) * HEAD_SLOT, (2 * pair + e + 1) * HEAD_SLOT)
            qh = q_ref[:, hs]
            s = _dot_nt(qh, k_ref[:, hs])
            mx = jnp.max(s, axis=-1, keepdims=True)
            if has_cache:
                sc = _dot_nt(qh, kc_ref[:, hs])
                mx = jnp.maximum(mx, jnp.max(sc, axis=-1, keepdims=True))
            p = jnp.exp((s - mx) * scale)
            denom = jnp.sum(p, axis=-1, keepdims=True)
            o = _dot(p.astype(BF16), v_ref[:, vs])
            if has_cache:
                pc = jnp.exp((sc - mx) * scale)
                denom = denom + jnp.sum(pc, axis=-1, keepdims=True)
                o = o + _dot(pc.astype(BF16), vc_ref[:, vs])
            outs.append(o / denom)
        o_ref[:, vs] = jnp.where(first_half, outs[0], outs[1]).astype(BF16)


def _attn_call(q, k, v, kc, vc, *, n_seq, seq_len, tok_off, tq):
    has_cache = kc is not None
    nq = seq_len // tq
    seq_blk0 = tok_off // seq_len
    q_blk0 = tok_off // tq
    vw = N_HEADS * V_DIM
    in_specs = [
        pl.BlockSpec((tq, QK_W), lambda b, j: (q_blk0 + b * nq + j, 0)),
        pl.BlockSpec((seq_len, QK_W), lambda b, j: (seq_blk0 + b, 0)),
        pl.BlockSpec((seq_len, vw), lambda b, j: (seq_blk0 + b, 0)),
    ]
    args = [q, k, v]
    if has_cache:
        in_specs += [pl.BlockSpec((None, PAST_LEN, QK_W), lambda b, j: (b, 0, 0)),
                     pl.BlockSpec((None, PAST_LEN, vw), lambda b, j: (b, 0, 0))]
        args += [kc, vc]
    return pl.pallas_call(
        functools.partial(_attn_kernel, has_cache=has_cache),
        out_shape=jax.ShapeDtypeStruct((n_seq * seq_len, vw), BF16),
        grid=(n_seq, nq),
        in_specs=in_specs,
        out_specs=pl.BlockSpec((tq, vw), lambda b, j: (b * nq + j, 0)),
        compiler_params=_params(("parallel", "parallel")),
        name="mla_attention_cache" if has_cache else "mla_attention",
    )(*args)


def _hyfilt_kernel(feat_ref, w1_ref, b1_ref, w2_ref, b2_ref, w3_ref, dec_ref, alt_ref, c_ref, s_ref,
                   hre_ref, him_ref, nyq_ref, a_sc, b_sc, *, L):
    j = pl.program_id(0)
    half = HY_ORDER * HY_W

    @pl.when(j == 0)
    def _():
        h = jnp.sin(_dot_hi(feat_ref[...], w1_ref[...]) + b1_ref[...])
        h = jnp.sin(_dot_hi(h, w2_ref[...]) + b2_ref[...])
        h = _dot_hi(h, w3_ref[...])
        dec = dec_ref[...]
        rows = lax.broadcasted_iota(jnp.int32, (L, 1), 0)
        h_fwd = h[:, :half] * dec
        h_bwd = jnp.where(rows > 0, h[:, half:] * dec, 0.0)
        norm = (jnp.sum(jnp.abs(h_fwd), axis=0, keepdims=True)
                + jnp.sum(jnp.abs(h_bwd), axis=0, keepdims=True) + EPS)
        inv = 1.0 / norm
        a = (h_fwd + h_bwd) * inv
        b = (h_fwd - h_bwd) * inv
        a_sc[...] = a.astype(BF16)
        b_sc[...] = b.astype(BF16)
        nyq_ref[...] = jnp.sum(a * alt_ref[...], axis=0, keepdims=True)

    hre_ref[...] = _dot(c_ref[...], a_sc[...])
    him_ref[...] = -_dot(s_ref[...], b_sc[...])


def _hyfilt_call(L, feats, w1, b1, w2, b2, w3, decay, alt, ctab, stab):
    tk = min(L, 512)
    half = HY_ORDER * HY_W

    def full(shape):
        return pl.BlockSpec(shape, lambda j: (0,) * len(shape))

    return pl.pallas_call(
        functools.partial(_hyfilt_kernel, L=L),
        out_shape=(jax.ShapeDtypeStruct((L, half), F32), jax.ShapeDtypeStruct((L, half), F32),
                   jax.ShapeDtypeStruct((1, half), F32)),
        grid=(L // tk,),
        in_specs=[
            full((L, POS_PAD)), full((POS_PAD, FILT_HID)), full((1, FILT_HID)), full((FILT_HID, FILT_HID)),
            full((1, FILT_HID)), full((FILT_HID, 2 * half)), full((L, half)), full((L, 1)),
            pl.BlockSpec((tk, L), lambda j: (j, 0)), pl.BlockSpec((tk, L), lambda j: (j, 0)),
        ],
        out_specs=(pl.BlockSpec((tk, half), lambda j: (j, 0)), pl.BlockSpec((tk, half), lambda j: (j, 0)),
                   full((1, half))),
        scratch_shapes=[pltpu.VMEM((L, half), BF16), pltpu.VMEM((L, half), BF16)],
        compiler_params=_params(("arbitrary",)),
        name="hyena_filter_spectrum_%d" % L,
    )(feats, w1, b1, w2, b2, w3, decay, alt, ctab, stab)


def _hyconv_kernel(hy_ref, cw_ref, cb_ref, skip_ref, hre_ref, him_ref, nyq_ref, alt_ref,
                   cr_ref, sr_ref, cc_ref, sc_ref, o_ref,
                   x1_sc, x2_sc, v_sc, vb_sc, acc_sc, nv_sc, *, L, tk, nk):
    p = pl.program_id(1)
    j = pl.program_id(2)

    @pl.when((p == 0) & (j == 0))
    def _():
        x = hy_ref[...]
        rows = lax.broadcasted_iota(jnp.int32, (L, 1), 0)
        prev = jnp.where(rows > 0, pltpu.roll(x, 1, 0), 0.0)
        nxt = jnp.where(rows < L - 1, pltpu.roll(x, L - 1, 0), 0.0)
        w = cw_ref[...]
        u = prev * w[0:1] + x * w[1:2] + nxt * w[2:3] + cb_ref[...]
        x1_sc[...] = u[:, :HY_W]
        x2_sc[...] = u[:, HY_W:2 * HY_W]
        v_sc[...] = u[:, 2 * HY_W:]

    @pl.when(j == 0)
    def _():
        v = v_sc[...]
        vb_sc[...] = v.astype(BF16)
        acc_sc[...] = jnp.zeros_like(acc_sc)
        nv_sc[...] = jnp.sum(v * alt_ref[...], axis=0, keepdims=True)

    vb = vb_sc[...]
    v_re = _dot(cr_ref[...], vb)
    v_im = -_dot(sr_ref[...], vb)
    h_re = hre_ref[...]
    h_im = him_ref[...]
    freq = j * tk + lax.broadcasted_iota(jnp.int32, (tk, 1), 0)
    wk = jnp.where(freq == 0, 0.5, 1.0)
    y_re = ((v_re * h_re - v_im * h_im) * wk).astype(BF16)
    y_im = ((v_re * h_im + v_im * h_re) * wk).astype(BF16)
    acc_sc[...] += _dot(cc_ref[...], y_re) - _dot(sc_ref[...], y_im)

    @pl.when(j == nk - 1)
    def _():
        v = v_sc[...]
        nyq = nv_sc[...] * nyq_ref[...]
        y = acc_sc[...] * (1.0 / L) + (0.5 / L) * alt_ref[...] * nyq + v * skip_ref[...]

        @pl.when(p == 0)
        def _():
            v_sc[...] = x1_sc[...] * y

        @pl.when(p == 1)
        def _():
            o_ref[...] = (x2_sc[...] * y).astype(BF16)


def _hyconv_call(hy, conv_w, conv_b, skip, hre, him, nyq, alt, ctab, stab, *, n_seq, L, tok_off):
    tk = min(L, 256)
    nk = L // tk
    seq_blk0 = tok_off // L

    def full(shape):
        return pl.BlockSpec(shape, lambda b, p, j: (0,) * len(shape))

    vm = lambda: pltpu.VMEM((L, HY_W), F32)
    return pl.pallas_call(
        functools.partial(_hyconv_kernel, L=L, tk=tk, nk=nk),
        out_shape=jax.ShapeDtypeStruct((n_seq * L, HY_W), BF16),
        grid=(n_seq, HY_ORDER, nk),
        in_specs=[
            pl.BlockSpec((L, 3 * HY_W), lambda b, p, j: (seq_blk0 + b, 0)),
            full((3, 3 * HY_W)), full((1, 3 * HY_W)),
            pl.BlockSpec((None, 1, HY_W), lambda b, p, j: (p, 0, 0)),
            pl.BlockSpec((tk, HY_W), lambda b, p, j: (j, p)),
            pl.BlockSpec((tk, HY_W), lambda b, p, j: (j, p)),
            pl.BlockSpec((None, 1, HY_W), lambda b, p, j: (p, 0, 0)),
            full((L, 1)),
            pl.BlockSpec((tk, L), lambda b, p, j: (j, 0)), pl.BlockSpec((tk, L), lambda b, p, j: (j, 0)),
            pl.BlockSpec((L, tk), lambda b, p, j: (0, j)), pl.BlockSpec((L, tk), lambda b, p, j: (0, j)),
        ],
        out_specs=pl.BlockSpec((L, HY_W), lambda b, p, j: (b, 0)),
        scratch_shapes=[vm(), vm(), vm(), pltpu.VMEM((L, HY_W), BF16), vm(), pltpu.VMEM((1, HY_W), F32)],
        compiler_params=_params(("arbitrary", "arbitrary", "arbitrary")),
        name="hyena_long_conv_%d" % L,
    )(hy, conv_w, conv_b, skip, hre, him, nyq, alt, ctab, stab, ctab, stab)


def _fnet_kernel(fcs_ref, c_ref, s_ref, o_ref, *, scale):
    y = _dot(c_ref[...], fcs_ref[:, :FN_W]) - _dot(s_ref[...], fcs_ref[:, FN_W:])
    o_ref[...] = (y * scale).astype(BF16)


def _fnet_call(fcs, ctab, stab, *, n_seq, L, tok_off):
    tk = min(L, 512)
    nk = L // tk
    seq_blk0 = tok_off // L
    return pl.pallas_call(
        functools.partial(_fnet_kernel, scale=1.0 / math.sqrt(L * FN_GROUP_W)),
        out_shape=jax.ShapeDtypeStruct((n_seq * L, FN_W), BF16),
        grid=(n_seq, nk),
        in_specs=[
            pl.BlockSpec((L, 2 * FN_W), lambda b, j: (seq_blk0 + b, 0)),
            pl.BlockSpec((tk, L), lambda b, j: (j, 0)), pl.BlockSpec((tk, L), lambda b, j: (j, 0)),
        ],
        out_specs=pl.BlockSpec((tk, FN_W), lambda b, j: (b * nk + j, 0)),
        compiler_params=_params(("parallel", "parallel")),
        name="fnet_position_dft_%d" % L,
    )(fcs, ctab, stab)


def _mixout_kernel(h_ref, mod_ref, ng_ref, wg_ref, z_ref, a_ref, f_ref, wa_ref, wb_ref, wc_ref, wo_ref, o_ref):
    i = pl.program_id(0)
    m = mod_ref[_group_of(i * TM_MIX)]
    h = h_ref[...]
    n = (_rms(h, ng_ref[...]) * (1.0 + m[4:5]) + m[3:4]).astype(BF16)
    acc = _sigmoid(_dot(n, wg_ref[:, 0:D_MODEL])) * _dot(z_ref[...], wa_ref[...])
    acc = acc + _sigmoid(_dot(n, wg_ref[:, D_MODEL:2 * D_MODEL])) * _dot(a_ref[...], wb_ref[...])
    acc = acc + _sigmoid(_dot(n, wg_ref[:, 2 * D_MODEL:])) * _dot(f_ref[...], wc_ref[...])
    y = _dot(acc.astype(BF16), wo_ref[...])
    o_ref[...] = h + m[5:6] * y


def _mixout_call(h, mods, ng, w_gate, l, z, a, f, wa, wb, wc, wo):
    tm = TM_MIX

    def full(shape):
        return pl.BlockSpec(shape, lambda i: (0,) * len(shape))

    def tok(width):
        return pl.BlockSpec((tm, width), lambda i: (i, 0))

    return pl.pallas_call(
        _mixout_kernel,
        out_shape=jax.ShapeDtypeStruct((T_ALL, D_MODEL), F32),
        grid=(T_ALL // tm,),
        in_specs=[
            tok(D_MODEL), full((N_GROUPS, 9, D_MODEL)), full((1, D_MODEL)),
            pl.BlockSpec((None, D_MODEL, N_BRANCH * D_MODEL), lambda i: (l, 0, 0)),
            tok(HY_W), tok(N_HEADS * V_DIM), tok(FN_W),
            full((HY_W, D_MODEL)), full((N_HEADS * V_DIM, D_MODEL)), full((FN_W, D_MODEL)),
            full((D_MODEL, D_MODEL)),
        ],
        out_specs=tok(D_MODEL),
        compiler_params=_params(("parallel",)),
        name="gated_merge_out_proj",
    )(h, mods, ng, w_gate, z, a, f, wa, wb, wc, wo)


def _wprep_kernel(w_ref, p_ref, mix_ref, gate_ref):
    w = w_ref[...]
    mix_ref[:, :MIX_FN] = w[:, :IN_KR].astype(BF16)
    mix_ref[:, MIX_FN:MIX_KR] = w[:, IN_FN:IN_GATE].astype(BF16)
    lane = lax.broadcasted_iota(jnp.int32, (1, 128), 1)
    k_r = jnp.where(lane < ROPE_DIM, w[:, IN_KR:IN_KR + 128], 0.0)
    mix_ref[:, MIX_KR:MIX_KRS] = k_r.astype(BF16)
    mix_ref[:, MIX_KRS:MIX_W] = _dot_hi(k_r, p_ref[...]).astype(BF16)
    gate_ref[...] = w[:, IN_GATE:].astype(BF16)


def _wprep_call(w_in):
    swap = np.zeros((128, 128), np.float32)
    swap[_ROPE_PERM, np.arange(ROPE_DIM)] = _ROPE_SIGN
    return pl.pallas_call(
        _wprep_kernel,
        out_shape=(jax.ShapeDtypeStruct((DEPTH, D_MODEL, MIX_W), BF16),
                   jax.ShapeDtypeStruct((DEPTH, D_MODEL, N_BRANCH * D_MODEL), BF16)),
        grid=(DEPTH, D_MODEL // TR_PREP),
        in_specs=[pl.BlockSpec((None, TR_PREP, IN_COLS), lambda l, r: (l, r, 0)),
                  pl.BlockSpec((128, 128), lambda l, r: (0, 0))],
        out_specs=(pl.BlockSpec((None, TR_PREP, MIX_W), lambda l, r: (l, r, 0)),
                   pl.BlockSpec((None, TR_PREP, N_BRANCH * D_MODEL), lambda l, r: (l, r, 0))),
        compiler_params=_params(("parallel", "parallel")),
        name="input_proj_weight_relayout",
    )(w_in, jnp.asarray(swap))


def _layer_weights(l, w_qb, w_kvb):
    sign = jnp.asarray(_ROPE_SIGN)
    wq3 = w_qb[l].reshape(Q_LORA, N_HEADS, NOPE_DIM + ROPE_DIM)
    slot_pad = HEAD_SLOT - NOPE_DIM - ROPE_DIM
    wq = jnp.pad(wq3, ((0, 0), (0, 0), (0, slot_pad))).reshape(Q_LORA, QK_W).astype(BF16)
    wq_sw = wq3[:, :, NOPE_DIM:][:, :, _ROPE_PERM] * sign
    wqs = jnp.pad(wq_sw, ((0, 0), (0, 0), (NOPE_DIM, slot_pad))).reshape(Q_LORA, QK_W).astype(BF16)

    wkv3 = w_kvb[l].reshape(KV_LORA, N_HEADS, NOPE_DIM + V_DIM)
    wk = jnp.pad(wkv3[:, :, :NOPE_DIM], ((0, 0), (0, 0), (0, HEAD_SLOT - NOPE_DIM))).reshape(KV_LORA, QK_W).astype(BF16)
    wv = wkv3[:, :, NOPE_DIM:].reshape(KV_LORA, N_HEADS * V_DIM).astype(BF16)
    return wq, wqs, wk, wv


def kernel(x_prompt, x_sample, cache_ckv, cache_krope, c, c_ctx, w_ada, b_ada, norm_g, w_ffn_up, w_ffn_down,
           w_in, hy_conv_w, hy_conv_b, hy_filt_w1, hy_filt_b1, hy_filt_w2, hy_filt_b2, hy_filt_w3, hy_skip,
           w_hy_out, q_norm_g, w_qb, kv_norm_g, w_kvb, w_mla_o, w_fnet, w_out, final_g):
    tabs = _tables()
    cosq, sinq, cosk, sink = (jnp.asarray(t) for t in tabs["rope"])
    fch = jnp.asarray(tabs["fnch"]).astype(BF16)
    ek = jnp.asarray(tabs["ropeexp"]).astype(BF16)
    dft = {}
    for L in (SEQ, DEC_SEQ):
        dft[("hy", L)] = tuple(jnp.asarray(t).astype(BF16) for t in tabs[("hy", L)])
        dft[("fn", L)] = tuple(jnp.asarray(t).astype(BF16) for t in tabs[("fn", L)])

    cvec = jnp.concatenate([c_ctx[None, :], c, jnp.zeros((8 - N_GROUPS, D_MODEL), F32)], axis=0)
    ada = _ada_call(cvec, w_ada, b_ada)
    mods_all = ada[:, :N_GROUPS].reshape(DEPTH, N_GROUPS, 9, D_MODEL)

    w_mix, w_gate = _wprep_call(w_in)
    w_hy_out, w_mla_o, w_fnet, w_out = (w.astype(BF16) for w in (w_hy_out, w_mla_o, w_fnet, w_out))
    hs = (x_prompt.reshape(T_CTX, D_MODEL), x_sample.reshape(T_LAT, D_MODEL))
    segs = ((BATCH, SEQ, 0), (DEC_BATCH, DEC_SEQ, T_CTX))
    ckv_out = []
    kr_out = []
    for l in range(DEPTH):
        mods = mods_all[l]
        ng = norm_g[l]
        wq, wqs, wk, wv = _layer_weights(l, w_qb, w_kvb)

        h = _ffn_call(hs, mods, ng[0:1], w_ffn_up, w_ffn_down, l, 0, 0)

        hy, q, k, v, ckv, k_r, fcs = _mixin_call(
            h, mods, ng[1:2], w_mix, l, q_norm_g[l][None, :], kv_norm_g[l][None, :], wq, wqs, wk, wv, ek, fch,
            cosq, sinq, cosk, sink)
        ckv_out.append(ckv[:T_CTX].reshape(BATCH, SEQ, KV_LORA))
        kr_out.append(k_r[:T_CTX].reshape(BATCH, SEQ, ROPE_DIM))

        kc, vc = _cachekv_call(cache_ckv, cache_krope, wk, wv, ek, l)

        w1 = jnp.pad(hy_filt_w1[l], ((0, POS_PAD - POS_EMB), (0, 0)))
        skip = hy_skip[l].reshape(HY_ORDER, 1, HY_W)
        z_parts, a_parts, f_parts = [], [], []
        for n_seq, L, off in segs:
            feats, decay, alt = (jnp.asarray(t) for t in tabs[("filt", L)])
            c_hy, s_hy = dft[("hy", L)]
            hre, him, nyq = _hyfilt_call(L, feats, w1, hy_filt_b1[l][None, :], hy_filt_w2[l],
                                         hy_filt_b2[l][None, :], hy_filt_w3[l], decay, alt, c_hy, s_hy)
            z_parts.append(_hyconv_call(hy, hy_conv_w[l], hy_conv_b[l][None, :], skip, hre, him,
                                        nyq.reshape(HY_ORDER, 1, HY_W), alt, c_hy, s_hy,
                                        n_seq=n_seq, L=L, tok_off=off))
            cache = (kc, vc) if off else (None, None)
            a_parts.append(_attn_call(q, k, v, *cache, n_seq=n_seq, seq_len=L, tok_off=off, tq=min(L, 512)))
            c_fn, s_fn = dft[("fn", L)]
            f_parts.append(_fnet_call(fcs, c_fn, s_fn, n_seq=n_seq, L=L, tok_off=off))
        z = jnp.concatenate(z_parts, axis=0)
        a = jnp.concatenate(a_parts, axis=0)
        f = jnp.concatenate(f_parts, axis=0)

        h = _mixout_call(h, mods, ng[1:2], w_gate, l, z, a, f, w_hy_out[l], w_mla_o[l], w_fnet[l], w_out[l])

        last = l == DEPTH - 1
        out = _ffn_call((h,), mods, ng[2:3], w_ffn_up, w_ffn_down, l, 1, 6, final_g[None, :] if last else None)
        hs = out if last else (out,)

    y_prompt = hs[0].reshape(BATCH, SEQ, D_MODEL)
    y_sample = hs[1].reshape(DEC_BATCH, DEC_SEQ, D_MODEL)
    return y_prompt, y_sample, jnp.stack(ckv_out, axis=1), jnp.stack(kr_out, axis=1)
```

```python
import functools
import math

import numpy as np
import jax
import jax.numpy as jnp
from jax import lax
from jax.experimental import pallas as pl
from jax.experimental.pallas import tpu as pltpu

F32 = jnp.float32
BF16 = jnp.bfloat16
HIGHEST = lax.Precision.HIGHEST

D_MODEL = 1024
BATCH = 16
SEQ = 256
DEPTH = 2
DEC_BATCH = 2
DEC_SEQ = 2048
PAST_LEN = 512
GRID_W = 64
HY_W = 256
HY_ORDER = 2
N_BANDS = 8
POS_EMB = 1 + 2 * N_BANDS
FILT_HID = 64
N_HEADS = 8
Q_LORA = 256
KV_LORA = 128
NOPE_DIM = 64
ROPE_DIM = 32
V_DIM = 64
ROPE_BASE = 10000.0
FN_GROUPS = 4
FN_GROUP_W = 64
FN_W = FN_GROUPS * FN_GROUP_W
N_BRANCH = 3
D_FF = 2816
EPS = 1e-6

T_CTX = BATCH * SEQ
T_LAT = DEC_BATCH * DEC_SEQ
T_ALL = T_CTX + T_LAT
N_GROUPS = 1 + DEC_BATCH
HEAD_SLOT = 128
QK_W = N_HEADS * HEAD_SLOT
POS_PAD = 128

VMEM_LIMIT = 56 * 1024 * 1024

TM_FFN = 512
TF_FFN = 256
FFN_LOAD = D_FF // TF_FFN
TM_MIX = 512
PREP_ROWS = 1152

IN_KR = 3 * HY_W + Q_LORA + KV_LORA
IN_FN = IN_KR + ROPE_DIM
IN_GATE = IN_FN + FN_W
IN_COLS = IN_GATE + N_BRANCH * D_MODEL

MIX_HY = 0
MIX_QA = 3 * HY_W
MIX_KVA = MIX_QA + Q_LORA
MIX_FN = MIX_KVA + KV_LORA
MIX_KR = MIX_FN + FN_W
MIX_KRS = MIX_KR + 128
MIX_W = MIX_KRS + 128

_ROPE_PERM = np.array(list(range(8, 16)) + list(range(0, 8)) + list(range(24, 32)) + list(range(16, 24)))
_ROPE_SIGN = np.array([-1.0] * 8 + [1.0] * 8 + [-1.0] * 8 + [1.0] * 8, np.float32)


def _dft_tables(L, half):
    k = np.arange(L, dtype=np.int64)
    period = 2 * L if half else L
    m = (k[:, None] * k[None, :]) % period
    ang = 2.0 * np.pi * m.astype(np.float64) / period
    return np.cos(ang).astype(np.float32), np.sin(ang).astype(np.float32)


def _filter_tables(L):
    t = np.arange(L, dtype=np.float64)
    t_norm = t / (L - 1)
    w = 2.0 * np.pi * t / L
    bands = np.linspace(1e-4, N_BANDS - 1, N_BANDS)
    ang = w[:, None] * bands[None, :]
    feats = np.concatenate([t_norm[:, None], np.cos(ang), -np.sin(ang)], axis=-1)
    feats = np.pad(feats, ((0, 0), (0, POS_PAD - POS_EMB)))
    deltas = np.linspace(math.log(1e-2) / 1.5, math.log(1e-2) / 0.3, HY_W)
    decay = np.exp(-t_norm[:, None] * np.abs(deltas)[None, :])
    decay = np.concatenate([decay, decay], axis=1)
    alt = np.where(np.arange(L) % 2 == 0, 1.0, -1.0)[:, None]
    return feats.astype(np.float32), decay.astype(np.float32), alt.astype(np.float32)


def _rope_tables():
    t = np.arange(DEC_SEQ)
    row = (t // GRID_W).astype(np.float64)
    col = (t % GRID_W).astype(np.float64)
    nf = ROPE_DIM // 4
    inv = ROPE_BASE ** (-np.arange(nf, dtype=np.float64) / nf)
    ar = row[:, None] * inv[None, :]
    ac = col[:, None] * inv[None, :]
    cos32 = np.concatenate([np.cos(ar), np.cos(ar), np.cos(ac), np.cos(ac)], axis=1)
    sin32 = np.concatenate([np.sin(ar), np.sin(ar), np.sin(ac), np.sin(ac)], axis=1)
    cos32 = np.concatenate([cos32, np.ones((TM_MIX, ROPE_DIM))], axis=0)
    sin32 = np.concatenate([sin32, np.zeros((TM_MIX, ROPE_DIM))], axis=0)
    n = cos32.shape[0]
    cosq = np.ones((n, N_HEADS, HEAD_SLOT))
    sinq = np.zeros((n, N_HEADS, HEAD_SLOT))
    cosq[:, :, NOPE_DIM:NOPE_DIM + ROPE_DIM] = cos32[:, None, :]
    sinq[:, :, NOPE_DIM:NOPE_DIM + ROPE_DIM] = sin32[:, None, :]
    return (cosq.reshape(n, QK_W).astype(np.float32), sinq.reshape(n, QK_W).astype(np.float32),
            cos32.astype(np.float32), sin32.astype(np.float32))


def _fnet_channel_table():
    j = np.arange(FN_GROUP_W)
    ang = 2.0 * np.pi * ((j[:, None] * j[None, :]) % FN_GROUP_W) / FN_GROUP_W
    out = np.zeros((FN_W, 2 * FN_W))
    for g in range(FN_GROUPS):
        sl = slice(g * FN_GROUP_W, (g + 1) * FN_GROUP_W)
        out[sl, sl] = np.cos(ang)
        out[sl, FN_W + g * FN_GROUP_W:FN_W + (g + 1) * FN_GROUP_W] = np.sin(ang)
    return out.astype(np.float32)


def _rope_expand_table():
    e = np.zeros((ROPE_DIM, N_HEADS, HEAD_SLOT), np.float32)
    for j in range(ROPE_DIM):
        e[j, :, NOPE_DIM + j] = 1.0
    return e.reshape(ROPE_DIM, QK_W)


_TABLES = {}


def _tables():
    if not _TABLES:
        for L in (SEQ, DEC_SEQ):
            _TABLES[("hy", L)] = _dft_tables(L, True)
            _TABLES[("fn", L)] = _dft_tables(L, False)
            _TABLES[("filt", L)] = _filter_tables(L)
        _TABLES["rope"] = _rope_tables()
        _TABLES["fnch"] = _fnet_channel_table()
        _TABLES["ropeexp"] = _rope_expand_table()
    return _TABLES


def _rms(x, g):
    ms = jnp.mean(x * x, axis=-1, keepdims=True)
    return x * lax.rsqrt(ms + EPS) * g


def _sigmoid(x):
    return 1.0 / (1.0 + jnp.exp(-x))


def _dot(a, b):
    return jnp.dot(a, b, preferred_element_type=F32)


def _dot_hi(a, b):
    return jnp.dot(a, b, precision=HIGHEST, preferred_element_type=F32)


def _dot_nt(a, b):
    return lax.dot_general(a, b, (((1,), (1,)), ((), ())), preferred_element_type=F32)


def _group_of(tok0):
    return jnp.where(tok0 < T_CTX, 0, 1 + (tok0 - T_CTX) // DEC_SEQ)


def _params(sem):
    return pltpu.CompilerParams(dimension_semantics=sem, vmem_limit_bytes=VMEM_LIMIT)


def _ada_kernel(c_ref, w_ref, b_ref, o_ref):
    x = c_ref[...]
    s = x * _sigmoid(x)
    w = w_ref[...]
    o_ref[...] = jnp.zeros_like(o_ref)
    for g in range(N_GROUPS):
        o_ref[g:g + 1, :] = jnp.sum(w * s[:, g:g + 1], axis=0, keepdims=True) + b_ref[...]


def _ada_call(cvec, w_ada, b_ada):
    tn = 1024
    n_out = 9 * D_MODEL
    return pl.pallas_call(
        _ada_kernel,
        out_shape=jax.ShapeDtypeStruct((DEPTH, 8, n_out), F32),
        grid=(DEPTH, n_out // tn),
        in_specs=[
            pl.BlockSpec((D_MODEL, 8), lambda l, j: (0, 0)),
            pl.BlockSpec((None, D_MODEL, tn), lambda l, j: (l, 0, j)),
            pl.BlockSpec((None, 1, tn), lambda l, j: (l, 0, j)),
        ],
        out_specs=pl.BlockSpec((None, 8, tn), lambda l, j: (l, 0, j)),
        compiler_params=_params(("arbitrary", "arbitrary")),
        name="ada_modulation",
    )(cvec, w_ada, b_ada.reshape(DEPTH, 1, n_out))


def _ffn_kernel(*refs, j0, split_in, final):
    refs = list(refs)
    x_refs = [refs.pop(0) for _ in range(2 if split_in else 1)]
    mod_ref, ng_ref, wg_ref, wu_ref, wd_ref = refs[:5]
    refs = refs[5:]
    fg_ref = refs.pop(0) if final else None
    o_refs = [refs.pop(0) for _ in range(2 if final else 1)]
    wg_sc, wu_sc, wd_sc, hid_sc = refs
    s = pl.program_id(0)
    n_ctx_tiles = T_CTX // TM_FFN

    @pl.when(s < FFN_LOAD)
    def _():
        wg_sc[s] = wg_ref[...].astype(BF16)
        wu_sc[s] = wu_ref[...].astype(BF16)
        wd_sc[pl.ds(pl.multiple_of(s * TF_FFN, TF_FFN), TF_FFN), :] = wd_ref[...].astype(BF16)

    @pl.when(s >= FFN_LOAD)
    def _():
        t = s - FFN_LOAD
        if split_in:
            x = jnp.where(t < n_ctx_tiles, x_refs[0][...], x_refs[1][...])
        else:
            x = x_refs[0][...]
        m = mod_ref[_group_of(t * TM_FFN)]
        n = (_rms(x, ng_ref[...]) * (1.0 + m[j0 + 1:j0 + 2]) + m[j0:j0 + 1]).astype(BF16)
        for c in range(FFN_LOAD):
            g = _dot(n, wg_sc[c])
            u = _dot(n, wu_sc[c])
            hid_sc[:, c * TF_FFN:(c + 1) * TF_FFN] = (g * _sigmoid(g) * u).astype(BF16)
        y = x + 0.5 * m[j0 + 2:j0 + 3] * _dot(hid_sc[...], wd_sc[...])
        if final:
            y = _rms(y, fg_ref[...])

            @pl.when(t < n_ctx_tiles)
            def _():
                o_refs[0][...] = y

            @pl.when(t >= n_ctx_tiles)
            def _():
                o_refs[1][...] = y
        else:
            o_refs[0][...] = y


def _ffn_call(xs, mods, ng, w_up, w_down, l, f, j0, final_g=None):
    split_in = len(xs) == 2
    final = final_g is not None
    n_ctx_tiles = T_CTX // TM_FFN
    tile = lambda s: jnp.maximum(s - FFN_LOAD, 0)
    chunk = lambda s: jnp.minimum(s, FFN_LOAD - 1)
    ctx_blk = lambda s: (jnp.minimum(tile(s), n_ctx_tiles - 1), 0)
    lat_blk = lambda s: (jnp.maximum(tile(s) - n_ctx_tiles, 0), 0)
    row = pl.BlockSpec((1, D_MODEL), lambda s: (0, 0))
    tok = lambda index_map: pl.BlockSpec((TM_FFN, D_MODEL), index_map)
    if split_in:
        in_specs = [tok(ctx_blk), tok(lat_blk)]
    else:
        in_specs = [tok(lambda s: (tile(s), 0))]
    in_specs += [
        pl.BlockSpec((N_GROUPS, 9, D_MODEL), lambda s: (0, 0, 0)),
        row,
        pl.BlockSpec((None, None, D_MODEL, TF_FFN), lambda s: (l, f, 0, chunk(s))),
        pl.BlockSpec((None, None, D_MODEL, TF_FFN), lambda s: (l, f, 0, chunk(s) + FFN_LOAD)),
        pl.BlockSpec((None, None, TF_FFN, D_MODEL), lambda s: (l, f, chunk(s), 0)),
    ]
    args = list(xs) + [mods, ng, w_up, w_up, w_down]
    if final:
        in_specs.append(row)
        args.append(final_g)
        out_shape = (jax.ShapeDtypeStruct((T_CTX, D_MODEL), F32), jax.ShapeDtypeStruct((T_LAT, D_MODEL), F32))
        out_specs = (tok(ctx_blk), tok(lat_blk))
    else:
        out_shape = jax.ShapeDtypeStruct((T_ALL, D_MODEL), F32)
        out_specs = tok(lambda s: (tile(s), 0))
    return pl.pallas_call(
        functools.partial(_ffn_kernel, j0=j0, split_in=split_in, final=final),
        out_shape=out_shape,
        grid=(FFN_LOAD + T_ALL // TM_FFN,),
        in_specs=in_specs,
        out_specs=out_specs,
        scratch_shapes=[pltpu.VMEM((FFN_LOAD, D_MODEL, TF_FFN), BF16), pltpu.VMEM((FFN_LOAD, D_MODEL, TF_FFN), BF16),
                        pltpu.VMEM((D_FF, D_MODEL), BF16), pltpu.VMEM((TM_FFN, D_FF), BF16)],
        compiler_params=_params(("arbitrary",)),
        name="swiglu_half_step",
    )(*args)


def _mixin_kernel(h_ref, mod_ref, ng_ref, w_ref, qg_ref, kvg_ref, wq_ref, wqs_ref, wk_ref, wv_ref, ek_ref,
                  fch_ref, cq_ref, sq_ref, ck_ref, sk_ref,
                  hy_ref, q_ref, k_ref, v_ref, ckv_ref, kr_ref, fcs_ref):
    i = pl.program_id(0)
    m = mod_ref[_group_of(i * TM_MIX)]
    n = (_rms(h_ref[...], ng_ref[...]) * (1.0 + m[4:5]) + m[3:4]).astype(BF16)
    proj = _dot_nt(n, w_ref[...])
    hy_ref[...] = proj[:, MIX_HY:MIX_QA]
    q_a = proj[:, MIX_QA:MIX_KVA]
    kv_a = proj[:, MIX_KVA:MIX_FN]
    fn = proj[:, MIX_FN:MIX_KR]
    k_r = proj[:, MIX_KR:MIX_KR + ROPE_DIM]
    k_r_swapped = proj[:, MIX_KRS:MIX_KRS + ROPE_DIM]

    qn = _rms(q_a, qg_ref[...]).astype(BF16)
    q = _dot(qn, wq_ref[...]) * cq_ref[...] + _dot(qn, wqs_ref[...]) * sq_ref[...]
    q_ref[...] = q.astype(BF16)

    ckv = _rms(kv_a, kvg_ref[...])
    ckv_ref[...] = ckv
    kr_ref[...] = k_r
    ckv_b = ckv.astype(BF16)
    k_rot = (k_r * ck_ref[...] + k_r_swapped * sk_ref[...]).astype(BF16)
    k_ref[...] = (_dot(ckv_b, wk_ref[...]) + _dot(k_rot, ek_ref[...])).astype(BF16)
    v_ref[...] = _dot(ckv_b, wv_ref[...]).astype(BF16)
    fcs_ref[...] = _dot(fn.astype(BF16), fch_ref[...]).astype(BF16)


def _mixin_call(h, mods, ng, w_mix, l, qg, kvg, wq, wqs, wk, wv, ek, fch, cosq, sinq, cosk, sink):
    tm = TM_MIX

    def pos_block(i):
        tok0 = i * tm
        return jnp.where(tok0 < T_CTX, DEC_SEQ // tm, ((tok0 - T_CTX) % DEC_SEQ) // tm)

    def full(shape):
        return pl.BlockSpec(shape, lambda i: (0,) * len(shape))

    def tok(width):
        return pl.BlockSpec((tm, width), lambda i: (i, 0))

    def pos(width):
        return pl.BlockSpec((tm, width), lambda i: (pos_block(i), 0))

    out_shape = (
        jax.ShapeDtypeStruct((T_ALL, 3 * HY_W), F32),
        jax.ShapeDtypeStruct((T_ALL, QK_W), BF16),
        jax.ShapeDtypeStruct((T_ALL, QK_W), BF16),
        jax.ShapeDtypeStruct((T_ALL, N_HEADS * V_DIM), BF16),
        jax.ShapeDtypeStruct((T_ALL, KV_LORA), F32),
        jax.ShapeDtypeStruct((T_ALL, ROPE_DIM), F32),
        jax.ShapeDtypeStruct((T_ALL, 2 * FN_W), BF16),
    )
    return pl.pallas_call(
        _mixin_kernel,
        out_shape=out_shape,
        grid=(T_ALL // tm,),
        in_specs=[
            tok(D_MODEL), full((N_GROUPS, 9, D_MODEL)), full((1, D_MODEL)),
            pl.BlockSpec((None, MIX_W, D_MODEL), lambda i: (l, 0, 0)),
            full((1, Q_LORA)), full((1, KV_LORA)), full((Q_LORA, QK_W)), full((Q_LORA, QK_W)),
            full((KV_LORA, QK_W)), full((KV_LORA, N_HEADS * V_DIM)), full((ROPE_DIM, QK_W)),
            full((FN_W, 2 * FN_W)), pos(QK_W), pos(QK_W), pos(ROPE_DIM), pos(ROPE_DIM),
        ],
        out_specs=tuple(tok(s.shape[1]) for s in out_shape),
        compiler_params=_params(("parallel",)),
        name="mixer_input_proj",
    )(h, mods, ng, w_mix, qg, kvg, wq, wqs, wk, wv, ek, fch, cosq, sinq, cosk, sink)


def _cachekv_kernel(ckv_ref, kr_ref, wk_ref, wv_ref, ek_ref, k_ref, v_ref):
    ckv_b = ckv_ref[...].astype(BF16)
    k_ref[...] = (_dot(ckv_b, wk_ref[...]) + _dot(kr_ref[...].astype(BF16), ek_ref[...])).astype(BF16)
    v_ref[...] = _dot(ckv_b, wv_ref[...]).astype(BF16)


def _cachekv_call(cache_ckv, cache_krope, wk, wv, ek, l):
    def full(shape):
        return pl.BlockSpec(shape, lambda b: (0,) * len(shape))

    return pl.pallas_call(
        _cachekv_kernel,
        out_shape=(jax.ShapeDtypeStruct((DEC_BATCH, PAST_LEN, QK_W), BF16),
                   jax.ShapeDtypeStruct((DEC_BATCH, PAST_LEN, N_HEADS * V_DIM), BF16)),
        grid=(DEC_BATCH,),
        in_specs=[
            pl.BlockSpec((None, None, PAST_LEN, KV_LORA), lambda b: (b, l, 0, 0)),
            pl.BlockSpec((None, None, PAST_LEN, ROPE_DIM), lambda b: (b, l, 0, 0)),
            full((KV_LORA, QK_W)), full((KV_LORA, N_HEADS * V_DIM)), full((ROPE_DIM, QK_W)),
        ],
        out_specs=(pl.BlockSpec((None, PAST_LEN, QK_W), lambda b: (b, 0, 0)),
                   pl.BlockSpec((None, PAST_LEN, N_HEADS * V_DIM), lambda b: (b, 0, 0))),
        compiler_params=_params(("parallel",)),
        name="cached_context_kv",
    )(cache_ckv, cache_krope, wk, wv, ek)


def _attn_kernel(q_ref, k_ref, v_ref, *rest, has_cache):
    if has_cache:
        kc_ref, vc_ref, o_ref = rest
    else:
        (o_ref,) = rest
    scale = 1.0 / math.sqrt(NOPE_DIM + ROPE_DIM)
    first_half = lax.broadcasted_iota(jnp.int32, (1, 2 * V_DIM), 1) < V_DIM
    for pair in range(N_HEADS // 2):
        vs = slice(pair * 2 * V_DIM, (pair + 1) * 2 * V_DIM)
        outs = []
        for e in range(2):
            hs = slice((2 * pair + e) * HEAD_SLOT, (2 * pair + e + 1) * HEAD_SLOT)
            qh = q_ref[:, hs]
            s = _dot_nt(qh, k_ref[:, hs])
            mx = jnp.max(s, axis=-1, keepdims=True)
            if has_cache:
                sc = _dot_nt(qh, kc_ref[:, hs])
                mx = jnp.maximum(mx, jnp.max(sc, axis=-1, keepdims=True))
            p = jnp.exp((s - mx) * scale)
            denom = jnp.sum(p, axis=-1, keepdims=True)
            o = _dot(p.astype(BF16), v_ref[:, vs])
            if has_cache:
                pc = jnp.exp((sc - mx) * scale)
                denom = denom + jnp.sum(pc, axis=-1, keepdims=True)
                o = o + _dot(pc.astype(BF16), vc_ref[:, vs])
            outs.append(o / denom)
        o_ref[:, vs] = jnp.where(first_half, outs[0], outs[1]).astype(BF16)


def _attn_call(q, k, v, kc, vc, *, n_seq, seq_len, tok_off, tq):
    has_cache = kc is not None
    nq = seq_len // tq
    seq_blk0 = tok_off // seq_len
    q_blk0 = tok_off // tq
    vw = N_HEADS * V_DIM
    in_specs = [
        pl.BlockSpec((tq, QK_W), lambda b, j: (q_blk0 + b * nq + j, 0)),
        pl.BlockSpec((seq_len, QK_W), lambda b, j: (seq_blk0 + b, 0)),
        pl.BlockSpec((seq_len, vw), lambda b, j: (seq_blk0 + b, 0)),
    ]
    args = [q, k, v]
    if has_cache:
        in_specs += [pl.BlockSpec((None, PAST_LEN, QK_W), lambda b, j: (b, 0, 0)),
                     pl.BlockSpec((None, PAST_LEN, vw), lambda b, j: (b, 0, 0))]
        args += [kc, vc]
    return pl.pallas_call(
        functools.partial(_attn_kernel, has_cache=has_cache),
        out_shape=jax.ShapeDtypeStruct((n_seq * seq_len, vw), BF16),
        grid=(n_seq, nq),
        in_specs=in_specs,
        out_specs=pl.BlockSpec((tq, vw), lambda b, j: (b * nq + j, 0)),
        compiler_params=_params(("parallel", "parallel")),
        name="mla_attention_cache" if has_cache else "mla_attention",
    )(*args)


def _hyfilt_kernel(feat_ref, w1_ref, b1_ref, w2_ref, b2_ref, w3_ref, dec_ref, alt_ref, c_ref, s_ref,
                   hre_ref, him_ref, nyq_ref, a_sc, b_sc, *, L):
    j = pl.program_id(0)
    half = HY_ORDER * HY_W

    @pl.when(j == 0)
    def _():
        h = jnp.sin(_dot_hi(feat_ref[...], w1_ref[...]) + b1_ref[...])
        h = jnp.sin(_dot_hi(h, w2_ref[...]) + b2_ref[...])
        h = _dot_hi(h, w3_ref[...])
        dec = dec_ref[...]
        rows = lax.broadcasted_iota(jnp.int32, (L, 1), 0)
        h_fwd = h[:, :half] * dec
        h_bwd = jnp.where(rows > 0, h[:, half:] * dec, 0.0)
        norm = (jnp.sum(jnp.abs(h_fwd), axis=0, keepdims=True)
                + jnp.sum(jnp.abs(h_bwd), axis=0, keepdims=True) + EPS)
        inv = 1.0 / norm
        a = (h_fwd + h_bwd) * inv
        b = (h_fwd - h_bwd) * inv
        a_sc[...] = a.astype(BF16)
        b_sc[...] = b.astype(BF16)
        nyq_ref[...] = jnp.sum(a * alt_ref[...], axis=0, keepdims=True)

    hre_ref[...] = _dot(c_ref[...], a_sc[...])
    him_ref[...] = -_dot(s_ref[...], b_sc[...])


def _hyfilt_call(L, feats, w1, b1, w2, b2, w3, decay, alt, ctab, stab):
    tk = min(L, 512)
    half = HY_ORDER * HY_W

    def full(shape):
        return pl.BlockSpec(shape, lambda j: (0,) * len(shape))

    return pl.pallas_call(
        functools.partial(_hyfilt_kernel, L=L),
        out_shape=(jax.ShapeDtypeStruct((L, half), F32), jax.ShapeDtypeStruct((L, half), F32),
                   jax.ShapeDtypeStruct((1, half), F32)),
        grid=(L // tk,),
        in_specs=[
            full((L, POS_PAD)), full((POS_PAD, FILT_HID)), full((1, FILT_HID)), full((FILT_HID, FILT_HID)),
            full((1, FILT_HID)), full((FILT_HID, 2 * half)), full((L, half)), full((L, 1)),
            pl.BlockSpec((tk, L), lambda j: (j, 0)), pl.BlockSpec((tk, L), lambda j: (j, 0)),
        ],
        out_specs=(pl.BlockSpec((tk, half), lambda j: (j, 0)), pl.BlockSpec((tk, half), lambda j: (j, 0)),
                   full((1, half))),
        scratch_shapes=[pltpu.VMEM((L, half), BF16), pltpu.VMEM((L, half), BF16)],
        compiler_params=_params(("arbitrary",)),
        name="hyena_filter_spectrum_%d" % L,
    )(feats, w1, b1, w2, b2, w3, decay, alt, ctab, stab)


def _hyconv_kernel(hy_ref, cw_ref, cb_ref, skip_ref, hre_ref, him_ref, nyq_ref, alt_ref,
                   cr_ref, sr_ref, cc_ref, sc_ref, o_ref,
                   x1_sc, x2_sc, v_sc, vb_sc, acc_sc, yre_sc, yim_sc, *, n_seq, L, tk, nk):
    p = pl.program_id(0)
    j = pl.program_id(1)
    cols = [slice(b * HY_W, (b + 1) * HY_W) for b in range(n_seq)]

    @pl.when((p == 0) & (j == 0))
    def _():
        rows = lax.broadcasted_iota(jnp.int32, (L, 1), 0)
        w = cw_ref[...]
        for b in range(n_seq):
            x = hy_ref[b * L:(b + 1) * L, :]
            prev = jnp.where(rows > 0, pltpu.roll(x, 1, 0), 0.0)
            nxt = jnp.where(rows < L - 1, pltpu.roll(x, L - 1, 0), 0.0)
            u = prev * w[0:1] + x * w[1:2] + nxt * w[2:3] + cb_ref[...]
            x1_sc[:, cols[b]] = u[:, :HY_W]
            x2_sc[:, cols[b]] = u[:, HY_W:2 * HY_W]
            v_sc[:, cols[b]] = u[:, 2 * HY_W:]

    @pl.when(j == 0)
    def _():
        vb_sc[...] = v_sc[...].astype(BF16)
        acc_sc[...] = jnp.zeros_like(acc_sc)

    vb = vb_sc[...]
    v_re = _dot(cr_ref[...], vb)
    v_im = -_dot(sr_ref[...], vb)
    freq = j * tk + lax.broadcasted_iota(jnp.int32, (tk, 1), 0)
    wk = jnp.where(freq == 0, 0.5, 1.0)
    h_re = hre_ref[...] * wk
    h_im = him_ref[...] * wk
    for b in range(n_seq):
        yre_sc[:, cols[b]] = (v_re[:, cols[b]] * h_re - v_im[:, cols[b]] * h_im).astype(BF16)
        yim_sc[:, cols[b]] = (v_re[:, cols[b]] * h_im + v_im[:, cols[b]] * h_re).astype(BF16)
    acc_sc[...] += _dot(cc_ref[...], yre_sc[...]) - _dot(sc_ref[...], yim_sc[...])

    @pl.when(j == nk - 1)
    def _():
        alt = alt_ref[...]
        for b in range(n_seq):
            v = v_sc[:, cols[b]]
            nyq = jnp.sum(v * alt, axis=0, keepdims=True) * nyq_ref[...]
            y = acc_sc[:, cols[b]] * (1.0 / L) + (0.5 / L) * alt * nyq + v * skip_ref[...]

            @pl.when(p == 0)
            def _():
                v_sc[:, cols[b]] = x1_sc[:, cols[b]] * y

            @pl.when(p == 1)
            def _():
                o_ref[b * L:(b + 1) * L, :] = (x2_sc[:, cols[b]] * y).astype(BF16)


def _hyconv_call(hy, conv_w, conv_b, skip, hre, him, nyq, alt, ctab, stab, *, n_seq, L, tok_off):
    tk = min(L, 256)
    nk = L // tk
    seg = tok_off // (n_seq * L)
    width = n_seq * HY_W

    def full(shape):
        return pl.BlockSpec(shape, lambda p, j: (0,) * len(shape))

    vm = lambda dtype: pltpu.VMEM((L, width), dtype)
    return pl.pallas_call(
        functools.partial(_hyconv_kernel, n_seq=n_seq, L=L, tk=tk, nk=nk),
        out_shape=jax.ShapeDtypeStruct((n_seq * L, HY_W), BF16),
        grid=(HY_ORDER, nk),
        in_specs=[
            pl.BlockSpec((n_seq * L, 3 * HY_W), lambda p, j: (seg, 0), pipeline_mode=pl.Buffered(1)),
            full((3, 3 * HY_W)), full((1, 3 * HY_W)),
            pl.BlockSpec((None, 1, HY_W), lambda p, j: (p, 0, 0)),
            pl.BlockSpec((tk, HY_W), lambda p, j: (j, p)),
            pl.BlockSpec((tk, HY_W), lambda p, j: (j, p)),
            pl.BlockSpec((None, 1, HY_W), lambda p, j: (p, 0, 0)),
            full((L, 1)),
            pl.BlockSpec((tk, L), lambda p, j: (j, 0)), pl.BlockSpec((tk, L), lambda p, j: (j, 0)),
            pl.BlockSpec((L, tk), lambda p, j: (0, j)), pl.BlockSpec((L, tk), lambda p, j: (0, j)),
        ],
        out_specs=pl.BlockSpec((n_seq * L, HY_W), lambda p, j: (0, 0)),
        scratch_shapes=[vm(F32), vm(F32), vm(F32), vm(BF16), vm(F32),
                        pltpu.VMEM((tk, width), BF16), pltpu.VMEM((tk, width), BF16)],
        compiler_params=_params(("arbitrary", "arbitrary")),
        name="hyena_long_conv_%d" % L,
    )(hy, conv_w, conv_b, skip, hre, him, nyq, alt, ctab, stab, ctab, stab)


def _fnet_kernel(fcs_ref, c_ref, s_ref, o_ref, *, scale):
    y = _dot(c_ref[...], fcs_ref[:, :FN_W]) - _dot(s_ref[...], fcs_ref[:, FN_W:])
    o_ref[...] = (y * scale).astype(BF16)


def _fnet_call(fcs, ctab, stab, *, n_seq, L, tok_off):
    tk = min(L, 512)
    nk = L // tk
    seq_blk0 = tok_off // L
    return pl.pallas_call(
        functools.partial(_fnet_kernel, scale=1.0 / math.sqrt(L * FN_GROUP_W)),
        out_shape=jax.ShapeDtypeStruct((n_seq * L, FN_W), BF16),
        grid=(n_seq, nk),
        in_specs=[
            pl.BlockSpec((L, 2 * FN_W), lambda b, j: (seq_blk0 + b, 0)),
            pl.BlockSpec((tk, L), lambda b, j: (j, 0)), pl.BlockSpec((tk, L), lambda b, j: (j, 0)),
        ],
        out_specs=pl.BlockSpec((tk, FN_W), lambda b, j: (b * nk + j, 0)),
        compiler_params=_params(("parallel", "parallel")),
        name="fnet_position_dft_%d" % L,
    )(fcs, ctab, stab)


def _mixout_kernel(h_ref, mod_ref, ng_ref, wg_ref, z_ref, a_ref, f_ref, wa_ref, wb_ref, wc_ref, wo_ref, o_ref):
    i = pl.program_id(0)
    m = mod_ref[_group_of(i * TM_MIX)]
    h = h_ref[...]
    n = (_rms(h, ng_ref[...]) * (1.0 + m[4:5]) + m[3:4]).astype(BF16)
    acc = _sigmoid(_dot_nt(n, wg_ref[0:D_MODEL, :])) * _dot(z_ref[...], wa_ref[...])
    acc = acc + _sigmoid(_dot_nt(n, wg_ref[D_MODEL:2 * D_MODEL, :])) * _dot(a_ref[...], wb_ref[...])
    acc = acc + _sigmoid(_dot_nt(n, wg_ref[2 * D_MODEL:, :])) * _dot(f_ref[...], wc_ref[...])
    y = _dot(acc.astype(BF16), wo_ref[...])
    o_ref[...] = h + m[5:6] * y


def _mixout_call(h, mods, ng, w_gate, l, z, a, f, wa, wb, wc, wo):
    tm = TM_MIX

    def full(shape):
        return pl.BlockSpec(shape, lambda i: (0,) * len(shape))

    def tok(width):
        return pl.BlockSpec((tm, width), lambda i: (i, 0))

    return pl.pallas_call(
        _mixout_kernel,
        out_shape=jax.ShapeDtypeStruct((T_ALL, D_MODEL), F32),
        grid=(T_ALL // tm,),
        in_specs=[
            tok(D_MODEL), full((N_GROUPS, 9, D_MODEL)), full((1, D_MODEL)),
            pl.BlockSpec((None, N_BRANCH * D_MODEL, D_MODEL), lambda i: (l, 0, 0)),
            tok(HY_W), tok(N_HEADS * V_DIM), tok(FN_W),
            full((HY_W, D_MODEL)), full((N_HEADS * V_DIM, D_MODEL)), full((FN_W, D_MODEL)),
            full((D_MODEL, D_MODEL)),
        ],
        out_specs=tok(D_MODEL),
        compiler_params=_params(("parallel",)),
        name="gated_merge_out_proj",
    )(h, mods, ng, w_gate, z, a, f, wa, wb, wc, wo)


def _wprep_kernel(w_ref, mix_ref, gate_ref):
    r = pl.program_id(1)
    n_gate = N_BRANCH * D_MODEL
    g0 = IN_GATE - PREP_ROWS
    last = n_gate - (3 * PREP_ROWS - IN_GATE)

    @pl.when(r == 0)
    def _():
        mix_ref[:MIX_FN, :] = w_ref[...].astype(BF16)

    @pl.when(r == 1)
    def _():
        k_r = w_ref[0:ROPE_DIM, :]
        swapped = jnp.concatenate([-k_r[8:16], k_r[0:8], -k_r[24:32], k_r[16:24]], axis=0)
        pad = jnp.zeros((128 - ROPE_DIM, D_MODEL), BF16)
        mix_ref[MIX_FN:MIX_KR, :] = w_ref[ROPE_DIM:g0, :].astype(BF16)
        mix_ref[MIX_KR:MIX_KR + ROPE_DIM, :] = k_r.astype(BF16)
        mix_ref[MIX_KR + ROPE_DIM:MIX_KRS, :] = pad
        mix_ref[MIX_KRS:MIX_KRS + ROPE_DIM, :] = swapped.astype(BF16)
        mix_ref[MIX_KRS + ROPE_DIM:, :] = pad
        gate_ref[:PREP_ROWS - g0, :] = w_ref[g0:, :].astype(BF16)

    @pl.when(r == 2)
    def _():
        gate_ref[PREP_ROWS - g0:2 * PREP_ROWS - g0, :] = w_ref[...].astype(BF16)

    @pl.when(r == 3)
    def _():
        gate_ref[2 * PREP_ROWS - g0:, :] = w_ref[:last, :].astype(BF16)


def _wprep_call(w_in_t):
    assert IN_KR == PREP_ROWS and 4 * PREP_ROWS >= IN_COLS
    return pl.pallas_call(
        _wprep_kernel,
        out_shape=(jax.ShapeDtypeStruct((DEPTH, MIX_W, D_MODEL), BF16),
                   jax.ShapeDtypeStruct((DEPTH, N_BRANCH * D_MODEL, D_MODEL), BF16)),
        grid=(DEPTH, 4),
        in_specs=[pl.BlockSpec((None, PREP_ROWS, D_MODEL), lambda l, r: (l, r, 0))],
        out_specs=(pl.BlockSpec((None, MIX_W, D_MODEL), lambda l, r: (l, 0, 0)),
                   pl.BlockSpec((None, N_BRANCH * D_MODEL, D_MODEL), lambda l, r: (l, 0, 0))),
        compiler_params=_params(("arbitrary", "arbitrary")),
        name="input_proj_weight_relayout",
    )(w_in_t)


def _layer_weights(l, w_qb, w_kvb):
    sign = jnp.asarray(_ROPE_SIGN)
    wq3 = w_qb[l].reshape(Q_LORA, N_HEADS, NOPE_DIM + ROPE_DIM)
    slot_pad = HEAD_SLOT - NOPE_DIM - ROPE_DIM
    wq = jnp.pad(wq3, ((0, 0), (0, 0), (0, slot_pad))).reshape(Q_LORA, QK_W).astype(BF16)
    wq_sw = wq3[:, :, NOPE_DIM:][:, :, _ROPE_PERM] * sign
    wqs = jnp.pad(wq_sw, ((0, 0), (0, 0), (NOPE_DIM, slot_pad))).reshape(Q_LORA, QK_W).astype(BF16)

    wkv3 = w_kvb[l].reshape(KV_LORA, N_HEADS, NOPE_DIM + V_DIM)
    wk = jnp.pad(wkv3[:, :, :NOPE_DIM], ((0, 0), (0, 0), (0, HEAD_SLOT - NOPE_DIM))).reshape(KV_LORA, QK_W).astype(BF16)
    wv = wkv3[:, :, NOPE_DIM:].reshape(KV_LORA, N_HEADS * V_DIM).astype(BF16)
    return wq, wqs, wk, wv


def kernel(x_prompt, x_sample, cache_ckv, cache_krope, c, c_ctx, w_ada, b_ada, norm_g, w_ffn_up, w_ffn_down,
           w_in, hy_conv_w, hy_conv_b, hy_filt_w1, hy_filt_b1, hy_filt_w2, hy_filt_b2, hy_filt_w3, hy_skip,
           w_hy_out, q_norm_g, w_qb, kv_norm_g, w_kvb, w_mla_o, w_fnet, w_out, final_g):
    tabs = _tables()
    cosq, sinq, cosk, sink = (jnp.asarray(t) for t in tabs["rope"])
    fch = jnp.asarray(tabs["fnch"]).astype(BF16)
    ek = jnp.asarray(tabs["ropeexp"]).astype(BF16)
    dft = {}
    for L in (SEQ, DEC_SEQ):
        dft[("hy", L)] = tuple(jnp.asarray(t).astype(BF16) for t in tabs[("hy", L)])
        dft[("fn", L)] = tuple(jnp.asarray(t).astype(BF16) for t in tabs[("fn", L)])

    cvec = jnp.concatenate([c_ctx[None, :], c, jnp.zeros((8 - N_GROUPS, D_MODEL), F32)], axis=0).T
    ada = _ada_call(cvec, w_ada, b_ada)
    mods_all = ada[:, :N_GROUPS].reshape(DEPTH, N_GROUPS, 9, D_MODEL)

    w_mix, w_gate = _wprep_call(jnp.swapaxes(w_in, 1, 2))
    w_hy_out, w_mla_o, w_fnet, w_out = (w.astype(BF16) for w in (w_hy_out, w_mla_o, w_fnet, w_out))
    hs = (x_prompt.reshape(T_CTX, D_MODEL), x_sample.reshape(T_LAT, D_MODEL))
    segs = ((BATCH, SEQ, 0), (DEC_BATCH, DEC_SEQ, T_CTX))
    ckv_out = []
    kr_out = []
    for l in range(DEPTH):
        mods = mods_all[l]
        ng = norm_g[l]
        wq, wqs, wk, wv = _layer_weights(l, w_qb, w_kvb)

        h = _ffn_call(hs, mods, ng[0:1], w_ffn_up, w_ffn_down, l, 0, 0)

        hy, q, k, v, ckv, k_r, fcs = _mixin_call(
            h, mods, ng[1:2], w_mix, l, q_norm_g[l][None, :], kv_norm_g[l][None, :], wq, wqs, wk, wv, ek, fch,
            cosq, sinq, cosk, sink)
        ckv_out.append(ckv[:T_CTX].reshape(BATCH, SEQ, KV_LORA))
        kr_out.append(k_r[:T_CTX].reshape(BATCH, SEQ, ROPE_DIM))

        kc, vc = _cachekv_call(cache_ckv, cache_krope, wk, wv, ek, l)

        w1 = jnp.pad(hy_filt_w1[l], ((0, POS_PAD - POS_EMB), (0, 0)))
        skip = hy_skip[l].reshape(HY_ORDER, 1, HY_W)
        z_parts, a_parts, f_parts = [], [], []
        for n_seq, L, off in segs:
            feats, decay, alt = (jnp.asarray(t) for t in tabs[("filt", L)])
            c_hy, s_hy = dft[("hy", L)]
            hre, him, nyq = _hyfilt_call(L, feats, w1, hy_filt_b1[l][None, :], hy_filt_w2[l],
                                         hy_filt_b2[l][None, :], hy_filt_w3[l], decay, alt, c_hy, s_hy)
            z_parts.append(_hyconv_call(hy, hy_conv_w[l], hy_conv_b[l][None, :], skip, hre, him,
                                        nyq.reshape(HY_ORDER, 1, HY_W), alt, c_hy, s_hy,
                                        n_seq=n_seq, L=L, tok_off=off))
            cache = (kc, vc) if off else (None, None)
            a_parts.append(_attn_call(q, k, v, *cache, n_seq=n_seq, seq_len=L, tok_off=off, tq=min(L, 512)))
            c_fn, s_fn = dft[("fn", L)]
            f_parts.append(_fnet_call(fcs, c_fn, s_fn, n_seq=n_seq, L=L, tok_off=off))
        z = jnp.concatenate(z_parts, axis=0)
        a = jnp.concatenate(a_parts, axis=0)
        f = jnp.concatenate(f_parts, axis=0)

        h = _mixout_call(h, mods, ng[1:2], w_gate, l, z, a, f, w_hy_out[l], w_mla_o[l], w_fnet[l], w_out[l])

        last = l == DEPTH - 1
        out = _ffn_call((h,), mods, ng[2:3], w_ffn_up, w_ffn_down, l, 1, 6, final_g[None, :] if last else None)
        hs = out if last else (out,)

    y_prompt = hs[0].reshape(BATCH, SEQ, D_MODEL)
    y_sample = hs[1].reshape(DEC_BATCH, DEC_SEQ, D_MODEL)
    return y_prompt, y_sample, jnp.stack(ckv_out, axis=1), jnp.stack(kr_out, axis=1)
```

```python
import functools
import math

import numpy as np
import jax
import jax.numpy as jnp
from jax import lax
from jax.experimental import pallas as pl
from jax.experimental.pallas import tpu as pltpu

F32 = jnp.float32
BF16 = jnp.bfloat16
HIGHEST = lax.Precision.HIGHEST

D_MODEL = 1024
BATCH = 16
SEQ = 256
DEPTH = 2
DEC_BATCH = 2
DEC_SEQ = 2048
PAST_LEN = 512
GRID_W = 64
HY_W = 256
HY_ORDER = 2
N_BANDS = 8
POS_EMB = 1 + 2 * N_BANDS
FILT_HID = 64
N_HEADS = 8
Q_LORA = 256
KV_LORA = 128
NOPE_DIM = 64
ROPE_DIM = 32
V_DIM = 64
ROPE_BASE = 10000.0
FN_GROUPS = 4
FN_GROUP_W = 64
FN_W = FN_GROUPS * FN_GROUP_W
N_BRANCH = 3
D_FF = 2816
EPS = 1e-6

T_CTX = BATCH * SEQ
T_LAT = DEC_BATCH * DEC_SEQ
T_ALL = T_CTX + T_LAT
N_GROUPS = 1 + DEC_BATCH
HEAD_SLOT = 128
QK_W = N_HEADS * HEAD_SLOT
PAIR_W = 2 * V_DIM
V_EXT_W = N_HEADS * PAIR_W
POS_PAD = 128

VMEM_LIMIT = 56 * 1024 * 1024

TM_FFN = 512
TF_FFN = 256
FFN_LOAD = D_FF // TF_FFN
TM_MIX = 512
PREP_ROWS = 1152

IN_KR = 3 * HY_W + Q_LORA + KV_LORA
IN_FN = IN_KR + ROPE_DIM
IN_GATE = IN_FN + FN_W
IN_COLS = IN_GATE + N_BRANCH * D_MODEL

MIX_HY = 0
MIX_QA = 3 * HY_W
MIX_KVA = MIX_QA + Q_LORA
MIX_FN = MIX_KVA + KV_LORA
MIX_KR = MIX_FN + FN_W
MIX_KRS = MIX_KR + 128
MIX_W = MIX_KRS + 128

_ROPE_PERM = np.array(list(range(8, 16)) + list(range(0, 8)) + list(range(24, 32)) + list(range(16, 24)))
_ROPE_SIGN = np.array([-1.0] * 8 + [1.0] * 8 + [-1.0] * 8 + [1.0] * 8, np.float32)


def _dft_tables(L, half):
    k = np.arange(L, dtype=np.int64)
    period = 2 * L if half else L
    m = (k[:, None] * k[None, :]) % period
    ang = 2.0 * np.pi * m.astype(np.float64) / period
    return np.cos(ang).astype(np.float32), np.sin(ang).astype(np.float32)


def _filter_tables(L):
    t = np.arange(L, dtype=np.float64)
    t_norm = t / (L - 1)
    w = 2.0 * np.pi * t / L
    bands = np.linspace(1e-4, N_BANDS - 1, N_BANDS)
    ang = w[:, None] * bands[None, :]
    feats = np.concatenate([t_norm[:, None], np.cos(ang), -np.sin(ang)], axis=-1)
    feats = np.pad(feats, ((0, 0), (0, POS_PAD - POS_EMB)))
    deltas = np.linspace(math.log(1e-2) / 1.5, math.log(1e-2) / 0.3, HY_W)
    decay = np.exp(-t_norm[:, None] * np.abs(deltas)[None, :])
    decay = np.concatenate([decay, decay], axis=1)
    alt = np.where(np.arange(L) % 2 == 0, 1.0, -1.0)[:, None]
    return feats.astype(np.float32), decay.astype(np.float32), alt.astype(np.float32)


def _rope_tables():
    t = np.arange(DEC_SEQ)
    row = (t // GRID_W).astype(np.float64)
    col = (t % GRID_W).astype(np.float64)
    nf = ROPE_DIM // 4
    inv = ROPE_BASE ** (-np.arange(nf, dtype=np.float64) / nf)
    ar = row[:, None] * inv[None, :]
    ac = col[:, None] * inv[None, :]
    cos32 = np.concatenate([np.cos(ar), np.cos(ar), np.cos(ac), np.cos(ac)], axis=1)
    sin32 = np.concatenate([np.sin(ar), np.sin(ar), np.sin(ac), np.sin(ac)], axis=1)
    cos32 = np.concatenate([cos32, np.ones((TM_MIX, ROPE_DIM))], axis=0)
    sin32 = np.concatenate([sin32, np.zeros((TM_MIX, ROPE_DIM))], axis=0)
    n = cos32.shape[0]
    cosq = np.ones((n, HEAD_SLOT))
    sinq = np.zeros((n, HEAD_SLOT))
    cosq[:, NOPE_DIM:NOPE_DIM + ROPE_DIM] = cos32
    sinq[:, NOPE_DIM:NOPE_DIM + ROPE_DIM] = sin32
    return (cosq.astype(np.float32), sinq.astype(np.float32), cos32.astype(np.float32), sin32.astype(np.float32))


def _fnet_channel_table():
    j = np.arange(FN_GROUP_W)
    ang = 2.0 * np.pi * ((j[:, None] * j[None, :]) % FN_GROUP_W) / FN_GROUP_W
    out = np.zeros((FN_W, 2 * FN_W))
    for g in range(FN_GROUPS):
        sl = slice(g * FN_GROUP_W, (g + 1) * FN_GROUP_W)
        out[sl, sl] = np.cos(ang)
        out[sl, FN_W + g * FN_GROUP_W:FN_W + (g + 1) * FN_GROUP_W] = np.sin(ang)
    return out.astype(np.float32)


def _rope_expand_table():
    e = np.zeros((ROPE_DIM, N_HEADS, HEAD_SLOT), np.float32)
    for j in range(ROPE_DIM):
        e[j, :, NOPE_DIM + j] = 1.0
    return e.reshape(ROPE_DIM, QK_W)


_TABLES = {}


def _tables():
    if not _TABLES:
        for L in (SEQ, DEC_SEQ):
            _TABLES[("hy", L)] = _dft_tables(L, True)
            _TABLES[("fn", L)] = _dft_tables(L, False)
            _TABLES[("filt", L)] = _filter_tables(L)
        _TABLES["rope"] = _rope_tables()
        _TABLES["fnch"] = _fnet_channel_table()
        _TABLES["ropeexp"] = _rope_expand_table()
    return _TABLES


def _rms(x, g):
    ms = jnp.mean(x * x, axis=-1, keepdims=True)
    return x * lax.rsqrt(ms + EPS) * g


def _sigmoid(x):
    return 1.0 / (1.0 + jnp.exp(-x))


def _dot(a, b):
    return jnp.dot(a, b, preferred_element_type=F32)


def _dot_hi(a, b):
    return jnp.dot(a, b, precision=HIGHEST, preferred_element_type=F32)


def _dot_nt(a, b):
    return lax.dot_general(a, b, (((1,), (1,)), ((), ())), preferred_element_type=F32)


def _store_values(v_ref, v):
    ones = jnp.ones((v.shape[0], PAIR_W), BF16)
    for pair in range(N_HEADS // 2):
        v_ref[:, 2 * pair * PAIR_W:(2 * pair + 1) * PAIR_W] = v[:, pair * PAIR_W:(pair + 1) * PAIR_W].astype(BF16)
        v_ref[:, (2 * pair + 1) * PAIR_W:(2 * pair + 2) * PAIR_W] = ones


def _group_of(tok0):
    return jnp.where(tok0 < T_CTX, 0, 1 + (tok0 - T_CTX) // DEC_SEQ)


def _params(sem):
    return pltpu.CompilerParams(dimension_semantics=sem, vmem_limit_bytes=VMEM_LIMIT)


def _ada_kernel(c_ref, w_ref, b_ref, o_ref):
    x = c_ref[...]
    s = x * _sigmoid(x)
    w = w_ref[...]
    o_ref[...] = jnp.zeros_like(o_ref)
    for g in range(N_GROUPS):
        o_ref[g:g + 1, :] = jnp.sum(w * s[:, g:g + 1], axis=0, keepdims=True) + b_ref[...]


def _ada_call(cvec, w_ada, b_ada):
    tn = 1024
    n_out = 9 * D_MODEL
    return pl.pallas_call(
        _ada_kernel,
        out_shape=jax.ShapeDtypeStruct((DEPTH, 8, n_out), F32),
        grid=(DEPTH, n_out // tn),
        in_specs=[
            pl.BlockSpec((D_MODEL, 8), lambda l, j: (0, 0)),
            pl.BlockSpec((None, D_MODEL, tn), lambda l, j: (l, 0, j)),
            pl.BlockSpec((None, 1, tn), lambda l, j: (l, 0, j)),
        ],
        out_specs=pl.BlockSpec((None, 8, tn), lambda l, j: (l, 0, j)),
        compiler_params=_params(("arbitrary", "arbitrary")),
        name="ada_modulation",
    )(cvec, w_ada, b_ada.reshape(DEPTH, 1, n_out))


def _ffn_kernel(*refs, j0, split_in, final):
    refs = list(refs)
    x_refs = [refs.pop(0) for _ in range(2 if split_in else 1)]
    mod_ref, ng_ref, wg_ref, wu_ref, wd_ref = refs[:5]
    refs = refs[5:]
    fg_ref = refs.pop(0) if final else None
    o_refs = [refs.pop(0) for _ in range(2 if final else 1)]
    wg_sc, wu_sc, wd_sc, hid_sc = refs
    s = pl.program_id(0)
    n_ctx_tiles = T_CTX // TM_FFN

    @pl.when(s < FFN_LOAD)
    def _():
        wg_sc[s] = wg_ref[...].astype(BF16)
        wu_sc[s] = wu_ref[...].astype(BF16)
        wd_sc[pl.ds(pl.multiple_of(s * TF_FFN, TF_FFN), TF_FFN), :] = wd_ref[...].astype(BF16)

    @pl.when(s >= FFN_LOAD)
    def _():
        t = s - FFN_LOAD
        if split_in:
            x = jnp.where(t < n_ctx_tiles, x_refs[0][...], x_refs[1][...])
        else:
            x = x_refs[0][...]
        m = mod_ref[_group_of(t * TM_FFN)]
        n = (_rms(x, ng_ref[...]) * (1.0 + m[j0 + 1:j0 + 2]) + m[j0:j0 + 1]).astype(BF16)
        for c in range(FFN_LOAD):
            g = _dot(n, wg_sc[c])
            u = _dot(n, wu_sc[c])
            hid_sc[:, c * TF_FFN:(c + 1) * TF_FFN] = (g * _sigmoid(g) * u).astype(BF16)
        y = x + 0.5 * m[j0 + 2:j0 + 3] * _dot(hid_sc[...], wd_sc[...])
        if final:
            y = _rms(y, fg_ref[...])

            @pl.when(t < n_ctx_tiles)
            def _():
                o_refs[0][...] = y

            @pl.when(t >= n_ctx_tiles)
            def _():
                o_refs[1][...] = y
        else:
            o_refs[0][...] = y


def _ffn_call(xs, mods, ng, w_up, w_down, l, f, j0, final_g=None):
    split_in = len(xs) == 2
    final = final_g is not None
    n_ctx_tiles = T_CTX // TM_FFN
    tile = lambda s: jnp.maximum(s - FFN_LOAD, 0)
    chunk = lambda s: jnp.minimum(s, FFN_LOAD - 1)
    ctx_blk = lambda s: (jnp.minimum(tile(s), n_ctx_tiles - 1), 0)
    lat_blk = lambda s: (jnp.maximum(tile(s) - n_ctx_tiles, 0), 0)
    row = pl.BlockSpec((1, D_MODEL), lambda s: (0, 0))
    tok = lambda index_map: pl.BlockSpec((TM_FFN, D_MODEL), index_map)
    if split_in:
        in_specs = [tok(ctx_blk), tok(lat_blk)]
    else:
        in_specs = [tok(lambda s: (tile(s), 0))]
    in_specs += [
        pl.BlockSpec((N_GROUPS, 9, D_MODEL), lambda s: (0, 0, 0)),
        row,
        pl.BlockSpec((None, None, D_MODEL, TF_FFN), lambda s: (l, f, 0, chunk(s))),
        pl.BlockSpec((None, None, D_MODEL, TF_FFN), lambda s: (l, f, 0, chunk(s) + FFN_LOAD)),
        pl.BlockSpec((None, None, TF_FFN, D_MODEL), lambda s: (l, f, chunk(s), 0)),
    ]
    args = list(xs) + [mods, ng, w_up, w_up, w_down]
    if final:
        in_specs.append(row)
        args.append(final_g)
        out_shape = (jax.ShapeDtypeStruct((T_CTX, D_MODEL), F32), jax.ShapeDtypeStruct((T_LAT, D_MODEL), F32))
        out_specs = (tok(ctx_blk), tok(lat_blk))
    else:
        out_shape = jax.ShapeDtypeStruct((T_ALL, D_MODEL), F32)
        out_specs = tok(lambda s: (tile(s), 0))
    return pl.pallas_call(
        functools.partial(_ffn_kernel, j0=j0, split_in=split_in, final=final),
        out_shape=out_shape,
        grid=(FFN_LOAD + T_ALL // TM_FFN,),
        in_specs=in_specs,
        out_specs=out_specs,
        scratch_shapes=[pltpu.VMEM((FFN_LOAD, D_MODEL, TF_FFN), BF16), pltpu.VMEM((FFN_LOAD, D_MODEL, TF_FFN), BF16),
                        pltpu.VMEM((D_FF, D_MODEL), BF16), pltpu.VMEM((TM_FFN, D_FF), BF16)],
        compiler_params=_params(("arbitrary",)),
        name="swiglu_half_step",
    )(*args)


def _mixin_kernel(h_ref, mod_ref, ng_ref, w_ref, qg_ref, kvg_ref, wq_ref, wqs_ref, wk_ref, wv_ref, ek_ref,
                  fch_ref, cq_ref, sq_ref, ck_ref, sk_ref,
                  hy_ref, q_ref, k_ref, v_ref, ckv_ref, kr_ref, fcs_ref):
    i = pl.program_id(0)
    m = mod_ref[_group_of(i * TM_MIX)]
    n = (_rms(h_ref[...], ng_ref[...]) * (1.0 + m[4:5]) + m[3:4]).astype(BF16)
    proj = _dot_nt(n, w_ref[...])
    hy_ref[...] = proj[:, MIX_HY:MIX_QA]
    q_a = proj[:, MIX_QA:MIX_KVA]
    kv_a = proj[:, MIX_KVA:MIX_FN]
    fn = proj[:, MIX_FN:MIX_KR]
    k_r = proj[:, MIX_KR:MIX_KR + ROPE_DIM]
    k_r_swapped = proj[:, MIX_KRS:MIX_KRS + ROPE_DIM]

    qn = _rms(q_a, qg_ref[...]).astype(BF16)
    q = _dot(qn, wq_ref[...])
    q_swapped = _dot(qn, wqs_ref[...])
    cos_q = cq_ref[...]
    sin_q = sq_ref[...]
    for h in range(N_HEADS):
        hs = slice(h * HEAD_SLOT, (h + 1) * HEAD_SLOT)
        q_ref[:, hs] = (q[:, hs] * cos_q + q_swapped[:, hs] * sin_q).astype(BF16)

    ckv = _rms(kv_a, kvg_ref[...])
    ckv_ref[...] = ckv
    kr_ref[...] = k_r
    ckv_b = ckv.astype(BF16)
    k_rot = (k_r * ck_ref[...] + k_r_swapped * sk_ref[...]).astype(BF16)
    k_ref[...] = (_dot(ckv_b, wk_ref[...]) + _dot(k_rot, ek_ref[...])).astype(BF16)
    _store_values(v_ref, _dot(ckv_b, wv_ref[...]))
    fcs_ref[...] = _dot(fn.astype(BF16), fch_ref[...]).astype(BF16)


def _mixin_call(h, mods, ng, w_mix, l, qg, kvg, wq, wqs, wk, wv, ek, fch, cosq, sinq, cosk, sink):
    tm = TM_MIX

    def pos_block(i):
        tok0 = i * tm
        return jnp.where(tok0 < T_CTX, DEC_SEQ // tm, ((tok0 - T_CTX) % DEC_SEQ) // tm)

    def full(shape):
        return pl.BlockSpec(shape, lambda i: (0,) * len(shape))

    def tok(width):
        return pl.BlockSpec((tm, width), lambda i: (i, 0))

    def pos(width):
        return pl.BlockSpec((tm, width), lambda i: (pos_block(i), 0))

    out_shape = (
        jax.ShapeDtypeStruct((T_ALL, 3 * HY_W), F32),
        jax.ShapeDtypeStruct((T_ALL, QK_W), BF16),
        jax.ShapeDtypeStruct((T_ALL, QK_W), BF16),
        jax.ShapeDtypeStruct((T_ALL, V_EXT_W), BF16),
        jax.ShapeDtypeStruct((T_ALL, KV_LORA), F32),
        jax.ShapeDtypeStruct((T_ALL, ROPE_DIM), F32),
        jax.ShapeDtypeStruct((T_ALL, 2 * FN_W), BF16),
    )
    return pl.pallas_call(
        _mixin_kernel,
        out_shape=out_shape,
        grid=(T_ALL // tm,),
        in_specs=[
            tok(D_MODEL), full((N_GROUPS, 9, D_MODEL)), full((1, D_MODEL)),
            pl.BlockSpec((None, MIX_W, D_MODEL), lambda i: (l, 0, 0)),
            full((1, Q_LORA)), full((1, KV_LORA)), full((Q_LORA, QK_W)), full((Q_LORA, QK_W)),
            full((KV_LORA, QK_W)), full((KV_LORA, N_HEADS * V_DIM)), full((ROPE_DIM, QK_W)),
            full((FN_W, 2 * FN_W)), pos(HEAD_SLOT), pos(HEAD_SLOT), pos(ROPE_DIM), pos(ROPE_DIM),
        ],
        out_specs=tuple(tok(s.shape[1]) for s in out_shape),
        compiler_params=_params(("parallel",)),
        name="mixer_input_proj",
    )(h, mods, ng, w_mix, qg, kvg, wq, wqs, wk, wv, ek, fch, cosq, sinq, cosk, sink)


def _cachekv_kernel(ckv_ref, kr_ref, wk_ref, wv_ref, ek_ref, k_ref, v_ref):
    ckv_b = ckv_ref[...].astype(BF16)
    k_ref[...] = (_dot(ckv_b, wk_ref[...]) + _dot(kr_ref[...].astype(BF16), ek_ref[...])).astype(BF16)
    _store_values(v_ref, _dot(ckv_b, wv_ref[...]))


def _cachekv_call(cache_ckv, cache_krope, wk, wv, ek, l):
    def full(shape):
        return pl.BlockSpec(shape, lambda b: (0,) * len(shape))

    return pl.pallas_call(
        _cachekv_kernel,
        out_shape=(jax.ShapeDtypeStruct((DEC_BATCH, PAST_LEN, QK_W), BF16),
                   jax.ShapeDtypeStruct((DEC_BATCH, PAST_LEN, V_EXT_W), BF16)),
        grid=(DEC_BATCH,),
        in_specs=[
            pl.BlockSpec((None, None, PAST_LEN, KV_LORA), lambda b: (b, l, 0, 0)),
            pl.BlockSpec((None, None, PAST_LEN, ROPE_DIM), lambda b: (b, l, 0, 0)),
            full((KV_LORA, QK_W)), full((KV_LORA, N_HEADS * V_DIM)), full((ROPE_DIM, QK_W)),
        ],
        out_specs=(pl.BlockSpec((None, PAST_LEN, QK_W), lambda b: (b, 0, 0)),
                   pl.BlockSpec((None, PAST_LEN, V_EXT_W), lambda b: (b, 0, 0))),
        compiler_params=_params(("parallel",)),
        name="cached_context_kv",
    )(cache_ckv, cache_krope, wk, wv, ek)


def _attn_kernel(q_ref, k_ref, v_ref, *rest, has_cache):
    if has_cache:
        kc_ref, vc_ref, o_ref = rest
    else:
        (o_ref,) = rest
    scale2 = math.log2(math.e) / math.sqrt(NOPE_DIM + ROPE_DIM)
    first_half = lax.broadcasted_iota(jnp.int32, (1, PAIR_W), 1) < V_DIM
    for pair in range(N_HEADS // 2):
        vs = slice(2 * pair * PAIR_W, (2 * pair + 2) * PAIR_W)
        outs = []
        for e in range(2):
            hs = slice((2 * pair + e) * HEAD_SLOT, (2 * pair + e + 1) * HEAD_SLOT)
            qh = q_ref[:, hs]
            s = _dot_nt(qh, k_ref[:, hs])
            mx = jnp.max(s, axis=-1, keepdims=True)
            if has_cache:
                sc = _dot_nt(qh, kc_ref[:, hs])
                mx = jnp.maximum(mx, jnp.max(sc, axis=-1, keepdims=True))
            o = _dot(jnp.exp2((s - mx) * scale2).astype(BF16), v_ref[:, vs])
            if has_cache:
                o = o + _dot(jnp.exp2((sc - mx) * scale2).astype(BF16), vc_ref[:, vs])
            outs.append(o[:, :PAIR_W] / o[:, PAIR_W:PAIR_W + 1])
        o_ref[:, pair * PAIR_W:(pair + 1) * PAIR_W] = jnp.where(first_half, outs[0], outs[1]).astype(BF16)


def _attn_call(q, k, v, kc, vc, *, n_seq, seq_len, tok_off, tq):
    has_cache = kc is not None
    nq = seq_len // tq
    seq_blk0 = tok_off // seq_len
    q_blk0 = tok_off // tq
    vw = N_HEADS * V_DIM
    in_specs = [
        pl.BlockSpec((tq, QK_W), lambda b, j: (q_blk0 + b * nq + j, 0)),
        pl.BlockSpec((seq_len, QK_W), lambda b, j: (seq_blk0 + b, 0)),
        pl.BlockSpec((seq_len, V_EXT_W), lambda b, j: (seq_blk0 + b, 0)),
    ]
    args = [q, k, v]
    if has_cache:
        in_specs += [pl.BlockSpec((None, PAST_LEN, QK_W), lambda b, j: (b, 0, 0)),
                     pl.BlockSpec((None, PAST_LEN, V_EXT_W), lambda b, j: (b, 0, 0))]
        args += [kc, vc]
    return pl.pallas_call(
        functools.partial(_attn_kernel, has_cache=has_cache),
        out_shape=jax.ShapeDtypeStruct((n_seq * seq_len, vw), BF16),
        grid=(n_seq, nq),
        in_specs=in_specs,
        out_specs=pl.BlockSpec((tq, vw), lambda b, j: (b * nq + j, 0)),
        compiler_params=_params(("parallel", "parallel")),
        name="mla_attention_cache" if has_cache else "mla_attention",
    )(*args)


def _hyfilt_kernel(feat_ref, w1_ref, b1_ref, w2_ref, b2_ref, w3_ref, dec_ref, alt_ref, c_ref, s_ref,
                   hre_ref, him_ref, nyq_ref, a_sc, b_sc, *, L):
    j = pl.program_id(0)
    half = HY_ORDER * HY_W

    @pl.when(j == 0)
    def _():
        h = jnp.sin(_dot_hi(feat_ref[...], w1_ref[...]) + b1_ref[...])
        h = jnp.sin(_dot_hi(h, w2_ref[...]) + b2_ref[...])
        h = _dot_hi(h, w3_ref[...])
        dec = dec_ref[...]
        rows = lax.broadcasted_iota(jnp.int32, (L, 1), 0)
        h_fwd = h[:, :half] * dec
        h_bwd = jnp.where(rows > 0, h[:, half:] * dec, 0.0)
        norm = (jnp.sum(jnp.abs(h_fwd), axis=0, keepdims=True)
                + jnp.sum(jnp.abs(h_bwd), axis=0, keepdims=True) + EPS)
        inv = 1.0 / norm
        a = (h_fwd + h_bwd) * inv
        b = (h_fwd - h_bwd) * inv
        a_sc[...] = a.astype(BF16)
        b_sc[...] = b.astype(BF16)
        nyq_ref[...] = jnp.sum(a * alt_ref[...], axis=0, keepdims=True)

    hre_ref[...] = _dot(c_ref[...], a_sc[...])
    him_ref[...] = -_dot(s_ref[...], b_sc[...])


def _hyfilt_call(L, feats, w1, b1, w2, b2, w3, decay, alt, ctab, stab):
    tk = min(L, 512)
    half = HY_ORDER * HY_W

    def full(shape):
        return pl.BlockSpec(shape, lambda j: (0,) * len(shape))

    return pl.pallas_call(
        functools.partial(_hyfilt_kernel, L=L),
        out_shape=(jax.ShapeDtypeStruct((L, half), F32), jax.ShapeDtypeStruct((L, half), F32),
                   jax.ShapeDtypeStruct((1, half), F32)),
        grid=(L // tk,),
        in_specs=[
            full((L, POS_PAD)), full((POS_PAD, FILT_HID)), full((1, FILT_HID)), full((FILT_HID, FILT_HID)),
            full((1, FILT_HID)), full((FILT_HID, 2 * half)), full((L, half)), full((L, 1)),
            pl.BlockSpec((tk, L), lambda j: (j, 0)), pl.BlockSpec((tk, L), lambda j: (j, 0)),
        ],
        out_specs=(pl.BlockSpec((tk, half), lambda j: (j, 0)), pl.BlockSpec((tk, half), lambda j: (j, 0)),
                   full((1, half))),
        scratch_shapes=[pltpu.VMEM((L, half), BF16), pltpu.VMEM((L, half), BF16)],
        compiler_params=_params(("arbitrary",)),
        name="hyena_filter_spectrum_%d" % L,
    )(feats, w1, b1, w2, b2, w3, decay, alt, ctab, stab)


def _hyconv_kernel(hy_ref, cw_ref, cb_ref, skip_ref, hre_ref, him_ref, nyq_ref, alt_ref, c_ref, s_ref, o_ref,
                   x1_sc, x2_sc, v_sc, vb_sc, yre_sc, yim_sc, nv_sc, *, n_seq, L, tk):
    p = pl.program_id(0)
    ph = pl.program_id(1)
    j = pl.program_id(2)
    cols = [slice(b * HY_W, (b + 1) * HY_W) for b in range(n_seq)]
    blk = pl.ds(pl.multiple_of(j * tk, tk), tk)

    @pl.when((p == 0) & (ph == 0) & (j == 0))
    def _():
        rows = lax.broadcasted_iota(jnp.int32, (L, 1), 0)
        w = cw_ref[...]
        for b in range(n_seq):
            x = hy_ref[b * L:(b + 1) * L, :]
            prev = jnp.where(rows > 0, pltpu.roll(x, 1, 0), 0.0)
            nxt = jnp.where(rows < L - 1, pltpu.roll(x, L - 1, 0), 0.0)
            u = prev * w[0:1] + x * w[1:2] + nxt * w[2:3] + cb_ref[...]
            x1_sc[:, cols[b]] = u[:, :HY_W]
            x2_sc[:, cols[b]] = u[:, HY_W:2 * HY_W]
            v_sc[:, cols[b]] = u[:, 2 * HY_W:]

    @pl.when((ph == 0) & (j == 0))
    def _():
        v = v_sc[...]
        vb_sc[...] = v.astype(BF16)
        nv_sc[...] = jnp.sum(v * alt_ref[...], axis=0, keepdims=True)

    @pl.when(ph == 0)
    def _():
        vb = vb_sc[...]
        v_re = _dot(c_ref[...], vb)
        v_im = -_dot(s_ref[...], vb)
        freq = j * tk + lax.broadcasted_iota(jnp.int32, (tk, 1), 0)
        wk = jnp.where(freq == 0, 0.5, 1.0)
        h_re = hre_ref[...] * wk
        h_im = him_ref[...] * wk
        for b in range(n_seq):
            yre_sc[blk, cols[b]] = (v_re[:, cols[b]] * h_re - v_im[:, cols[b]] * h_im).astype(BF16)
            yim_sc[blk, cols[b]] = (v_re[:, cols[b]] * h_im + v_im[:, cols[b]] * h_re).astype(BF16)

    @pl.when(ph == 1)
    def _():
        acc = _dot(c_ref[...], yre_sc[...]) - _dot(s_ref[...], yim_sc[...])
        alt = alt_ref[blk, :]
        for b in range(n_seq):
            v = v_sc[blk, cols[b]]
            nyq = nv_sc[:, cols[b]] * nyq_ref[...]
            y = acc[:, cols[b]] * (1.0 / L) + (0.5 / L) * alt * nyq + v * skip_ref[...]

            @pl.when(p == 0)
            def _():
                v_sc[blk, cols[b]] = x1_sc[blk, cols[b]] * y

            @pl.when(p == 1)
            def _():
                o_ref[pl.ds(pl.multiple_of(b * L + j * tk, tk), tk), :] = (x2_sc[blk, cols[b]] * y).astype(BF16)


def _hyconv_call(hy, conv_w, conv_b, skip, hre, him, nyq, alt, ctab, stab, *, n_seq, L, tok_off):
    tk = min(L, 256)
    nk = L // tk
    seg = tok_off // (n_seq * L)
    width = n_seq * HY_W

    def full(shape):
        return pl.BlockSpec(shape, lambda p, ph, j: (0,) * len(shape))

    def filt(p, ph, j):
        return (jnp.where(ph == 0, j, nk - 1), p)

    vm = lambda dtype: pltpu.VMEM((L, width), dtype)
    return pl.pallas_call(
        functools.partial(_hyconv_kernel, n_seq=n_seq, L=L, tk=tk),
        out_shape=jax.ShapeDtypeStruct((n_seq * L, HY_W), BF16),
        grid=(HY_ORDER, 2, nk),
        in_specs=[
            pl.BlockSpec((n_seq * L, 3 * HY_W), lambda p, ph, j: (seg, 0), pipeline_mode=pl.Buffered(1)),
            full((3, 3 * HY_W)), full((1, 3 * HY_W)),
            pl.BlockSpec((None, 1, HY_W), lambda p, ph, j: (p, 0, 0)),
            pl.BlockSpec((tk, HY_W), filt), pl.BlockSpec((tk, HY_W), filt),
            pl.BlockSpec((None, 1, HY_W), lambda p, ph, j: (p, 0, 0)),
            full((L, 1)),
            pl.BlockSpec((tk, L), lambda p, ph, j: (j, 0)), pl.BlockSpec((tk, L), lambda p, ph, j: (j, 0)),
        ],
        out_specs=pl.BlockSpec((n_seq * L, HY_W), lambda p, ph, j: (0, 0)),
        scratch_shapes=[vm(F32), vm(F32), vm(F32), vm(BF16), vm(BF16), vm(BF16), pltpu.VMEM((1, width), F32)],
        compiler_params=_params(("arbitrary", "arbitrary", "arbitrary")),
        name="hyena_long_conv_%d" % L,
    )(hy, conv_w, conv_b, skip, hre, him, nyq, alt, ctab, stab)


def _fnet_kernel(fcs_ref, c_ref, s_ref, o_ref, *, scale):
    y = _dot(c_ref[...], fcs_ref[:, :FN_W]) - _dot(s_ref[...], fcs_ref[:, FN_W:])
    o_ref[...] = (y * scale).astype(BF16)


def _fnet_call(fcs, ctab, stab, *, n_seq, L, tok_off):
    tk = min(L, 512)
    nk = L // tk
    seq_blk0 = tok_off // L
    return pl.pallas_call(
        functools.partial(_fnet_kernel, scale=1.0 / math.sqrt(L * FN_GROUP_W)),
        out_shape=jax.ShapeDtypeStruct((n_seq * L, FN_W), BF16),
        grid=(n_seq, nk),
        in_specs=[
            pl.BlockSpec((L, 2 * FN_W), lambda b, j: (seq_blk0 + b, 0)),
            pl.BlockSpec((tk, L), lambda b, j: (j, 0)), pl.BlockSpec((tk, L), lambda b, j: (j, 0)),
        ],
        out_specs=pl.BlockSpec((tk, FN_W), lambda b, j: (b * nk + j, 0)),
        compiler_params=_params(("parallel", "parallel")),
        name="fnet_position_dft_%d" % L,
    )(fcs, ctab, stab)


def _mixout_kernel(h_ref, mod_ref, ng_ref, wg_ref, z_ref, a_ref, f_ref, wa_ref, wb_ref, wc_ref, wo_ref, o_ref):
    i = pl.program_id(0)
    m = mod_ref[_group_of(i * TM_MIX)]
    h = h_ref[...]
    n = (_rms(h, ng_ref[...]) * (1.0 + m[4:5]) + m[3:4]).astype(BF16)
    acc = _sigmoid(_dot_nt(n, wg_ref[0:D_MODEL, :])) * _dot(z_ref[...], wa_ref[...])
    acc = acc + _sigmoid(_dot_nt(n, wg_ref[D_MODEL:2 * D_MODEL, :])) * _dot(a_ref[...], wb_ref[...])
    acc = acc + _sigmoid(_dot_nt(n, wg_ref[2 * D_MODEL:, :])) * _dot(f_ref[...], wc_ref[...])
    y = _dot(acc.astype(BF16), wo_ref[...])
    o_ref[...] = h + m[5:6] * y


def _mixout_call(h, mods, ng, w_gate, l, z, a, f, wa, wb, wc, wo):
    tm = TM_MIX

    def full(shape):
        return pl.BlockSpec(shape, lambda i: (0,) * len(shape))

    def tok(width):
        return pl.BlockSpec((tm, width), lambda i: (i, 0))

    return pl.pallas_call(
        _mixout_kernel,
        out_shape=jax.ShapeDtypeStruct((T_ALL, D_MODEL), F32),
        grid=(T_ALL // tm,),
        in_specs=[
            tok(D_MODEL), full((N_GROUPS, 9, D_MODEL)), full((1, D_MODEL)),
            pl.BlockSpec((None, N_BRANCH * D_MODEL, D_MODEL), lambda i: (l, 0, 0)),
            tok(HY_W), tok(N_HEADS * V_DIM), tok(FN_W),
            full((HY_W, D_MODEL)), full((N_HEADS * V_DIM, D_MODEL)), full((FN_W, D_MODEL)),
            full((D_MODEL, D_MODEL)),
        ],
        out_specs=tok(D_MODEL),
        compiler_params=_params(("parallel",)),
        name="gated_merge_out_proj",
    )(h, mods, ng, w_gate, z, a, f, wa, wb, wc, wo)


def _wprep_kernel(w_ref, mix_ref, gate_ref):
    r = pl.program_id(1)
    n_gate = N_BRANCH * D_MODEL
    g0 = IN_GATE - PREP_ROWS
    last = n_gate - (3 * PREP_ROWS - IN_GATE)

    @pl.when(r == 0)
    def _():
        mix_ref[:MIX_FN, :] = w_ref[...].astype(BF16)

    @pl.when(r == 1)
    def _():
        k_r = w_ref[0:ROPE_DIM, :]
        swapped = jnp.concatenate([-k_r[8:16], k_r[0:8], -k_r[24:32], k_r[16:24]], axis=0)
        pad = jnp.zeros((128 - ROPE_DIM, D_MODEL), BF16)
        mix_ref[MIX_FN:MIX_KR, :] = w_ref[ROPE_DIM:g0, :].astype(BF16)
        mix_ref[MIX_KR:MIX_KR + ROPE_DIM, :] = k_r.astype(BF16)
        mix_ref[MIX_KR + ROPE_DIM:MIX_KRS, :] = pad
        mix_ref[MIX_KRS:MIX_KRS + ROPE_DIM, :] = swapped.astype(BF16)
        mix_ref[MIX_KRS + ROPE_DIM:, :] = pad
        gate_ref[:PREP_ROWS - g0, :] = w_ref[g0:, :].astype(BF16)

    @pl.when(r == 2)
    def _():
        gate_ref[PREP_ROWS - g0:2 * PREP_ROWS - g0, :] = w_ref[...].astype(BF16)

    @pl.when(r == 3)
    def _():
        gate_ref[2 * PREP_ROWS - g0:, :] = w_ref[:last, :].astype(BF16)


def _wprep_call(w_in_t):
    assert IN_KR == PREP_ROWS and 4 * PREP_ROWS >= IN_COLS
    return pl.pallas_call(
        _wprep_kernel,
        out_shape=(jax.ShapeDtypeStruct((DEPTH, MIX_W, D_MODEL), BF16),
                   jax.ShapeDtypeStruct((DEPTH, N_BRANCH * D_MODEL, D_MODEL), BF16)),
        grid=(DEPTH, 4),
        in_specs=[pl.BlockSpec((None, PREP_ROWS, D_MODEL), lambda l, r: (l, r, 0))],
        out_specs=(pl.BlockSpec((None, MIX_W, D_MODEL), lambda l, r: (l, 0, 0)),
                   pl.BlockSpec((None, N_BRANCH * D_MODEL, D_MODEL), lambda l, r: (l, 0, 0))),
        compiler_params=_params(("arbitrary", "arbitrary")),
        name="input_proj_weight_relayout",
    )(w_in_t)


def _layer_weights(l, w_qb, w_kvb):
    sign = jnp.asarray(_ROPE_SIGN)
    wq3 = w_qb[l].reshape(Q_LORA, N_HEADS, NOPE_DIM + ROPE_DIM)
    slot_pad = HEAD_SLOT - NOPE_DIM - ROPE_DIM
    wq = jnp.pad(wq3, ((0, 0), (0, 0), (0, slot_pad))).reshape(Q_LORA, QK_W).astype(BF16)
    wq_sw = wq3[:, :, NOPE_DIM:][:, :, _ROPE_PERM] * sign
    wqs = jnp.pad(wq_sw, ((0, 0), (0, 0), (NOPE_DIM, slot_pad))).reshape(Q_LORA, QK_W).astype(BF16)

    wkv3 = w_kvb[l].reshape(KV_LORA, N_HEADS, NOPE_DIM + V_DIM)
    wk = jnp.pad(wkv3[:, :, :NOPE_DIM], ((0, 0), (0, 0), (0, HEAD_SLOT - NOPE_DIM))).reshape(KV_LORA, QK_W).astype(BF16)
    wv = wkv3[:, :, NOPE_DIM:].reshape(KV_LORA, N_HEADS * V_DIM).astype(BF16)
    return wq, wqs, wk, wv


def kernel(x_prompt, x_sample, cache_ckv, cache_krope, c, c_ctx, w_ada, b_ada, norm_g, w_ffn_up, w_ffn_down,
           w_in, hy_conv_w, hy_conv_b, hy_filt_w1, hy_filt_b1, hy_filt_w2, hy_filt_b2, hy_filt_w3, hy_skip,
           w_hy_out, q_norm_g, w_qb, kv_norm_g, w_kvb, w_mla_o, w_fnet, w_out, final_g):
    tabs = _tables()
    cosq, sinq, cosk, sink = (jnp.asarray(t) for t in tabs["rope"])
    fch = jnp.asarray(tabs["fnch"]).astype(BF16)
    ek = jnp.asarray(tabs["ropeexp"]).astype(BF16)
    dft = {}
    for L in (SEQ, DEC_SEQ):
        dft[("hy", L)] = tuple(jnp.asarray(t).astype(BF16) for t in tabs[("hy", L)])
        dft[("fn", L)] = tuple(jnp.asarray(t).astype(BF16) for t in tabs[("fn", L)])

    cvec = jnp.concatenate([c_ctx[None, :], c, jnp.zeros((8 - N_GROUPS, D_MODEL), F32)], axis=0).T
    ada = _ada_call(cvec, w_ada, b_ada)
    mods_all = ada[:, :N_GROUPS].reshape(DEPTH, N_GROUPS, 9, D_MODEL)

    w_mix, w_gate = _wprep_call(jnp.swapaxes(w_in, 1, 2))
    w_hy_out, w_mla_o, w_fnet, w_out = (w.astype(BF16) for w in (w_hy_out, w_mla_o, w_fnet, w_out))
    hs = (x_prompt.reshape(T_CTX, D_MODEL), x_sample.reshape(T_LAT, D_MODEL))
    segs = ((BATCH, SEQ, 0), (DEC_BATCH, DEC_SEQ, T_CTX))
    ckv_out = []
    kr_out = []
    for l in range(DEPTH):
        mods = mods_all[l]
        ng = norm_g[l]
        wq, wqs, wk, wv = _layer_weights(l, w_qb, w_kvb)

        h = _ffn_call(hs, mods, ng[0:1], w_ffn_up, w_ffn_down, l, 0, 0)

        hy, q, k, v, ckv, k_r, fcs = _mixin_call(
            h, mods, ng[1:2], w_mix, l, q_norm_g[l][None, :], kv_norm_g[l][None, :], wq, wqs, wk, wv, ek, fch,
            cosq, sinq, cosk, sink)
        ckv_out.append(ckv[:T_CTX].reshape(BATCH, SEQ, KV_LORA))
        kr_out.append(k_r[:T_CTX].reshape(BATCH, SEQ, ROPE_DIM))

        kc, vc = _cachekv_call(cache_ckv, cache_krope, wk, wv, ek, l)

        w1 = jnp.pad(hy_filt_w1[l], ((0, POS_PAD - POS_EMB), (0, 0)))
        skip = hy_skip[l].reshape(HY_ORDER, 1, HY_W)
        z_parts, a_parts, f_parts = [], [], []
        for n_seq, L, off in segs:
            feats, decay, alt = (jnp.asarray(t) for t in tabs[("filt", L)])
            c_hy, s_hy = dft[("hy", L)]
            hre, him, nyq = _hyfilt_call(L, feats, w1, hy_filt_b1[l][None, :], hy_filt_w2[l],
                                         hy_filt_b2[l][None, :], hy_filt_w3[l], decay, alt, c_hy, s_hy)
            z_parts.append(_hyconv_call(hy, hy_conv_w[l], hy_conv_b[l][None, :], skip, hre, him,
                                        nyq.reshape(HY_ORDER, 1, HY_W), alt, c_hy, s_hy,
                                        n_seq=n_seq, L=L, tok_off=off))
            cache = (kc, vc) if off else (None, None)
            a_parts.append(_attn_call(q, k, v, *cache, n_seq=n_seq, seq_len=L, tok_off=off, tq=min(L, 512)))
            c_fn, s_fn = dft[("fn", L)]
            f_parts.append(_fnet_call(fcs, c_fn, s_fn, n_seq=n_seq, L=L, tok_off=off))
        z = jnp.concatenate(z_parts, axis=0)
        a = jnp.concatenate(a_parts, axis=0)
        f = jnp.concatenate(f_parts, axis=0)

        h = _mixout_call(h, mods, ng[1:2], w_gate, l, z, a, f, w_hy_out[l], w_mla_o[l], w_fnet[l], w_out[l])

        last = l == DEPTH - 1
        out = _ffn_call((h,), mods, ng[2:3], w_ffn_up, w_ffn_down, l, 1, 6, final_g[None, :] if last else None)
        hs = out if last else (out,)

    y_prompt = hs[0].reshape(BATCH, SEQ, D_MODEL)
    y_sample = hs[1].reshape(DEC_BATCH, DEC_SEQ, D_MODEL)
    return y_prompt, y_sample, jnp.stack(ckv_out, axis=1), jnp.stack(kr_out, axis=1)
```

```python
import functools
import math

import numpy as np
import jax
import jax.numpy as jnp
from jax import lax
from jax.experimental import pallas as pl
from jax.experimental.pallas import tpu as pltpu

F32 = jnp.float32
BF16 = jnp.bfloat16
HIGHEST = lax.Precision.HIGHEST

D_MODEL = 1024
BATCH = 16
SEQ = 256
DEPTH = 2
DEC_BATCH = 2
DEC_SEQ = 2048
PAST_LEN = 512
GRID_W = 64
HY_W = 256
HY_ORDER = 2
N_BANDS = 8
POS_EMB = 1 + 2 * N_BANDS
FILT_HID = 64
N_HEADS = 8
Q_LORA = 256
KV_LORA = 128
NOPE_DIM = 64
ROPE_DIM = 32
V_DIM = 64
ROPE_BASE = 10000.0
FN_GROUPS = 4
FN_GROUP_W = 64
FN_W = FN_GROUPS * FN_GROUP_W
N_BRANCH = 3
D_FF = 2816
EPS = 1e-6

T_CTX = BATCH * SEQ
T_LAT = DEC_BATCH * DEC_SEQ
T_ALL = T_CTX + T_LAT
N_GROUPS = 1 + DEC_BATCH
HEAD_SLOT = 128
QK_W = N_HEADS * HEAD_SLOT
PAIR_W = 2 * V_DIM
V_EXT_W = N_HEADS * PAIR_W
POS_PAD = 128

VMEM_LIMIT = 56 * 1024 * 1024

TM_FFN = 512
TF_FFN = 256
FFN_LOAD = D_FF // TF_FFN
TM_MIX = 512
PREP_ROWS = 1152

IN_KR = 3 * HY_W + Q_LORA + KV_LORA
IN_FN = IN_KR + ROPE_DIM
IN_GATE = IN_FN + FN_W
IN_COLS = IN_GATE + N_BRANCH * D_MODEL

MIX_HY = 0
MIX_QA = 3 * HY_W
MIX_KVA = MIX_QA + Q_LORA
MIX_FN = MIX_KVA + KV_LORA
MIX_KR = MIX_FN + FN_W
MIX_KRS = MIX_KR + 128
MIX_W = MIX_KRS + 128

_ROPE_PERM = np.array(list(range(8, 16)) + list(range(0, 8)) + list(range(24, 32)) + list(range(16, 24)))
_ROPE_SIGN = np.array([-1.0] * 8 + [1.0] * 8 + [-1.0] * 8 + [1.0] * 8, np.float32)


def _dft_tables(L, half):
    k = np.arange(L, dtype=np.int64)
    period = 2 * L if half else L
    m = (k[:, None] * k[None, :]) % period
    ang = 2.0 * np.pi * m.astype(np.float64) / period
    return np.cos(ang).astype(np.float32), np.sin(ang).astype(np.float32)


def _filter_tables(L):
    t = np.arange(L, dtype=np.float64)
    t_norm = t / (L - 1)
    w = 2.0 * np.pi * t / L
    bands = np.linspace(1e-4, N_BANDS - 1, N_BANDS)
    ang = w[:, None] * bands[None, :]
    feats = np.concatenate([t_norm[:, None], np.cos(ang), -np.sin(ang)], axis=-1)
    feats = np.pad(feats, ((0, 0), (0, POS_PAD - POS_EMB)))
    deltas = np.linspace(math.log(1e-2) / 1.5, math.log(1e-2) / 0.3, HY_W)
    decay = np.exp(-t_norm[:, None] * np.abs(deltas)[None, :])
    decay = np.concatenate([decay, decay], axis=1)
    alt = np.where(np.arange(L) % 2 == 0, 1.0, -1.0)[:, None]
    return feats.astype(np.float32), decay.astype(np.float32), alt.astype(np.float32)


def _rope_tables():
    t = np.arange(DEC_SEQ)
    row = (t // GRID_W).astype(np.float64)
    col = (t % GRID_W).astype(np.float64)
    nf = ROPE_DIM // 4
    inv = ROPE_BASE ** (-np.arange(nf, dtype=np.float64) / nf)
    ar = row[:, None] * inv[None, :]
    ac = col[:, None] * inv[None, :]
    cos32 = np.concatenate([np.cos(ar), np.cos(ar), np.cos(ac), np.cos(ac)], axis=1)
    sin32 = np.concatenate([np.sin(ar), np.sin(ar), np.sin(ac), np.sin(ac)], axis=1)
    cos32 = np.concatenate([cos32, np.ones((TM_MIX, ROPE_DIM))], axis=0)
    sin32 = np.concatenate([sin32, np.zeros((TM_MIX, ROPE_DIM))], axis=0)
    n = cos32.shape[0]
    cosq = np.ones((n, HEAD_SLOT))
    sinq = np.zeros((n, HEAD_SLOT))
    cosq[:, NOPE_DIM:NOPE_DIM + ROPE_DIM] = cos32
    sinq[:, NOPE_DIM:NOPE_DIM + ROPE_DIM] = sin32
    return (cosq.astype(np.float32), sinq.astype(np.float32), cos32.astype(np.float32), sin32.astype(np.float32))


def _fnet_channel_table():
    j = np.arange(FN_GROUP_W)
    ang = 2.0 * np.pi * ((j[:, None] * j[None, :]) % FN_GROUP_W) / FN_GROUP_W
    out = np.zeros((FN_W, 2 * FN_W))
    for g in range(FN_GROUPS):
        sl = slice(g * FN_GROUP_W, (g + 1) * FN_GROUP_W)
        out[sl, sl] = np.cos(ang)
        out[sl, FN_W + g * FN_GROUP_W:FN_W + (g + 1) * FN_GROUP_W] = np.sin(ang)
    return out.astype(np.float32)


def _rope_expand_table():
    e = np.zeros((ROPE_DIM, N_HEADS, HEAD_SLOT), np.float32)
    for j in range(ROPE_DIM):
        e[j, :, NOPE_DIM + j] = 1.0
    return e.reshape(ROPE_DIM, QK_W)


_TABLES = {}


def _tables():
    if not _TABLES:
        for L in (SEQ, DEC_SEQ):
            _TABLES[("hy", L)] = _dft_tables(L, True)
            _TABLES[("fn", L)] = _dft_tables(L, False)
            _TABLES[("filt", L)] = _filter_tables(L)
        _TABLES["rope"] = _rope_tables()
        _TABLES["fnch"] = _fnet_channel_table()
        _TABLES["ropeexp"] = _rope_expand_table()
    return _TABLES


def _rms(x, g):
    ms = jnp.mean(x * x, axis=-1, keepdims=True)
    return x * lax.rsqrt(ms + EPS) * g


def _sigmoid(x):
    return 1.0 / (1.0 + jnp.exp(-x))


def _dot(a, b):
    return jnp.dot(a, b, preferred_element_type=F32)


def _dot_hi(a, b):
    return jnp.dot(a, b, precision=HIGHEST, preferred_element_type=F32)


def _dot_nt(a, b):
    return lax.dot_general(a, b, (((1,), (1,)), ((), ())), preferred_element_type=F32)


def _store_values(v_ref, v):
    ones = jnp.ones((v.shape[0], PAIR_W), BF16)
    for pair in range(N_HEADS // 2):
        v_ref[:, 2 * pair * PAIR_W:(2 * pair + 1) * PAIR_W] = v[:, pair * PAIR_W:(pair + 1) * PAIR_W].astype(BF16)
        v_ref[:, (2 * pair + 1) * PAIR_W:(2 * pair + 2) * PAIR_W] = ones


def _group_of(tok0):
    return jnp.where(tok0 < T_CTX, 0, 1 + (tok0 - T_CTX) // DEC_SEQ)


def _params(sem):
    return pltpu.CompilerParams(dimension_semantics=sem, vmem_limit_bytes=VMEM_LIMIT)


def _ada_kernel(c_ref, w_ref, b_ref, o_ref):
    x = c_ref[...]
    s = x * _sigmoid(x)
    w = w_ref[...]
    o_ref[...] = jnp.zeros_like(o_ref)
    for g in range(N_GROUPS):
        o_ref[g:g + 1, :] = jnp.sum(w * s[:, g:g + 1], axis=0, keepdims=True) + b_ref[...]


def _ada_call(cvec, w_ada, b_ada):
    tn = 1024
    n_out = 9 * D_MODEL
    return pl.pallas_call(
        _ada_kernel,
        out_shape=jax.ShapeDtypeStruct((DEPTH, 8, n_out), F32),
        grid=(DEPTH, n_out // tn),
        in_specs=[
            pl.BlockSpec((D_MODEL, 8), lambda l, j: (0, 0)),
            pl.BlockSpec((None, D_MODEL, tn), lambda l, j: (l, 0, j)),
            pl.BlockSpec((None, 1, tn), lambda l, j: (l, 0, j)),
        ],
        out_specs=pl.BlockSpec((None, 8, tn), lambda l, j: (l, 0, j)),
        compiler_params=_params(("arbitrary", "arbitrary")),
        name="ada_modulation",
    )(cvec, w_ada, b_ada.reshape(DEPTH, 1, n_out))


def _ffn_kernel(*refs, j0, split_in, final):
    refs = list(refs)
    x_refs = [refs.pop(0) for _ in range(2 if split_in else 1)]
    mod_ref, ng_ref, wg_ref, wu_ref, wd_ref = refs[:5]
    refs = refs[5:]
    fg_ref = refs.pop(0) if final else None
    o_refs = [refs.pop(0) for _ in range(2 if final else 1)]
    wg_sc, wu_sc, wd_sc, hid_sc = refs
    s = pl.program_id(0)
    n_ctx_tiles = T_CTX // TM_FFN

    @pl.when(s < FFN_LOAD)
    def _():
        wg_sc[s] = wg_ref[...].astype(BF16)
        wu_sc[s] = wu_ref[...].astype(BF16)
        wd_sc[pl.ds(pl.multiple_of(s * TF_FFN, TF_FFN), TF_FFN), :] = wd_ref[...].astype(BF16)

    @pl.when(s >= FFN_LOAD)
    def _():
        t = s - FFN_LOAD
        if split_in:
            x = jnp.where(t < n_ctx_tiles, x_refs[0][...], x_refs[1][...])
        else:
            x = x_refs[0][...]
        m = mod_ref[_group_of(t * TM_FFN)]
        n = (_rms(x, ng_ref[...]) * (1.0 + m[j0 + 1:j0 + 2]) + m[j0:j0 + 1]).astype(BF16)
        for c in range(FFN_LOAD):
            g = _dot(n, wg_sc[c])
            u = _dot(n, wu_sc[c])
            hid_sc[:, c * TF_FFN:(c + 1) * TF_FFN] = (g * _sigmoid(g) * u).astype(BF16)
        y = x + 0.5 * m[j0 + 2:j0 + 3] * _dot(hid_sc[...], wd_sc[...])
        if final:
            y = _rms(y, fg_ref[...])

            @pl.when(t < n_ctx_tiles)
            def _():
                o_refs[0][...] = y

            @pl.when(t >= n_ctx_tiles)
            def _():
                o_refs[1][...] = y
        else:
            o_refs[0][...] = y


def _ffn_call(xs, mods, ng, w_up, w_down, l, f, j0, final_g=None):
    split_in = len(xs) == 2
    final = final_g is not None
    n_ctx_tiles = T_CTX // TM_FFN
    tile = lambda s: jnp.maximum(s - FFN_LOAD, 0)
    chunk = lambda s: jnp.minimum(s, FFN_LOAD - 1)
    ctx_blk = lambda s: (jnp.minimum(tile(s), n_ctx_tiles - 1), 0)
    lat_blk = lambda s: (jnp.maximum(tile(s) - n_ctx_tiles, 0), 0)
    row = pl.BlockSpec((1, D_MODEL), lambda s: (0, 0))
    tok = lambda index_map: pl.BlockSpec((TM_FFN, D_MODEL), index_map)
    if split_in:
        in_specs = [tok(ctx_blk), tok(lat_blk)]
    else:
        in_specs = [tok(lambda s: (tile(s), 0))]
    in_specs += [
        pl.BlockSpec((N_GROUPS, 9, D_MODEL), lambda s: (0, 0, 0)),
        row,
        pl.BlockSpec((None, None, D_MODEL, TF_FFN), lambda s: (l, f, 0, chunk(s))),
        pl.BlockSpec((None, None, D_MODEL, TF_FFN), lambda s: (l, f, 0, chunk(s) + FFN_LOAD)),
        pl.BlockSpec((None, None, TF_FFN, D_MODEL), lambda s: (l, f, chunk(s), 0)),
    ]
    args = list(xs) + [mods, ng, w_up, w_up, w_down]
    if final:
        in_specs.append(row)
        args.append(final_g)
        out_shape = (jax.ShapeDtypeStruct((T_CTX, D_MODEL), F32), jax.ShapeDtypeStruct((T_LAT, D_MODEL), F32))
        out_specs = (tok(ctx_blk), tok(lat_blk))
    else:
        out_shape = jax.ShapeDtypeStruct((T_ALL, D_MODEL), F32)
        out_specs = tok(lambda s: (tile(s), 0))
    return pl.pallas_call(
        functools.partial(_ffn_kernel, j0=j0, split_in=split_in, final=final),
        out_shape=out_shape,
        grid=(FFN_LOAD + T_ALL // TM_FFN,),
        in_specs=in_specs,
        out_specs=out_specs,
        scratch_shapes=[pltpu.VMEM((FFN_LOAD, D_MODEL, TF_FFN), BF16), pltpu.VMEM((FFN_LOAD, D_MODEL, TF_FFN), BF16),
                        pltpu.VMEM((D_FF, D_MODEL), BF16), pltpu.VMEM((TM_FFN, D_FF), BF16)],
        compiler_params=_params(("arbitrary",)),
        name="swiglu_half_step",
    )(*args)


def _mixin_kernel(h_ref, mod_ref, ng_ref, w_ref, qg_ref, kvg_ref, wq_ref, wqs_ref, wk_ref, wv_ref, ek_ref,
                  fch_ref, cq_ref, sq_ref, ck_ref, sk_ref,
                  hy_ref, q_ref, k_ref, v_ref, ckv_ref, kr_ref, fcs_ref):
    i = pl.program_id(0)
    m = mod_ref[_group_of(i * TM_MIX)]
    n = (_rms(h_ref[...], ng_ref[...]) * (1.0 + m[4:5]) + m[3:4]).astype(BF16)
    proj = _dot_nt(n, w_ref[...])
    hy_ref[...] = proj[:, MIX_HY:MIX_QA]
    q_a = proj[:, MIX_QA:MIX_KVA]
    kv_a = proj[:, MIX_KVA:MIX_FN]
    fn = proj[:, MIX_FN:MIX_KR]
    k_r = proj[:, MIX_KR:MIX_KR + ROPE_DIM]
    k_r_swapped = proj[:, MIX_KRS:MIX_KRS + ROPE_DIM]

    qn = _rms(q_a, qg_ref[...]).astype(BF16)
    q = _dot(qn, wq_ref[...])
    q_swapped = _dot(qn, wqs_ref[...])
    cos_q = cq_ref[...]
    sin_q = sq_ref[...]
    for h in range(N_HEADS):
        hs = slice(h * HEAD_SLOT, (h + 1) * HEAD_SLOT)
        q_ref[:, hs] = (q[:, hs] * cos_q + q_swapped[:, hs] * sin_q).astype(BF16)

    ckv = _rms(kv_a, kvg_ref[...])
    ckv_ref[...] = ckv
    kr_ref[...] = k_r
    ckv_b = ckv.astype(BF16)
    k_rot = (k_r * ck_ref[...] + k_r_swapped * sk_ref[...]).astype(BF16)
    k_ref[...] = (_dot(ckv_b, wk_ref[...]) + _dot(k_rot, ek_ref[...])).astype(BF16)
    _store_values(v_ref, _dot(ckv_b, wv_ref[...]))
    fcs_ref[...] = _dot(fn.astype(BF16), fch_ref[...]).astype(BF16)


def _mixin_call(h, mods, ng, w_mix, l, qg, kvg, wq, wqs, wk, wv, ek, fch, cosq, sinq, cosk, sink):
    tm = TM_MIX

    def pos_block(i):
        tok0 = i * tm
        return jnp.where(tok0 < T_CTX, DEC_SEQ // tm, ((tok0 - T_CTX) % DEC_SEQ) // tm)

    def full(shape):
        return pl.BlockSpec(shape, lambda i: (0,) * len(shape))

    def tok(width):
        return pl.BlockSpec((tm, width), lambda i: (i, 0))

    def pos(width):
        return pl.BlockSpec((tm, width), lambda i: (pos_block(i), 0))

    out_shape = (
        jax.ShapeDtypeStruct((T_ALL, 3 * HY_W), F32),
        jax.ShapeDtypeStruct((T_ALL, QK_W), BF16),
        jax.ShapeDtypeStruct((T_ALL, QK_W), BF16),
        jax.ShapeDtypeStruct((T_ALL, V_EXT_W), BF16),
        jax.ShapeDtypeStruct((T_ALL, KV_LORA), F32),
        jax.ShapeDtypeStruct((T_ALL, ROPE_DIM), F32),
        jax.ShapeDtypeStruct((T_ALL, 2 * FN_W), BF16),
    )
    return pl.pallas_call(
        _mixin_kernel,
        out_shape=out_shape,
        grid=(T_ALL // tm,),
        in_specs=[
            tok(D_MODEL), full((N_GROUPS, 9, D_MODEL)), full((1, D_MODEL)),
            pl.BlockSpec((None, MIX_W, D_MODEL), lambda i: (l, 0, 0)),
            full((1, Q_LORA)), full((1, KV_LORA)), full((Q_LORA, QK_W)), full((Q_LORA, QK_W)),
            full((KV_LORA, QK_W)), full((KV_LORA, N_HEADS * V_DIM)), full((ROPE_DIM, QK_W)),
            full((FN_W, 2 * FN_W)), pos(HEAD_SLOT), pos(HEAD_SLOT), pos(ROPE_DIM), pos(ROPE_DIM),
        ],
        out_specs=tuple(tok(s.shape[1]) for s in out_shape),
        compiler_params=_params(("parallel",)),
        name="mixer_input_proj",
    )(h, mods, ng, w_mix, qg, kvg, wq, wqs, wk, wv, ek, fch, cosq, sinq, cosk, sink)


def _cachekv_kernel(ckv_ref, kr_ref, wk_ref, wv_ref, ek_ref, k_ref, v_ref):
    ckv_b = ckv_ref[...].astype(BF16)
    k_ref[...] = (_dot(ckv_b, wk_ref[...]) + _dot(kr_ref[...].astype(BF16), ek_ref[...])).astype(BF16)
    _store_values(v_ref, _dot(ckv_b, wv_ref[...]))


def _cachekv_call(cache_ckv, cache_krope, wk, wv, ek, l):
    def full(shape):
        return pl.BlockSpec(shape, lambda b: (0,) * len(shape))

    return pl.pallas_call(
        _cachekv_kernel,
        out_shape=(jax.ShapeDtypeStruct((DEC_BATCH, PAST_LEN, QK_W), BF16),
                   jax.ShapeDtypeStruct((DEC_BATCH, PAST_LEN, V_EXT_W), BF16)),
        grid=(DEC_BATCH,),
        in_specs=[
            pl.BlockSpec((None, None, PAST_LEN, KV_LORA), lambda b: (b, l, 0, 0)),
            pl.BlockSpec((None, None, PAST_LEN, ROPE_DIM), lambda b: (b, l, 0, 0)),
            full((KV_LORA, QK_W)), full((KV_LORA, N_HEADS * V_DIM)), full((ROPE_DIM, QK_W)),
        ],
        out_specs=(pl.BlockSpec((None, PAST_LEN, QK_W), lambda b: (b, 0, 0)),
                   pl.BlockSpec((None, PAST_LEN, V_EXT_W), lambda b: (b, 0, 0))),
        compiler_params=_params(("parallel",)),
        name="cached_context_kv",
    )(cache_ckv, cache_krope, wk, wv, ek)


def _attn_kernel(q_ref, k_ref, v_ref, *rest, has_cache):
    if has_cache:
        kc_ref, vc_ref, o_ref = rest
    else:
        (o_ref,) = rest
    scale2 = math.log2(math.e) / math.sqrt(NOPE_DIM + ROPE_DIM)
    first_half = lax.broadcasted_iota(jnp.int32, (1, PAIR_W), 1) < V_DIM
    for pair in range(N_HEADS // 2):
        vs = slice(2 * pair * PAIR_W, (2 * pair + 2) * PAIR_W)
        outs = []
        for e in range(2):
            hs = slice((2 * pair + e) * HEAD_SLOT, (2 * pair + e + 1) * HEAD_SLOT)
            qh = q_ref[:, hs]
            s = _dot_nt(qh, k_ref[:, hs])
            mx = jnp.max(s, axis=-1, keepdims=True)
            if has_cache:
                sc = _dot_nt(qh, kc_ref[:, hs])
                mx = jnp.maximum(mx, jnp.max(sc, axis=-1, keepdims=True))
            p = jnp.exp2((s - mx) * scale2)
            if has_cache:
                o = _dot(p.astype(BF16), v_ref[:, vs])
                o = o + _dot(jnp.exp2((sc - mx) * scale2).astype(BF16), vc_ref[:, vs])
                outs.append(o[:, :PAIR_W] / o[:, PAIR_W:PAIR_W + 1])
            else:
                o = _dot(p.astype(BF16), v_ref[:, vs.start:vs.start + PAIR_W])
                outs.append(o / jnp.sum(p, axis=-1, keepdims=True))
        o_ref[:, pair * PAIR_W:(pair + 1) * PAIR_W] = jnp.where(first_half, outs[0], outs[1]).astype(BF16)


def _attn_call(q, k, v, kc, vc, *, n_seq, seq_len, tok_off, tq):
    has_cache = kc is not None
    nq = seq_len // tq
    seq_blk0 = tok_off // seq_len
    q_blk0 = tok_off // tq
    vw = N_HEADS * V_DIM
    in_specs = [
        pl.BlockSpec((tq, QK_W), lambda b, j: (q_blk0 + b * nq + j, 0)),
        pl.BlockSpec((seq_len, QK_W), lambda b, j: (seq_blk0 + b, 0)),
        pl.BlockSpec((seq_len, V_EXT_W), lambda b, j: (seq_blk0 + b, 0)),
    ]
    args = [q, k, v]
    if has_cache:
        in_specs += [pl.BlockSpec((None, PAST_LEN, QK_W), lambda b, j: (b, 0, 0)),
                     pl.BlockSpec((None, PAST_LEN, V_EXT_W), lambda b, j: (b, 0, 0))]
        args += [kc, vc]
    return pl.pallas_call(
        functools.partial(_attn_kernel, has_cache=has_cache),
        out_shape=jax.ShapeDtypeStruct((n_seq * seq_len, vw), BF16),
        grid=(n_seq, nq),
        in_specs=in_specs,
        out_specs=pl.BlockSpec((tq, vw), lambda b, j: (b * nq + j, 0)),
        compiler_params=_params(("parallel", "parallel")),
        name="mla_attention_cache" if has_cache else "mla_attention",
    )(*args)


def _hyfilt_kernel(feat_ref, w1_ref, b1_ref, w2_ref, b2_ref, w3_ref, dec_ref, alt_ref, c_ref, s_ref,
                   hre_ref, him_ref, nyq_ref, a_sc, b_sc, *, L):
    j = pl.program_id(0)
    half = HY_ORDER * HY_W

    @pl.when(j == 0)
    def _():
        h = jnp.sin(_dot_hi(feat_ref[...], w1_ref[...]) + b1_ref[...])
        h = jnp.sin(_dot_hi(h, w2_ref[...]) + b2_ref[...])
        h = _dot_hi(h, w3_ref[...])
        dec = dec_ref[...]
        rows = lax.broadcasted_iota(jnp.int32, (L, 1), 0)
        h_fwd = h[:, :half] * dec
        h_bwd = jnp.where(rows > 0, h[:, half:] * dec, 0.0)
        norm = (jnp.sum(jnp.abs(h_fwd), axis=0, keepdims=True)
                + jnp.sum(jnp.abs(h_bwd), axis=0, keepdims=True) + EPS)
        inv = 1.0 / norm
        a = (h_fwd + h_bwd) * inv
        b = (h_fwd - h_bwd) * inv
        a_sc[...] = a.astype(BF16)
        b_sc[...] = b.astype(BF16)
        nyq_ref[...] = jnp.sum(a * alt_ref[...], axis=0, keepdims=True)

    hre_ref[...] = _dot(c_ref[...], a_sc[...])
    him_ref[...] = -_dot(s_ref[...], b_sc[...])


def _hyfilt_call(L, feats, w1, b1, w2, b2, w3, decay, alt, ctab, stab):
    tk = min(L, 512)
    half = HY_ORDER * HY_W

    def full(shape):
        return pl.BlockSpec(shape, lambda j: (0,) * len(shape))

    return pl.pallas_call(
        functools.partial(_hyfilt_kernel, L=L),
        out_shape=(jax.ShapeDtypeStruct((L, half), F32), jax.ShapeDtypeStruct((L, half), F32),
                   jax.ShapeDtypeStruct((1, half), F32)),
        grid=(L // tk,),
        in_specs=[
            full((L, POS_PAD)), full((POS_PAD, FILT_HID)), full((1, FILT_HID)), full((FILT_HID, FILT_HID)),
            full((1, FILT_HID)), full((FILT_HID, 2 * half)), full((L, half)), full((L, 1)),
            pl.BlockSpec((tk, L), lambda j: (j, 0)), pl.BlockSpec((tk, L), lambda j: (j, 0)),
        ],
        out_specs=(pl.BlockSpec((tk, half), lambda j: (j, 0)), pl.BlockSpec((tk, half), lambda j: (j, 0)),
                   full((1, half))),
        scratch_shapes=[pltpu.VMEM((L, half), BF16), pltpu.VMEM((L, half), BF16)],
        compiler_params=_params(("arbitrary",)),
        name="hyena_filter_spectrum_%d" % L,
    )(feats, w1, b1, w2, b2, w3, decay, alt, ctab, stab)


def _hyconv_kernel(hy_ref, cw_ref, cb_ref, skip_ref, hre_ref, him_ref, nyq_ref, alt_ref, c_ref, s_ref, o_ref,
                   x1_sc, x2_sc, v_sc, vb_sc, yre_sc, yim_sc, nv_sc, *, n_seq, L, tk):
    p = pl.program_id(0)
    ph = pl.program_id(1)
    j = pl.program_id(2)
    cols = [slice(b * HY_W, (b + 1) * HY_W) for b in range(n_seq)]
    blk = pl.ds(pl.multiple_of(j * tk, tk), tk)

    @pl.when((p == 0) & (ph == 0) & (j == 0))
    def _():
        rows = lax.broadcasted_iota(jnp.int32, (L, 1), 0)
        w = cw_ref[...]
        for b in range(n_seq):
            x = hy_ref[b * L:(b + 1) * L, :]
            prev = jnp.where(rows > 0, pltpu.roll(x, 1, 0), 0.0)
            nxt = jnp.where(rows < L - 1, pltpu.roll(x, L - 1, 0), 0.0)
            u = prev * w[0:1] + x * w[1:2] + nxt * w[2:3] + cb_ref[...]
            x1_sc[:, cols[b]] = u[:, :HY_W]
            x2_sc[:, cols[b]] = u[:, HY_W:2 * HY_W]
            v_sc[:, cols[b]] = u[:, 2 * HY_W:]

    @pl.when((ph == 0) & (j == 0))
    def _():
        v = v_sc[...]
        vb_sc[...] = v.astype(BF16)
        nv_sc[...] = jnp.sum(v * alt_ref[...], axis=0, keepdims=True)

    @pl.when(ph == 0)
    def _():
        vb = vb_sc[...]
        v_re = _dot(c_ref[...], vb)
        v_im = -_dot(s_ref[...], vb)
        freq = j * tk + lax.broadcasted_iota(jnp.int32, (tk, 1), 0)
        wk = jnp.where(freq == 0, 0.5, 1.0)
        h_re = hre_ref[...] * wk
        h_im = him_ref[...] * wk
        for b in range(n_seq):
            yre_sc[blk, cols[b]] = (v_re[:, cols[b]] * h_re - v_im[:, cols[b]] * h_im).astype(BF16)
            yim_sc[blk, cols[b]] = (v_re[:, cols[b]] * h_im + v_im[:, cols[b]] * h_re).astype(BF16)

    @pl.when(ph == 1)
    def _():
        acc = _dot(c_ref[...], yre_sc[...]) - _dot(s_ref[...], yim_sc[...])
        alt = alt_ref[blk, :]
        for b in range(n_seq):
            v = v_sc[blk, cols[b]]
            nyq = nv_sc[:, cols[b]] * nyq_ref[...]
            y = acc[:, cols[b]] * (1.0 / L) + (0.5 / L) * alt * nyq + v * skip_ref[...]

            @pl.when(p == 0)
            def _():
                v_sc[blk, cols[b]] = x1_sc[blk, cols[b]] * y

            @pl.when(p == 1)
            def _():
                o_ref[pl.ds(pl.multiple_of(b * L + j * tk, tk), tk), :] = (x2_sc[blk, cols[b]] * y).astype(BF16)


def _hyconv_call(hy, conv_w, conv_b, skip, hre, him, nyq, alt, ctab, stab, *, n_seq, L, tok_off):
    tk = min(L, 512)
    nk = L // tk
    seg = tok_off // (n_seq * L)
    width = n_seq * HY_W

    def full(shape):
        return pl.BlockSpec(shape, lambda p, ph, j: (0,) * len(shape))

    def filt(p, ph, j):
        return (jnp.where(ph == 0, j, nk - 1), p)

    vm = lambda dtype: pltpu.VMEM((L, width), dtype)
    return pl.pallas_call(
        functools.partial(_hyconv_kernel, n_seq=n_seq, L=L, tk=tk),
        out_shape=jax.ShapeDtypeStruct((n_seq * L, HY_W), BF16),
        grid=(HY_ORDER, 2, nk),
        in_specs=[
            pl.BlockSpec((n_seq * L, 3 * HY_W), lambda p, ph, j: (seg, 0), pipeline_mode=pl.Buffered(1)),
            full((3, 3 * HY_W)), full((1, 3 * HY_W)),
            pl.BlockSpec((None, 1, HY_W), lambda p, ph, j: (p, 0, 0)),
            pl.BlockSpec((tk, HY_W), filt), pl.BlockSpec((tk, HY_W), filt),
            pl.BlockSpec((None, 1, HY_W), lambda p, ph, j: (p, 0, 0)),
            full((L, 1)),
            pl.BlockSpec((tk, L), lambda p, ph, j: (j, 0)), pl.BlockSpec((tk, L), lambda p, ph, j: (j, 0)),
        ],
        out_specs=pl.BlockSpec((n_seq * L, HY_W), lambda p, ph, j: (0, 0)),
        scratch_shapes=[vm(F32), vm(F32), vm(F32), vm(BF16), vm(BF16), vm(BF16), pltpu.VMEM((1, width), F32)],
        compiler_params=_params(("arbitrary", "arbitrary", "arbitrary")),
        name="hyena_long_conv_%d" % L,
    )(hy, conv_w, conv_b, skip, hre, him, nyq, alt, ctab, stab)


def _fnet_kernel(fcs_ref, c_ref, s_ref, o_ref, fc_sc, fs_sc, *, n_seq, L, tk, scale):
    j = pl.program_id(0)
    cols = [slice(b * FN_W, (b + 1) * FN_W) for b in range(n_seq)]

    @pl.when(j == 0)
    def _():
        for b in range(n_seq):
            fc_sc[:, cols[b]] = fcs_ref[b * L:(b + 1) * L, :FN_W]
            fs_sc[:, cols[b]] = fcs_ref[b * L:(b + 1) * L, FN_W:]

    y = (_dot(c_ref[...], fc_sc[...]) - _dot(s_ref[...], fs_sc[...])) * scale
    for b in range(n_seq):
        o_ref[pl.ds(pl.multiple_of(b * L + j * tk, tk), tk), :] = y[:, cols[b]].astype(BF16)


def _fnet_call(fcs, ctab, stab, *, n_seq, L, tok_off):
    tk = min(L, 512)
    seg = tok_off // (n_seq * L)
    vm = lambda: pltpu.VMEM((L, n_seq * FN_W), BF16)
    return pl.pallas_call(
        functools.partial(_fnet_kernel, n_seq=n_seq, L=L, tk=tk, scale=1.0 / math.sqrt(L * FN_GROUP_W)),
        out_shape=jax.ShapeDtypeStruct((n_seq * L, FN_W), BF16),
        grid=(L // tk,),
        in_specs=[
            pl.BlockSpec((n_seq * L, 2 * FN_W), lambda j: (seg, 0)),
            pl.BlockSpec((tk, L), lambda j: (j, 0)), pl.BlockSpec((tk, L), lambda j: (j, 0)),
        ],
        out_specs=pl.BlockSpec((n_seq * L, FN_W), lambda j: (0, 0)),
        scratch_shapes=[vm(), vm()],
        compiler_params=_params(("arbitrary",)),
        name="fnet_position_dft_%d" % L,
    )(fcs, ctab, stab)


def _mixout_kernel(h_ref, mod_ref, ng_ref, wg_ref, zc_ref, zl_ref, ac_ref, al_ref, fc_ref, fl_ref,
                   wa_ref, wb_ref, wc_ref, wo_ref, o_ref):
    i = pl.program_id(0)
    is_ctx = i < T_CTX // TM_MIX
    m = mod_ref[_group_of(i * TM_MIX)]
    h = h_ref[...]
    n = (_rms(h, ng_ref[...]) * (1.0 + m[4:5]) + m[3:4]).astype(BF16)
    z = jnp.where(is_ctx, zc_ref[...], zl_ref[...])
    a = jnp.where(is_ctx, ac_ref[...], al_ref[...])
    f = jnp.where(is_ctx, fc_ref[...], fl_ref[...])
    acc = _sigmoid(_dot_nt(n, wg_ref[0:D_MODEL, :])) * _dot(z, wa_ref[...])
    acc = acc + _sigmoid(_dot_nt(n, wg_ref[D_MODEL:2 * D_MODEL, :])) * _dot(a, wb_ref[...])
    acc = acc + _sigmoid(_dot_nt(n, wg_ref[2 * D_MODEL:, :])) * _dot(f, wc_ref[...])
    y = _dot(acc.astype(BF16), wo_ref[...])
    o_ref[...] = h + m[5:6] * y


def _mixout_call(h, mods, ng, w_gate, l, z, a, f, wa, wb, wc, wo):
    tm = TM_MIX
    n_ctx = T_CTX // tm

    def full(shape):
        return pl.BlockSpec(shape, lambda i: (0,) * len(shape))

    def tok(width):
        return pl.BlockSpec((tm, width), lambda i: (i, 0))

    def pair(width):
        return [pl.BlockSpec((tm, width), lambda i: (jnp.minimum(i, n_ctx - 1), 0)),
                pl.BlockSpec((tm, width), lambda i: (jnp.maximum(i - n_ctx, 0), 0))]

    return pl.pallas_call(
        _mixout_kernel,
        out_shape=jax.ShapeDtypeStruct((T_ALL, D_MODEL), F32),
        grid=(T_ALL // tm,),
        in_specs=[
            tok(D_MODEL), full((N_GROUPS, 9, D_MODEL)), full((1, D_MODEL)),
            pl.BlockSpec((None, N_BRANCH * D_MODEL, D_MODEL), lambda i: (l, 0, 0)),
            *pair(HY_W), *pair(N_HEADS * V_DIM), *pair(FN_W),
            full((HY_W, D_MODEL)), full((N_HEADS * V_DIM, D_MODEL)), full((FN_W, D_MODEL)),
            full((D_MODEL, D_MODEL)),
        ],
        out_specs=tok(D_MODEL),
        compiler_params=_params(("parallel",)),
        name="gated_merge_out_proj",
    )(h, mods, ng, w_gate, *z, *a, *f, wa, wb, wc, wo)


def _wprep_kernel(w_ref, mix_ref, gate_ref):
    r = pl.program_id(1)
    n_gate = N_BRANCH * D_MODEL
    g0 = IN_GATE - PREP_ROWS
    last = n_gate - (3 * PREP_ROWS - IN_GATE)

    @pl.when(r == 0)
    def _():
        mix_ref[:MIX_FN, :] = w_ref[...].astype(BF16)

    @pl.when(r == 1)
    def _():
        k_r = w_ref[0:ROPE_DIM, :]
        swapped = jnp.concatenate([-k_r[8:16], k_r[0:8], -k_r[24:32], k_r[16:24]], axis=0)
        pad = jnp.zeros((128 - ROPE_DIM, D_MODEL), BF16)
        mix_ref[MIX_FN:MIX_KR, :] = w_ref[ROPE_DIM:g0, :].astype(BF16)
        mix_ref[MIX_KR:MIX_KR + ROPE_DIM, :] = k_r.astype(BF16)
        mix_ref[MIX_KR + ROPE_DIM:MIX_KRS, :] = pad
        mix_ref[MIX_KRS:MIX_KRS + ROPE_DIM, :] = swapped.astype(BF16)
        mix_ref[MIX_KRS + ROPE_DIM:, :] = pad
        gate_ref[:PREP_ROWS - g0, :] = w_ref[g0:, :].astype(BF16)

    @pl.when(r == 2)
    def _():
        gate_ref[PREP_ROWS - g0:2 * PREP_ROWS - g0, :] = w_ref[...].astype(BF16)

    @pl.when(r == 3)
    def _():
        gate_ref[2 * PREP_ROWS - g0:, :] = w_ref[:last, :].astype(BF16)


def _wprep_call(w_in_t):
    assert IN_KR == PREP_ROWS and 4 * PREP_ROWS >= IN_COLS
    return pl.pallas_call(
        _wprep_kernel,
        out_shape=(jax.ShapeDtypeStruct((DEPTH, MIX_W, D_MODEL), BF16),
                   jax.ShapeDtypeStruct((DEPTH, N_BRANCH * D_MODEL, D_MODEL), BF16)),
        grid=(DEPTH, 4),
        in_specs=[pl.BlockSpec((None, PREP_ROWS, D_MODEL), lambda l, r: (l, r, 0))],
        out_specs=(pl.BlockSpec((None, MIX_W, D_MODEL), lambda l, r: (l, 0, 0)),
                   pl.BlockSpec((None, N_BRANCH * D_MODEL, D_MODEL), lambda l, r: (l, 0, 0))),
        compiler_params=_params(("arbitrary", "arbitrary")),
        name="input_proj_weight_relayout",
    )(w_in_t)


def _layer_weights(l, w_qb, w_kvb):
    sign = jnp.asarray(_ROPE_SIGN)
    wq3 = w_qb[l].reshape(Q_LORA, N_HEADS, NOPE_DIM + ROPE_DIM)
    slot_pad = HEAD_SLOT - NOPE_DIM - ROPE_DIM
    wq = jnp.pad(wq3, ((0, 0), (0, 0), (0, slot_pad))).reshape(Q_LORA, QK_W).astype(BF16)
    wq_sw = wq3[:, :, NOPE_DIM:][:, :, _ROPE_PERM] * sign
    wqs = jnp.pad(wq_sw, ((0, 0), (0, 0), (NOPE_DIM, slot_pad))).reshape(Q_LORA, QK_W).astype(BF16)

    wkv3 = w_kvb[l].reshape(KV_LORA, N_HEADS, NOPE_DIM + V_DIM)
    wk = jnp.pad(wkv3[:, :, :NOPE_DIM], ((0, 0), (0, 0), (0, HEAD_SLOT - NOPE_DIM))).reshape(KV_LORA, QK_W).astype(BF16)
    wv = wkv3[:, :, NOPE_DIM:].reshape(KV_LORA, N_HEADS * V_DIM).astype(BF16)
    return wq, wqs, wk, wv


def kernel(x_prompt, x_sample, cache_ckv, cache_krope, c, c_ctx, w_ada, b_ada, norm_g, w_ffn_up, w_ffn_down,
           w_in, hy_conv_w, hy_conv_b, hy_filt_w1, hy_filt_b1, hy_filt_w2, hy_filt_b2, hy_filt_w3, hy_skip,
           w_hy_out, q_norm_g, w_qb, kv_norm_g, w_kvb, w_mla_o, w_fnet, w_out, final_g):
    tabs = _tables()
    cosq, sinq, cosk, sink = (jnp.asarray(t) for t in tabs["rope"])
    fch = jnp.asarray(tabs["fnch"]).astype(BF16)
    ek = jnp.asarray(tabs["ropeexp"]).astype(BF16)
    dft = {}
    for L in (SEQ, DEC_SEQ):
        dft[("hy", L)] = tuple(jnp.asarray(t).astype(BF16) for t in tabs[("hy", L)])
        dft[("fn", L)] = tuple(jnp.asarray(t).astype(BF16) for t in tabs[("fn", L)])

    cvec = jnp.concatenate([c_ctx[None, :], c, jnp.zeros((8 - N_GROUPS, D_MODEL), F32)], axis=0).T
    ada = _ada_call(cvec, w_ada, b_ada)
    mods_all = ada[:, :N_GROUPS].reshape(DEPTH, N_GROUPS, 9, D_MODEL)

    w_mix, w_gate = _wprep_call(jnp.swapaxes(w_in, 1, 2))
    w_hy_out, w_mla_o, w_fnet, w_out = (w.astype(BF16) for w in (w_hy_out, w_mla_o, w_fnet, w_out))
    hs = (x_prompt.reshape(T_CTX, D_MODEL), x_sample.reshape(T_LAT, D_MODEL))
    segs = ((BATCH, SEQ, 0), (DEC_BATCH, DEC_SEQ, T_CTX))
    ckv_out = []
    kr_out = []
    for l in range(DEPTH):
        mods = mods_all[l]
        ng = norm_g[l]
        wq, wqs, wk, wv = _layer_weights(l, w_qb, w_kvb)

        h = _ffn_call(hs, mods, ng[0:1], w_ffn_up, w_ffn_down, l, 0, 0)

        hy, q, k, v, ckv, k_r, fcs = _mixin_call(
            h, mods, ng[1:2], w_mix, l, q_norm_g[l][None, :], kv_norm_g[l][None, :], wq, wqs, wk, wv, ek, fch,
            cosq, sinq, cosk, sink)
        ckv_out.append(ckv[:T_CTX].reshape(BATCH, SEQ, KV_LORA))
        kr_out.append(k_r[:T_CTX].reshape(BATCH, SEQ, ROPE_DIM))

        kc, vc = _cachekv_call(cache_ckv, cache_krope, wk, wv, ek, l)

        w1 = jnp.pad(hy_filt_w1[l], ((0, POS_PAD - POS_EMB), (0, 0)))
        skip = hy_skip[l].reshape(HY_ORDER, 1, HY_W)
        z_parts, a_parts, f_parts = [], [], []
        for n_seq, L, off in segs:
            feats, decay, alt = (jnp.asarray(t) for t in tabs[("filt", L)])
            c_hy, s_hy = dft[("hy", L)]
            hre, him, nyq = _hyfilt_call(L, feats, w1, hy_filt_b1[l][None, :], hy_filt_w2[l],
                                         hy_filt_b2[l][None, :], hy_filt_w3[l], decay, alt, c_hy, s_hy)
            z_parts.append(_hyconv_call(hy, hy_conv_w[l], hy_conv_b[l][None, :], skip, hre, him,
                                        nyq.reshape(HY_ORDER, 1, HY_W), alt, c_hy, s_hy,
                                        n_seq=n_seq, L=L, tok_off=off))
            cache = (kc, vc) if off else (None, None)
            a_parts.append(_attn_call(q, k, v, *cache, n_seq=n_seq, seq_len=L, tok_off=off, tq=min(L, 512)))
            c_fn, s_fn = dft[("fn", L)]
            f_parts.append(_fnet_call(fcs, c_fn, s_fn, n_seq=n_seq, L=L, tok_off=off))
        h = _mixout_call(h, mods, ng[1:2], w_gate, l, z_parts, a_parts, f_parts,
                         w_hy_out[l], w_mla_o[l], w_fnet[l], w_out[l])

        last = l == DEPTH - 1
        out = _ffn_call((h,), mods, ng[2:3], w_ffn_up, w_ffn_down, l, 1, 6, final_g[None, :] if last else None)
        hs = out if last else (out,)

    y_prompt = hs[0].reshape(BATCH, SEQ, D_MODEL)
    y_sample = hs[1].reshape(DEC_BATCH, DEC_SEQ, D_MODEL)
    return y_prompt, y_sample, jnp.stack(ckv_out, axis=1), jnp.stack(kr_out, axis=1)
```

```python
import functools
import math

import numpy as np
import jax
import jax.numpy as jnp
from jax import lax
from jax.experimental import pallas as pl
from jax.experimental.pallas import tpu as pltpu

F32 = jnp.float32
BF16 = jnp.bfloat16
HIGHEST = lax.Precision.HIGHEST

D_MODEL = 1024
BATCH = 16
SEQ = 256
DEPTH = 2
DEC_BATCH = 2
DEC_SEQ = 2048
PAST_LEN = 512
GRID_W = 64
HY_W = 256
HY_ORDER = 2
N_BANDS = 8
POS_EMB = 1 + 2 * N_BANDS
FILT_HID = 64
N_HEADS = 8
Q_LORA = 256
KV_LORA = 128
NOPE_DIM = 64
ROPE_DIM = 32
V_DIM = 64
ROPE_BASE = 10000.0
FN_GROUPS = 4
FN_GROUP_W = 64
FN_W = FN_GROUPS * FN_GROUP_W
N_BRANCH = 3
D_FF = 2816
EPS = 1e-6

T_CTX = BATCH * SEQ
T_LAT = DEC_BATCH * DEC_SEQ
T_ALL = T_CTX + T_LAT
N_GROUPS = 1 + DEC_BATCH
HEAD_SLOT = 128
QK_W = N_HEADS * HEAD_SLOT
PAIR_W = 2 * V_DIM
V_EXT_W = N_HEADS * PAIR_W
POS_PAD = 128

VMEM_LIMIT = 56 * 1024 * 1024

TM_FFN = 512
TF_FFN = 256
FFN_LOAD = 8
FFN_UP_ROWS = D_MODEL // FFN_LOAD
FFN_DN_ROWS = D_FF // FFN_LOAD
FFN_SUB = 2
TM_MIX = 512
PREP_ROWS = 1152

IN_KR = 3 * HY_W + Q_LORA + KV_LORA
IN_FN = IN_KR + ROPE_DIM
IN_GATE = IN_FN + FN_W
IN_COLS = IN_GATE + N_BRANCH * D_MODEL

MIX_HY = 0
MIX_QA = 3 * HY_W
MIX_KVA = MIX_QA + Q_LORA
MIX_FN = MIX_KVA + KV_LORA
MIX_KR = MIX_FN + FN_W
MIX_W = MIX_KR + 128
ROPE_HALF = ROPE_DIM // 4


def _dft_tables(L, half):
    k = np.arange(L, dtype=np.int64)
    period = 2 * L if half else L
    m = (k[:, None] * k[None, :]) % period
    ang = 2.0 * np.pi * m.astype(np.float64) / period
    return np.cos(ang).astype(np.float32), np.sin(ang).astype(np.float32)


def _filter_tables(L):
    t = np.arange(L, dtype=np.float64)
    t_norm = t / (L - 1)
    w = 2.0 * np.pi * t / L
    bands = np.linspace(1e-4, N_BANDS - 1, N_BANDS)
    ang = w[:, None] * bands[None, :]
    feats = np.concatenate([t_norm[:, None], np.cos(ang), -np.sin(ang)], axis=-1)
    feats = np.pad(feats, ((0, 0), (0, POS_PAD - POS_EMB)))
    deltas = np.linspace(math.log(1e-2) / 1.5, math.log(1e-2) / 0.3, HY_W)
    decay = np.exp(-t_norm[:, None] * np.abs(deltas)[None, :])
    decay = np.concatenate([decay, decay], axis=1)
    alt = np.where(np.arange(L) % 2 == 0, 1.0, -1.0)[:, None]
    return feats.astype(np.float32), decay.astype(np.float32), alt.astype(np.float32)


def _rope_tables():
    t = np.arange(DEC_SEQ)
    row = (t // GRID_W).astype(np.float64)
    col = (t % GRID_W).astype(np.float64)
    nf = ROPE_DIM // 4
    inv = ROPE_BASE ** (-np.arange(nf, dtype=np.float64) / nf)
    ar = row[:, None] * inv[None, :]
    ac = col[:, None] * inv[None, :]
    cos32 = np.concatenate([np.cos(ar), np.cos(ar), np.cos(ac), np.cos(ac)], axis=1)
    sin32 = np.concatenate([np.sin(ar), np.sin(ar), np.sin(ac), np.sin(ac)], axis=1)
    cos32 = np.concatenate([cos32, np.ones((TM_MIX, ROPE_DIM))], axis=0)
    sin32 = np.concatenate([sin32, np.zeros((TM_MIX, ROPE_DIM))], axis=0)
    n = cos32.shape[0]
    cosq = np.ones((n, HEAD_SLOT))
    sinq = np.zeros((n, HEAD_SLOT))
    cosq[:, NOPE_DIM:NOPE_DIM + ROPE_DIM] = cos32
    sinq[:, NOPE_DIM:NOPE_DIM + ROPE_DIM] = sin32
    cosk = np.zeros((n, HEAD_SLOT))
    sink = np.zeros((n, HEAD_SLOT))
    cosk[:, :ROPE_DIM] = cos32
    sink[:, :ROPE_DIM] = sin32
    return tuple(t.astype(np.float32) for t in (cosq, sinq, cosk, sink))


def _fnet_channel_table():
    j = np.arange(FN_GROUP_W)
    ang = 2.0 * np.pi * ((j[:, None] * j[None, :]) % FN_GROUP_W) / FN_GROUP_W
    out = np.zeros((FN_W, 2 * FN_W))
    for g in range(FN_GROUPS):
        sl = slice(g * FN_GROUP_W, (g + 1) * FN_GROUP_W)
        out[sl, sl] = np.cos(ang)
        out[sl, FN_W + g * FN_GROUP_W:FN_W + (g + 1) * FN_GROUP_W] = np.sin(ang)
    return out.astype(np.float32)


def _rope_expand_table():
    e = np.zeros((ROPE_DIM, N_HEADS, HEAD_SLOT), np.float32)
    for j in range(ROPE_DIM):
        e[j, :, NOPE_DIM + j] = 1.0
    return e.reshape(ROPE_DIM, QK_W)


_TABLES = {}


def _tables():
    if not _TABLES:
        for L in (SEQ, DEC_SEQ):
            _TABLES[("hy", L)] = _dft_tables(L, True)
            _TABLES[("fn", L)] = _dft_tables(L, False)
            _TABLES[("filt", L)] = _filter_tables(L)
        _TABLES["rope"] = _rope_tables()
        _TABLES["fnch"] = _fnet_channel_table()
        _TABLES["ropeexp"] = _rope_expand_table()
    return _TABLES


def _rms(x, g):
    ms = jnp.mean(x * x, axis=-1, keepdims=True)
    return x * lax.rsqrt(ms + EPS) * g


def _sigmoid(x):
    return 1.0 / (1.0 + jnp.exp(-x))


def _dot(a, b):
    return jnp.dot(a, b, preferred_element_type=F32)


def _dot_hi(a, b):
    return jnp.dot(a, b, precision=HIGHEST, preferred_element_type=F32)


def _dot_nt(a, b):
    return lax.dot_general(a, b, (((1,), (1,)), ((), ())), preferred_element_type=F32)


def _store_values(v_ref, v):
    ones = jnp.ones((v.shape[0], PAIR_W), BF16)
    for pair in range(N_HEADS // 2):
        v_ref[:, 2 * pair * PAIR_W:(2 * pair + 1) * PAIR_W] = v[:, pair * PAIR_W:(pair + 1) * PAIR_W].astype(BF16)
        v_ref[:, (2 * pair + 1) * PAIR_W:(2 * pair + 2) * PAIR_W] = ones


def _group_of(tok0):
    return jnp.where(tok0 < T_CTX, 0, 1 + (tok0 - T_CTX) // DEC_SEQ)


def _params(sem):
    return pltpu.CompilerParams(dimension_semantics=sem, vmem_limit_bytes=VMEM_LIMIT)


def _ada_kernel(c_ref, w_ref, b_ref, o_ref):
    x = c_ref[...]
    s = x * _sigmoid(x)
    w = w_ref[...]
    o_ref[...] = jnp.zeros_like(o_ref)
    for g in range(N_GROUPS):
        o_ref[g:g + 1, :] = jnp.sum(w * s[:, g:g + 1], axis=0, keepdims=True) + b_ref[...]


def _ada_call(cvec, w_ada, b_ada):
    tn = 2304
    n_out = 9 * D_MODEL
    return pl.pallas_call(
        _ada_kernel,
        out_shape=jax.ShapeDtypeStruct((DEPTH, 8, n_out), F32),
        grid=(DEPTH, n_out // tn),
        in_specs=[
            pl.BlockSpec((D_MODEL, 8), lambda l, j: (0, 0)),
            pl.BlockSpec((None, D_MODEL, tn), lambda l, j: (l, 0, j)),
            pl.BlockSpec((None, 1, tn), lambda l, j: (l, 0, j)),
        ],
        out_specs=pl.BlockSpec((None, 8, tn), lambda l, j: (l, 0, j)),
        compiler_params=_params(("arbitrary", "arbitrary")),
        name="ada_modulation",
    )(cvec, w_ada, b_ada.reshape(DEPTH, 1, n_out))


def _ffn_kernel(*refs, j0, split_in, final):
    refs = list(refs)
    x_refs = [refs.pop(0) for _ in range(2 if split_in else 1)]
    mod_ref, ng_ref, wup_ref, wd_ref = refs[:4]
    refs = refs[4:]
    fg_ref = refs.pop(0) if final else None
    o_refs = [refs.pop(0) for _ in range(2 if final else 1)]
    wup_sc, wd_sc, hid_sc = refs
    s = pl.program_id(0)
    n_ctx_tiles = T_CTX // TM_FFN

    @pl.when(s < FFN_LOAD)
    def _():
        wup_sc[pl.ds(pl.multiple_of(s * FFN_UP_ROWS, FFN_UP_ROWS), FFN_UP_ROWS), :] = wup_ref[...].astype(BF16)
        wd_sc[pl.ds(pl.multiple_of(s * FFN_DN_ROWS, FFN_DN_ROWS), FFN_DN_ROWS), :] = wd_ref[...].astype(BF16)

    @pl.when(s >= FFN_LOAD)
    def _():
        t = s - FFN_LOAD
        m = mod_ref[_group_of(t * TM_FFN)]
        ys = []
        for sub in range(FFN_SUB):
            rows = slice(sub * (TM_FFN // FFN_SUB), (sub + 1) * (TM_FFN // FFN_SUB))
            if split_in:
                x = jnp.where(t < n_ctx_tiles, x_refs[0][rows, :], x_refs[1][rows, :])
            else:
                x = x_refs[0][rows, :]
            n = (_rms(x, ng_ref[...]) * (1.0 + m[j0 + 1:j0 + 2]) + m[j0:j0 + 1]).astype(BF16)
            for c in range(D_FF // TF_FFN):
                cols = slice(c * TF_FFN, (c + 1) * TF_FFN)
                g = _dot(n, wup_sc[:, cols])
                u = _dot(n, wup_sc[:, D_FF + c * TF_FFN:D_FF + (c + 1) * TF_FFN])
                hid_sc[rows, cols] = (g * _sigmoid(g) * u).astype(BF16)
            y = x + 0.5 * m[j0 + 2:j0 + 3] * _dot(hid_sc[rows, :], wd_sc[...])
            if final:
                y = _rms(y, fg_ref[...])
            ys.append((rows, y))
        if final:
            @pl.when(t < n_ctx_tiles)
            def _():
                for rows, y in ys:
                    o_refs[0][rows, :] = y

            @pl.when(t >= n_ctx_tiles)
            def _():
                for rows, y in ys:
                    o_refs[1][rows, :] = y
        else:
            for rows, y in ys:
                o_refs[0][rows, :] = y


def _ffn_call(xs, mods, ng, w_up, w_down, l, f, j0, final_g=None):
    split_in = len(xs) == 2
    final = final_g is not None
    n_ctx_tiles = T_CTX // TM_FFN
    tile = lambda s: jnp.maximum(s - FFN_LOAD, 0)
    chunk = lambda s: jnp.minimum(s, FFN_LOAD - 1)
    ctx_blk = lambda s: (jnp.minimum(tile(s), n_ctx_tiles - 1), 0)
    lat_blk = lambda s: (jnp.maximum(tile(s) - n_ctx_tiles, 0), 0)
    row = pl.BlockSpec((1, D_MODEL), lambda s: (0, 0))
    tok = lambda index_map: pl.BlockSpec((TM_FFN, D_MODEL), index_map)
    if split_in:
        in_specs = [tok(ctx_blk), tok(lat_blk)]
    else:
        in_specs = [tok(lambda s: (tile(s), 0))]
    in_specs += [
        pl.BlockSpec((N_GROUPS, 9, D_MODEL), lambda s: (0, 0, 0)),
        row,
        pl.BlockSpec((None, None, FFN_UP_ROWS, 2 * D_FF), lambda s: (l, f, chunk(s), 0)),
        pl.BlockSpec((None, None, FFN_DN_ROWS, D_MODEL), lambda s: (l, f, chunk(s), 0)),
    ]
    args = list(xs) + [mods, ng, w_up, w_down]
    if final:
        in_specs.append(row)
        args.append(final_g)
        out_shape = (jax.ShapeDtypeStruct((T_CTX, D_MODEL), F32), jax.ShapeDtypeStruct((T_LAT, D_MODEL), F32))
        out_specs = (tok(ctx_blk), tok(lat_blk))
    else:
        out_shape = jax.ShapeDtypeStruct((T_ALL, D_MODEL), F32)
        out_specs = tok(lambda s: (tile(s), 0))
    return pl.pallas_call(
        functools.partial(_ffn_kernel, j0=j0, split_in=split_in, final=final),
        out_shape=out_shape,
        grid=(FFN_LOAD + T_ALL // TM_FFN,),
        in_specs=in_specs,
        out_specs=out_specs,
        scratch_shapes=[pltpu.VMEM((D_MODEL, 2 * D_FF), BF16), pltpu.VMEM((D_FF, D_MODEL), BF16),
                        pltpu.VMEM((TM_FFN, D_FF), BF16)],
        compiler_params=_params(("arbitrary",)),
        name="swiglu_half_step",
    )(*args)


def _rope_partner(x):
    lane = lax.broadcasted_iota(jnp.int32, (1, HEAD_SLOT), 1)
    first = (lane % (2 * ROPE_HALF)) < ROPE_HALF
    return jnp.where(first, -pltpu.roll(x, HEAD_SLOT - ROPE_HALF, 1), pltpu.roll(x, ROPE_HALF, 1))


def _mixin_kernel(h_ref, mod_ref, ng_ref, w_ref, qg_ref, kvg_ref, wq_ref, wqs_ref, wk_ref, wv_ref,
                  fch_ref, cq_ref, sq_ref, ck_ref, sk_ref,
                  hy_ref, q_ref, k_ref, v_ref, ckv_ref, kr_ref, fcs_ref):
    i = pl.program_id(0)
    m = mod_ref[_group_of(i * TM_MIX)]
    n = (_rms(h_ref[...], ng_ref[...]) * (1.0 + m[4:5]) + m[3:4]).astype(BF16)
    proj = _dot_nt(n, w_ref[...])
    hy_ref[...] = proj[:, MIX_HY:MIX_QA]
    q_a = proj[:, MIX_QA:MIX_KVA]
    kv_a = proj[:, MIX_KVA:MIX_FN]
    fn = proj[:, MIX_FN:MIX_KR]
    k_r = proj[:, MIX_KR:MIX_W]

    qn = _rms(q_a, qg_ref[...]).astype(BF16)
    q = _dot(qn, wq_ref[...])
    q_partner = _dot(qn, wqs_ref[...])
    cos_q = cq_ref[...]
    sin_q = sq_ref[...]
    for h in range(N_HEADS):
        hs = slice(h * HEAD_SLOT, (h + 1) * HEAD_SLOT)
        q_ref[:, hs] = (q[:, hs] * cos_q + q_partner[:, hs] * sin_q).astype(BF16)

    ckv = _rms(kv_a, kvg_ref[...])

    @pl.when(i < T_CTX // TM_MIX)
    def _():
        ckv_ref[...] = ckv
        kr_ref[...] = k_r[:, :ROPE_DIM]

    ckv_b = ckv.astype(BF16)
    k_rot = k_r * ck_ref[...] + _rope_partner(k_r) * sk_ref[...]
    k_rope = pltpu.roll(k_rot, NOPE_DIM, 1)
    k_nope = _dot(ckv_b, wk_ref[...])
    for h in range(N_HEADS):
        hs = slice(h * HEAD_SLOT, (h + 1) * HEAD_SLOT)
        k_ref[:, hs] = (k_nope[:, hs] + k_rope).astype(BF16)
    _store_values(v_ref, _dot(ckv_b, wv_ref[...]))
    fcs_ref[...] = _dot(fn.astype(BF16), fch_ref[...]).astype(BF16)


def _mixin_call(h, mods, ng, w_mix, l, qg, kvg, wq, wqs, wk, wv, fch, cosq, sinq, cosk, sink):
    tm = TM_MIX

    def pos_block(i):
        tok0 = i * tm
        return jnp.where(tok0 < T_CTX, DEC_SEQ // tm, ((tok0 - T_CTX) % DEC_SEQ) // tm)

    def full(shape):
        return pl.BlockSpec(shape, lambda i: (0,) * len(shape))

    def tok(width):
        return pl.BlockSpec((tm, width), lambda i: (i, 0))

    def pos(width):
        return pl.BlockSpec((tm, width), lambda i: (pos_block(i), 0))

    out_shape = (
        jax.ShapeDtypeStruct((T_ALL, 3 * HY_W), F32),
        jax.ShapeDtypeStruct((T_ALL, QK_W), BF16),
        jax.ShapeDtypeStruct((T_ALL, QK_W), BF16),
        jax.ShapeDtypeStruct((T_ALL, V_EXT_W), BF16),
        jax.ShapeDtypeStruct((T_CTX, KV_LORA), F32),
        jax.ShapeDtypeStruct((T_CTX, ROPE_DIM), F32),
        jax.ShapeDtypeStruct((T_ALL, 2 * FN_W), BF16),
    )
    return pl.pallas_call(
        _mixin_kernel,
        out_shape=out_shape,
        grid=(T_ALL // tm,),
        in_specs=[
            tok(D_MODEL), full((N_GROUPS, 9, D_MODEL)), full((1, D_MODEL)),
            pl.BlockSpec((None, MIX_W, D_MODEL), lambda i: (l, 0, 0)),
            full((1, Q_LORA)), full((1, KV_LORA)), full((Q_LORA, QK_W)), full((Q_LORA, QK_W)),
            full((KV_LORA, QK_W)), full((KV_LORA, N_HEADS * V_DIM)),
            full((FN_W, 2 * FN_W)), pos(HEAD_SLOT), pos(HEAD_SLOT), pos(HEAD_SLOT), pos(HEAD_SLOT),
        ],
        out_specs=tuple(
            tok(s.shape[1]) if s.shape[0] == T_ALL else
            pl.BlockSpec((tm, s.shape[1]), lambda i: (jnp.minimum(i, T_CTX // tm - 1), 0))
            for s in out_shape),
        compiler_params=_params(("arbitrary",)),
        name="mixer_input_proj",
    )(h, mods, ng, w_mix, qg, kvg, wq, wqs, wk, wv, fch, cosq, sinq, cosk, sink)


def _cachekv_kernel(ckv_ref, kr_ref, wk_ref, wv_ref, ek_ref, k_ref, v_ref):
    ckv_b = ckv_ref[...].astype(BF16)
    k_ref[...] = (_dot(ckv_b, wk_ref[...]) + _dot(kr_ref[...].astype(BF16), ek_ref[...])).astype(BF16)
    _store_values(v_ref, _dot(ckv_b, wv_ref[...]))


def _cachekv_call(cache_ckv, cache_krope, wk, wv, ek, l):
    def full(shape):
        return pl.BlockSpec(shape, lambda b: (0,) * len(shape))

    return pl.pallas_call(
        _cachekv_kernel,
        out_shape=(jax.ShapeDtypeStruct((DEC_BATCH, PAST_LEN, QK_W), BF16),
                   jax.ShapeDtypeStruct((DEC_BATCH, PAST_LEN, V_EXT_W), BF16)),
        grid=(DEC_BATCH,),
        in_specs=[
            pl.BlockSpec((None, None, PAST_LEN, KV_LORA), lambda b: (b, l, 0, 0)),
            pl.BlockSpec((None, None, PAST_LEN, ROPE_DIM), lambda b: (b, l, 0, 0)),
            full((KV_LORA, QK_W)), full((KV_LORA, N_HEADS * V_DIM)), full((ROPE_DIM, QK_W)),
        ],
        out_specs=(pl.BlockSpec((None, PAST_LEN, QK_W), lambda b: (b, 0, 0)),
                   pl.BlockSpec((None, PAST_LEN, V_EXT_W), lambda b: (b, 0, 0))),
        compiler_params=_params(("parallel",)),
        name="cached_context_kv",
    )(cache_ckv, cache_krope, wk, wv, ek)


def _attn_kernel(q_ref, k_ref, v_ref, *rest, has_cache):
    if has_cache:
        kc_ref, vc_ref, o_ref = rest
    else:
        (o_ref,) = rest
    scale2 = math.log2(math.e) / math.sqrt(NOPE_DIM + ROPE_DIM)
    first_half = lax.broadcasted_iota(jnp.int32, (1, PAIR_W), 1) < V_DIM
    for pair in range(N_HEADS // 2):
        vs = slice(2 * pair * PAIR_W, (2 * pair + 2) * PAIR_W)
        outs = []
        for e in range(2):
            hs = slice((2 * pair + e) * HEAD_SLOT, (2 * pair + e + 1) * HEAD_SLOT)
            qh = q_ref[:, hs]
            s = _dot_nt(qh, k_ref[:, hs])
            mx = jnp.max(s, axis=-1, keepdims=True)
            if has_cache:
                sc = _dot_nt(qh, kc_ref[:, hs])
                mx = jnp.maximum(mx, jnp.max(sc, axis=-1, keepdims=True))
            p = jnp.exp2((s - mx) * scale2)
            if has_cache:
                o = _dot(p.astype(BF16), v_ref[:, vs])
                o = o + _dot(jnp.exp2((sc - mx) * scale2).astype(BF16), vc_ref[:, vs])
                outs.append(o[:, :PAIR_W] / o[:, PAIR_W:PAIR_W + 1])
            else:
                o = _dot(p.astype(BF16), v_ref[:, vs.start:vs.start + PAIR_W])
                outs.append(o / jnp.sum(p, axis=-1, keepdims=True))
        o_ref[:, pair * PAIR_W:(pair + 1) * PAIR_W] = jnp.where(first_half, outs[0], outs[1]).astype(BF16)


def _attn_call(q, k, v, kc, vc, *, n_seq, seq_len, tok_off, tq):
    has_cache = kc is not None
    nq = seq_len // tq
    seq_blk0 = tok_off // seq_len
    q_blk0 = tok_off // tq
    vw = N_HEADS * V_DIM
    in_specs = [
        pl.BlockSpec((tq, QK_W), lambda b, j: (q_blk0 + b * nq + j, 0)),
        pl.BlockSpec((seq_len, QK_W), lambda b, j: (seq_blk0 + b, 0)),
        pl.BlockSpec((seq_len, V_EXT_W), lambda b, j: (seq_blk0 + b, 0)),
    ]
    args = [q, k, v]
    if has_cache:
        in_specs += [pl.BlockSpec((None, PAST_LEN, QK_W), lambda b, j: (b, 0, 0)),
                     pl.BlockSpec((None, PAST_LEN, V_EXT_W), lambda b, j: (b, 0, 0))]
        args += [kc, vc]
    return pl.pallas_call(
        functools.partial(_attn_kernel, has_cache=has_cache),
        out_shape=jax.ShapeDtypeStruct((n_seq * seq_len, vw), BF16),
        grid=(n_seq, nq),
        in_specs=in_specs,
        out_specs=pl.BlockSpec((tq, vw), lambda b, j: (b * nq + j, 0)),
        compiler_params=_params(("parallel", "parallel")),
        name="mla_attention_cache" if has_cache else "mla_attention",
    )(*args)


def _hyfilt_kernel(feat_ref, w1_ref, b1_ref, w2_ref, b2_ref, w3_ref, dec_ref, alt_ref, c_ref, s_ref,
                   hre_ref, him_ref, nyq_ref, a_sc, b_sc, *, L):
    j = pl.program_id(0)
    half = HY_ORDER * HY_W

    @pl.when(j == 0)
    def _():
        h = jnp.sin(_dot_hi(feat_ref[...], w1_ref[...]) + b1_ref[...])
        h = jnp.sin(_dot_hi(h, w2_ref[...]) + b2_ref[...])
        h = _dot_hi(h, w3_ref[...])
        dec = dec_ref[...]
        rows = lax.broadcasted_iota(jnp.int32, (L, 1), 0)
        h_fwd = h[:, :half] * dec
        h_bwd = jnp.where(rows > 0, h[:, half:] * dec, 0.0)
        norm = (jnp.sum(jnp.abs(h_fwd), axis=0, keepdims=True)
                + jnp.sum(jnp.abs(h_bwd), axis=0, keepdims=True) + EPS)
        inv = 1.0 / norm
        a = (h_fwd + h_bwd) * inv
        b = (h_fwd - h_bwd) * inv
        a_sc[...] = a.astype(BF16)
        b_sc[...] = b.astype(BF16)
        nyq_ref[...] = jnp.sum(a * alt_ref[...], axis=0, keepdims=True)

    hre_ref[...] = _dot(c_ref[...], a_sc[...])
    him_ref[...] = -_dot(s_ref[...], b_sc[...])


def _hyfilt_call(L, feats, w1, b1, w2, b2, w3, decay, alt, ctab, stab):
    tk = min(L, 512)
    half = HY_ORDER * HY_W

    def full(shape):
        return pl.BlockSpec(shape, lambda j: (0,) * len(shape))

    return pl.pallas_call(
        functools.partial(_hyfilt_kernel, L=L),
        out_shape=(jax.ShapeDtypeStruct((L, half), F32), jax.ShapeDtypeStruct((L, half), F32),
                   jax.ShapeDtypeStruct((1, half), F32)),
        grid=(L // tk,),
        in_specs=[
            full((L, POS_PAD)), full((POS_PAD, FILT_HID)), full((1, FILT_HID)), full((FILT_HID, FILT_HID)),
            full((1, FILT_HID)), full((FILT_HID, 2 * half)), full((L, half)), full((L, 1)),
            pl.BlockSpec((tk, L), lambda j: (j, 0)), pl.BlockSpec((tk, L), lambda j: (j, 0)),
        ],
        out_specs=(pl.BlockSpec((tk, half), lambda j: (j, 0)), pl.BlockSpec((tk, half), lambda j: (j, 0)),
                   full((1, half))),
        scratch_shapes=[pltpu.VMEM((L, half), BF16), pltpu.VMEM((L, half), BF16)],
        compiler_params=_params(("arbitrary",)),
        name="hyena_filter_spectrum_%d" % L,
    )(feats, w1, b1, w2, b2, w3, decay, alt, ctab, stab)


def _hyconv_kernel(hy_ref, cw_ref, cb_ref, skip_ref, hre_ref, him_ref, nyq_ref, alt_ref, c_ref, s_ref, o_ref,
                   x1_sc, x2_sc, v_sc, vb_sc, yre_sc, yim_sc, nv_sc, *, n_seq, L, tk):
    p = pl.program_id(0)
    ph = pl.program_id(1)
    j = pl.program_id(2)
    cols = [slice(b * HY_W, (b + 1) * HY_W) for b in range(n_seq)]
    blk = pl.ds(pl.multiple_of(j * tk, tk), tk)

    @pl.when((p == 0) & (ph == 0) & (j == 0))
    def _():
        rows = lax.broadcasted_iota(jnp.int32, (L, 1), 0)
        w = cw_ref[...]
        for b in range(n_seq):
            x = hy_ref[b * L:(b + 1) * L, :]
            prev = jnp.where(rows > 0, pltpu.roll(x, 1, 0), 0.0)
            nxt = jnp.where(rows < L - 1, pltpu.roll(x, L - 1, 0), 0.0)
            u = prev * w[0:1] + x * w[1:2] + nxt * w[2:3] + cb_ref[...]
            x1_sc[:, cols[b]] = u[:, :HY_W]
            x2_sc[:, cols[b]] = u[:, HY_W:2 * HY_W]
            v_sc[:, cols[b]] = u[:, 2 * HY_W:]

    @pl.when((ph == 0) & (j == 0))
    def _():
        v = v_sc[...]
        vb_sc[...] = v.astype(BF16)
        nv_sc[...] = jnp.sum(v * alt_ref[...], axis=0, keepdims=True)

    @pl.when(ph == 0)
    def _():
        vb = vb_sc[...]
        v_re = _dot(c_ref[...], vb)
        v_im = -_dot(s_ref[...], vb)
        freq = j * tk + lax.broadcasted_iota(jnp.int32, (tk, 1), 0)
        wk = jnp.where(freq == 0, 0.5, 1.0)
        h_re = hre_ref[...] * wk
        h_im = him_ref[...] * wk
        for b in range(n_seq):
            yre_sc[blk, cols[b]] = (v_re[:, cols[b]] * h_re - v_im[:, cols[b]] * h_im).astype(BF16)
            yim_sc[blk, cols[b]] = (v_re[:, cols[b]] * h_im + v_im[:, cols[b]] * h_re).astype(BF16)

    @pl.when(ph == 1)
    def _():
        acc = _dot(c_ref[...], yre_sc[...]) - _dot(s_ref[...], yim_sc[...])
        alt = alt_ref[blk, :]
        for b in range(n_seq):
            v = v_sc[blk, cols[b]]
            nyq = nv_sc[:, cols[b]] * nyq_ref[...]
            y = acc[:, cols[b]] * (1.0 / L) + (0.5 / L) * alt * nyq + v * skip_ref[...]

            @pl.when(p == 0)
            def _():
                v_sc[blk, cols[b]] = x1_sc[blk, cols[b]] * y

            @pl.when(p == 1)
            def _():
                o_ref[pl.ds(pl.multiple_of(b * L + j * tk, tk), tk), :] = (x2_sc[blk, cols[b]] * y).astype(BF16)


def _hyconv_call(hy, conv_w, conv_b, skip, hre, him, nyq, alt, ctab, stab, *, n_seq, L, tok_off):
    tk = min(L, 512)
    nk = L // tk
    seg = tok_off // (n_seq * L)
    width = n_seq * HY_W

    def full(shape):
        return pl.BlockSpec(shape, lambda p, ph, j: (0,) * len(shape))

    def filt(p, ph, j):
        return (jnp.where(ph == 0, j, nk - 1), p)

    vm = lambda dtype: pltpu.VMEM((L, width), dtype)
    return pl.pallas_call(
        functools.partial(_hyconv_kernel, n_seq=n_seq, L=L, tk=tk),
        out_shape=jax.ShapeDtypeStruct((n_seq * L, HY_W), BF16),
        grid=(HY_ORDER, 2, nk),
        in_specs=[
            pl.BlockSpec((n_seq * L, 3 * HY_W), lambda p, ph, j: (seg, 0), pipeline_mode=pl.Buffered(1)),
            full((3, 3 * HY_W)), full((1, 3 * HY_W)),
            pl.BlockSpec((None, 1, HY_W), lambda p, ph, j: (p, 0, 0)),
            pl.BlockSpec((tk, HY_W), filt), pl.BlockSpec((tk, HY_W), filt),
            pl.BlockSpec((None, 1, HY_W), lambda p, ph, j: (p, 0, 0)),
            full((L, 1)),
            pl.BlockSpec((tk, L), lambda p, ph, j: (j, 0)), pl.BlockSpec((tk, L), lambda p, ph, j: (j, 0)),
        ],
        out_specs=pl.BlockSpec((n_seq * L, HY_W), lambda p, ph, j: (0, 0)),
        scratch_shapes=[vm(F32), vm(F32), vm(F32), vm(BF16), vm(BF16), vm(BF16), pltpu.VMEM((1, width), F32)],
        compiler_params=_params(("arbitrary", "arbitrary", "arbitrary")),
        name="hyena_long_conv_%d" % L,
    )(hy, conv_w, conv_b, skip, hre, him, nyq, alt, ctab, stab)


def _fnet_kernel(fcs_ref, c_ref, s_ref, o_ref, fc_sc, fs_sc, *, n_seq, L, tk, scale):
    j = pl.program_id(0)
    cols = [slice(b * FN_W, (b + 1) * FN_W) for b in range(n_seq)]

    @pl.when(j == 0)
    def _():
        for b in range(n_seq):
            fc_sc[:, cols[b]] = fcs_ref[b * L:(b + 1) * L, :FN_W]
            fs_sc[:, cols[b]] = fcs_ref[b * L:(b + 1) * L, FN_W:]

    y = (_dot(c_ref[...], fc_sc[...]) - _dot(s_ref[...], fs_sc[...])) * scale
    for b in range(n_seq):
        o_ref[pl.ds(pl.multiple_of(b * L + j * tk, tk), tk), :] = y[:, cols[b]].astype(BF16)


def _fnet_call(fcs, ctab, stab, *, n_seq, L, tok_off):
    tk = min(L, 512)
    seg = tok_off // (n_seq * L)
    vm = lambda: pltpu.VMEM((L, n_seq * FN_W), BF16)
    return pl.pallas_call(
        functools.partial(_fnet_kernel, n_seq=n_seq, L=L, tk=tk, scale=1.0 / math.sqrt(L * FN_GROUP_W)),
        out_shape=jax.ShapeDtypeStruct((n_seq * L, FN_W), BF16),
        grid=(L // tk,),
        in_specs=[
            pl.BlockSpec((n_seq * L, 2 * FN_W), lambda j: (seg, 0)),
            pl.BlockSpec((tk, L), lambda j: (j, 0)), pl.BlockSpec((tk, L), lambda j: (j, 0)),
        ],
        out_specs=pl.BlockSpec((n_seq * L, FN_W), lambda j: (0, 0)),
        scratch_shapes=[vm(), vm()],
        compiler_params=_params(("arbitrary",)),
        name="fnet_position_dft_%d" % L,
    )(fcs, ctab, stab)


def _mixout_kernel(h_ref, mod_ref, ng_ref, wg_ref, zc_ref, zl_ref, ac_ref, al_ref, fc_ref, fl_ref,
                   wa_ref, wb_ref, wc_ref, wo_ref, o_ref):
    i = pl.program_id(0)
    is_ctx = i < T_CTX // TM_MIX
    m = mod_ref[_group_of(i * TM_MIX)]
    h = h_ref[...]
    n = (_rms(h, ng_ref[...]) * (1.0 + m[4:5]) + m[3:4]).astype(BF16)
    z = jnp.where(is_ctx, zc_ref[...], zl_ref[...])
    a = jnp.where(is_ctx, ac_ref[...], al_ref[...])
    f = jnp.where(is_ctx, fc_ref[...], fl_ref[...])
    acc = _sigmoid(_dot_nt(n, wg_ref[0:D_MODEL, :])) * _dot(z, wa_ref[...])
    acc = acc + _sigmoid(_dot_nt(n, wg_ref[D_MODEL:2 * D_MODEL, :])) * _dot(a, wb_ref[...])
    acc = acc + _sigmoid(_dot_nt(n, wg_ref[2 * D_MODEL:, :])) * _dot(f, wc_ref[...])
    y = _dot(acc.astype(BF16), wo_ref[...])
    o_ref[...] = h + m[5:6] * y


def _mixout_call(h, mods, ng, w_gate, l, z, a, f, wa, wb, wc, wo):
    tm = TM_MIX
    n_ctx = T_CTX // tm

    def full(shape):
        return pl.BlockSpec(shape, lambda i: (0,) * len(shape))

    def tok(width):
        return pl.BlockSpec((tm, width), lambda i: (i, 0))

    def pair(width):
        return [pl.BlockSpec((tm, width), lambda i: (jnp.minimum(i, n_ctx - 1), 0)),
                pl.BlockSpec((tm, width), lambda i: (jnp.maximum(i - n_ctx, 0), 0))]

    return pl.pallas_call(
        _mixout_kernel,
        out_shape=jax.ShapeDtypeStruct((T_ALL, D_MODEL), F32),
        grid=(T_ALL // tm,),
        in_specs=[
            tok(D_MODEL), full((N_GROUPS, 9, D_MODEL)), full((1, D_MODEL)),
            pl.BlockSpec((None, N_BRANCH * D_MODEL, D_MODEL), lambda i: (l, 0, 0)),
            *pair(HY_W), *pair(N_HEADS * V_DIM), *pair(FN_W),
            full((HY_W, D_MODEL)), full((N_HEADS * V_DIM, D_MODEL)), full((FN_W, D_MODEL)),
            full((D_MODEL, D_MODEL)),
        ],
        out_specs=tok(D_MODEL),
        compiler_params=_params(("parallel",)),
        name="gated_merge_out_proj",
    )(h, mods, ng, w_gate, *z, *a, *f, wa, wb, wc, wo)


def _wprep_kernel(w_ref, mix_ref, gate_ref):
    r = pl.program_id(1)
    n_gate = N_BRANCH * D_MODEL
    g0 = IN_GATE - PREP_ROWS
    last = n_gate - (3 * PREP_ROWS - IN_GATE)

    @pl.when(r == 0)
    def _():
        mix_ref[:MIX_FN, :] = w_ref[...].astype(BF16)

    @pl.when(r == 1)
    def _():
        mix_ref[MIX_FN:MIX_KR, :] = w_ref[ROPE_DIM:g0, :].astype(BF16)
        mix_ref[MIX_KR:MIX_KR + ROPE_DIM, :] = w_ref[0:ROPE_DIM, :].astype(BF16)
        mix_ref[MIX_KR + ROPE_DIM:, :] = jnp.zeros((MIX_W - MIX_KR - ROPE_DIM, D_MODEL), BF16)
        gate_ref[:PREP_ROWS - g0, :] = w_ref[g0:, :].astype(BF16)

    @pl.when(r == 2)
    def _():
        gate_ref[PREP_ROWS - g0:2 * PREP_ROWS - g0, :] = w_ref[...].astype(BF16)

    @pl.when(r == 3)
    def _():
        gate_ref[2 * PREP_ROWS - g0:, :] = w_ref[:last, :].astype(BF16)


def _wprep_call(w_in_t):
    assert IN_KR == PREP_ROWS and 4 * PREP_ROWS >= IN_COLS
    return pl.pallas_call(
        _wprep_kernel,
        out_shape=(jax.ShapeDtypeStruct((DEPTH, MIX_W, D_MODEL), BF16),
                   jax.ShapeDtypeStruct((DEPTH, N_BRANCH * D_MODEL, D_MODEL), BF16)),
        grid=(DEPTH, 4),
        in_specs=[pl.BlockSpec((None, PREP_ROWS, D_MODEL), lambda l, r: (l, r, 0))],
        out_specs=(pl.BlockSpec((None, MIX_W, D_MODEL), lambda l, r: (l, 0, 0)),
                   pl.BlockSpec((None, N_BRANCH * D_MODEL, D_MODEL), lambda l, r: (l, 0, 0))),
        compiler_params=_params(("arbitrary", "arbitrary")),
        name="input_proj_weight_relayout",
    )(w_in_t)


def _layer_weights(l, w_qb, w_kvb):
    wq3 = w_qb[l].reshape(Q_LORA, N_HEADS, NOPE_DIM + ROPE_DIM)
    slot_pad = HEAD_SLOT - NOPE_DIM - ROPE_DIM
    wq = jnp.pad(wq3, ((0, 0), (0, 0), (0, slot_pad))).reshape(Q_LORA, QK_W).astype(BF16)
    rope = wq3[:, :, NOPE_DIM:].reshape(Q_LORA, N_HEADS, 2, 2, ROPE_HALF)
    partner = jnp.stack([-rope[:, :, :, 1], rope[:, :, :, 0]], axis=3).reshape(Q_LORA, N_HEADS, ROPE_DIM)
    wqs = jnp.pad(partner, ((0, 0), (0, 0), (NOPE_DIM, slot_pad))).reshape(Q_LORA, QK_W).astype(BF16)

    wkv3 = w_kvb[l].reshape(KV_LORA, N_HEADS, NOPE_DIM + V_DIM)
    wk = jnp.pad(wkv3[:, :, :NOPE_DIM], ((0, 0), (0, 0), (0, HEAD_SLOT - NOPE_DIM))).reshape(KV_LORA, QK_W).astype(BF16)
    wv = wkv3[:, :, NOPE_DIM:].reshape(KV_LORA, N_HEADS * V_DIM).astype(BF16)
    return wq, wqs, wk, wv


def kernel(x_prompt, x_sample, cache_ckv, cache_krope, c, c_ctx, w_ada, b_ada, norm_g, w_ffn_up, w_ffn_down,
           w_in, hy_conv_w, hy_conv_b, hy_filt_w1, hy_filt_b1, hy_filt_w2, hy_filt_b2, hy_filt_w3, hy_skip,
           w_hy_out, q_norm_g, w_qb, kv_norm_g, w_kvb, w_mla_o, w_fnet, w_out, final_g):
    tabs = _tables()
    cosq, sinq, cosk, sink = (jnp.asarray(t) for t in tabs["rope"])
    fch = jnp.asarray(tabs["fnch"]).astype(BF16)
    ek = jnp.asarray(tabs["ropeexp"]).astype(BF16)
    dft = {}
    for L in (SEQ, DEC_SEQ):
        dft[("hy", L)] = tuple(jnp.asarray(t).astype(BF16) for t in tabs[("hy", L)])
        dft[("fn", L)] = tuple(jnp.asarray(t).astype(BF16) for t in tabs[("fn", L)])

    cvec = jnp.concatenate([c_ctx[None, :], c, jnp.zeros((8 - N_GROUPS, D_MODEL), F32)], axis=0).T
    ada = _ada_call(cvec, w_ada, b_ada)
    mods_all = ada[:, :N_GROUPS].reshape(DEPTH, N_GROUPS, 9, D_MODEL)

    w_mix, w_gate = _wprep_call(jnp.swapaxes(w_in, 1, 2))
    w_hy_out, w_mla_o, w_fnet, w_out = (w.astype(BF16) for w in (w_hy_out, w_mla_o, w_fnet, w_out))
    hs = (x_prompt.reshape(T_CTX, D_MODEL), x_sample.reshape(T_LAT, D_MODEL))
    segs = ((BATCH, SEQ, 0), (DEC_BATCH, DEC_SEQ, T_CTX))
    ckv_out = []
    kr_out = []
    for l in range(DEPTH):
        mods = mods_all[l]
        ng = norm_g[l]
        wq, wqs, wk, wv = _layer_weights(l, w_qb, w_kvb)

        h = _ffn_call(hs, mods, ng[0:1], w_ffn_up, w_ffn_down, l, 0, 0)

        hy, q, k, v, ckv, k_r, fcs = _mixin_call(
            h, mods, ng[1:2], w_mix, l, q_norm_g[l][None, :], kv_norm_g[l][None, :], wq, wqs, wk, wv, fch,
            cosq, sinq, cosk, sink)
        ckv_out.append(ckv.reshape(BATCH, SEQ, KV_LORA))
        kr_out.append(k_r.reshape(BATCH, SEQ, ROPE_DIM))

        kc, vc = _cachekv_call(cache_ckv, cache_krope, wk, wv, ek, l)

        w1 = jnp.pad(hy_filt_w1[l], ((0, POS_PAD - POS_EMB), (0, 0)))
        skip = hy_skip[l].reshape(HY_ORDER, 1, HY_W)
        z_parts, a_parts, f_parts = [], [], []
        for n_seq, L, off in segs:
            feats, decay, alt = (jnp.asarray(t) for t in tabs[("filt", L)])
            c_hy, s_hy = dft[("hy", L)]
            hre, him, nyq = _hyfilt_call(L, feats, w1, hy_filt_b1[l][None, :], hy_filt_w2[l],
                                         hy_filt_b2[l][None, :], hy_filt_w3[l], decay, alt, c_hy, s_hy)
            z_parts.append(_hyconv_call(hy, hy_conv_w[l], hy_conv_b[l][None, :], skip, hre, him,
                                        nyq.reshape(HY_ORDER, 1, HY_W), alt, c_hy, s_hy,
                                        n_seq=n_seq, L=L, tok_off=off))
            cache = (kc, vc) if off else (None, None)
            a_parts.append(_attn_call(q, k, v, *cache, n_seq=n_seq, seq_len=L, tok_off=off, tq=min(L, 512)))
            c_fn, s_fn = dft[("fn", L)]
            f_parts.append(_fnet_call(fcs, c_fn, s_fn, n_seq=n_seq, L=L, tok_off=off))
        h = _mixout_call(h, mods, ng[1:2], w_gate, l, z_parts, a_parts, f_parts,
                         w_hy_out[l], w_mla_o[l], w_fnet[l], w_out[l])

        last = l == DEPTH - 1
        out = _ffn_call((h,), mods, ng[2:3], w_ffn_up, w_ffn_down, l, 1, 6, final_g[None, :] if last else None)
        hs = out if last else (out,)

    y_prompt = hs[0].reshape(BATCH, SEQ, D_MODEL)
    y_sample = hs[1].reshape(DEC_BATCH, DEC_SEQ, D_MODEL)
    return y_prompt, y_sample, jnp.stack(ckv_out, axis=1), jnp.stack(kr_out, axis=1)
```

```python
import functools
import math

import numpy as np
import jax
import jax.numpy as jnp
from jax import lax
from jax.experimental import pallas as pl
from jax.experimental.pallas import tpu as pltpu

F32 = jnp.float32
BF16 = jnp.bfloat16
HIGHEST = lax.Precision.HIGHEST

D_MODEL = 1024
BATCH = 16
SEQ = 256
DEPTH = 2
DEC_BATCH = 2
DEC_SEQ = 2048
PAST_LEN = 512
GRID_W = 64
HY_W = 256
HY_ORDER = 2
N_BANDS = 8
POS_EMB = 1 + 2 * N_BANDS
FILT_HID = 64
N_HEADS = 8
Q_LORA = 256
KV_LORA = 128
NOPE_DIM = 64
ROPE_DIM = 32
V_DIM = 64
ROPE_BASE = 10000.0
FN_GROUPS = 4
FN_GROUP_W = 64
FN_W = FN_GROUPS * FN_GROUP_W
N_BRANCH = 3
D_FF = 2816
EPS = 1e-6

T_CTX = BATCH * SEQ
T_LAT = DEC_BATCH * DEC_SEQ
T_ALL = T_CTX + T_LAT
N_GROUPS = 1 + DEC_BATCH
HEAD_SLOT = 128
QK_W = N_HEADS * HEAD_SLOT
PAIR_W = 2 * V_DIM
V_EXT_W = N_HEADS * PAIR_W
POS_PAD = 128

VMEM_LIMIT = 56 * 1024 * 1024

TM_FFN = 512
TF_FFN = 256
FFN_LOAD = 8
FFN_UP_ROWS = D_MODEL // FFN_LOAD
FFN_DN_ROWS = D_FF // FFN_LOAD
FFN_SUB = 2
TM_MIX = 512
PREP_ROWS = 1152

IN_KR = 3 * HY_W + Q_LORA + KV_LORA
IN_FN = IN_KR + ROPE_DIM
IN_GATE = IN_FN + FN_W
IN_COLS = IN_GATE + N_BRANCH * D_MODEL

MIX_HY = 0
MIX_QA = 3 * HY_W
MIX_KVA = MIX_QA + Q_LORA
MIX_FN = MIX_KVA + KV_LORA
MIX_KR = MIX_FN + FN_W
MIX_W = MIX_KR + 128
ROPE_HALF = ROPE_DIM // 4


def _dft_tables(L, half):
    k = np.arange(L, dtype=np.int64)
    period = 2 * L if half else L
    m = (k[:, None] * k[None, :]) % period
    ang = 2.0 * np.pi * m.astype(np.float64) / period
    return np.cos(ang).astype(np.float32), np.sin(ang).astype(np.float32)


def _filter_tables(L):
    t = np.arange(L, dtype=np.float64)
    t_norm = t / (L - 1)
    w = 2.0 * np.pi * t / L
    bands = np.linspace(1e-4, N_BANDS - 1, N_BANDS)
    ang = w[:, None] * bands[None, :]
    feats = np.concatenate([t_norm[:, None], np.cos(ang), -np.sin(ang)], axis=-1)
    feats = np.pad(feats, ((0, 0), (0, POS_PAD - POS_EMB)))
    deltas = np.linspace(math.log(1e-2) / 1.5, math.log(1e-2) / 0.3, HY_W)
    decay = np.exp(-t_norm[:, None] * np.abs(deltas)[None, :])
    decay = np.concatenate([decay, decay], axis=1)
    alt = np.where(np.arange(L) % 2 == 0, 1.0, -1.0)[:, None]
    return feats.astype(np.float32), decay.astype(np.float32), alt.astype(np.float32)


def _rope_tables():
    t = np.arange(DEC_SEQ)
    row = (t // GRID_W).astype(np.float64)
    col = (t % GRID_W).astype(np.float64)
    nf = ROPE_DIM // 4
    inv = ROPE_BASE ** (-np.arange(nf, dtype=np.float64) / nf)
    ar = row[:, None] * inv[None, :]
    ac = col[:, None] * inv[None, :]
    cos32 = np.concatenate([np.cos(ar), np.cos(ar), np.cos(ac), np.cos(ac)], axis=1)
    sin32 = np.concatenate([np.sin(ar), np.sin(ar), np.sin(ac), np.sin(ac)], axis=1)
    cos32 = np.concatenate([cos32, np.ones((TM_MIX, ROPE_DIM))], axis=0)
    sin32 = np.concatenate([sin32, np.zeros((TM_MIX, ROPE_DIM))], axis=0)
    n = cos32.shape[0]
    cosq = np.ones((n, HEAD_SLOT))
    sinq = np.zeros((n, HEAD_SLOT))
    cosq[:, NOPE_DIM:NOPE_DIM + ROPE_DIM] = cos32
    sinq[:, NOPE_DIM:NOPE_DIM + ROPE_DIM] = sin32
    cosk = np.zeros((n, HEAD_SLOT))
    sink = np.zeros((n, HEAD_SLOT))
    cosk[:, :ROPE_DIM] = cos32
    sink[:, :ROPE_DIM] = sin32
    return tuple(t.astype(np.float32) for t in (cosq, sinq, cosk, sink))


def _fnet_channel_table():
    j = np.arange(FN_GROUP_W)
    ang = 2.0 * np.pi * ((j[:, None] * j[None, :]) % FN_GROUP_W) / FN_GROUP_W
    out = np.zeros((FN_W, 2 * FN_W))
    for g in range(FN_GROUPS):
        sl = slice(g * FN_GROUP_W, (g + 1) * FN_GROUP_W)
        out[sl, sl] = np.cos(ang)
        out[sl, FN_W + g * FN_GROUP_W:FN_W + (g + 1) * FN_GROUP_W] = np.sin(ang)
    return out.astype(np.float32)


def _rope_expand_table():
    e = np.zeros((ROPE_DIM, N_HEADS, HEAD_SLOT), np.float32)
    for j in range(ROPE_DIM):
        e[j, :, NOPE_DIM + j] = 1.0
    return e.reshape(ROPE_DIM, QK_W)


_TABLES = {}


def _tables():
    if not _TABLES:
        for L in (SEQ, DEC_SEQ):
            _TABLES[("hy", L)] = _dft_tables(L, True)
            _TABLES[("fn", L)] = _dft_tables(L, False)
            _TABLES[("filt", L)] = _filter_tables(L)
        _TABLES["rope"] = _rope_tables()
        _TABLES["fnch"] = _fnet_channel_table()
        _TABLES["ropeexp"] = _rope_expand_table()
    return _TABLES


def _rms(x, g):
    ms = jnp.mean(x * x, axis=-1, keepdims=True)
    return x * lax.rsqrt(ms + EPS) * g


def _sigmoid(x):
    return 1.0 / (1.0 + jnp.exp(-x))


def _dot(a, b):
    return jnp.dot(a, b, preferred_element_type=F32)


def _dot_hi(a, b):
    return jnp.dot(a, b, precision=HIGHEST, preferred_element_type=F32)


def _dot_x3(a, b):
    a_hi = a.astype(BF16)
    b_hi = b.astype(BF16)
    a_lo = (a - a_hi.astype(F32)).astype(BF16)
    b_lo = (b - b_hi.astype(F32)).astype(BF16)
    return _dot(a_hi, b_hi) + (_dot(a_hi, b_lo) + _dot(a_lo, b_hi))


def _dot_nt(a, b):
    return lax.dot_general(a, b, (((1,), (1,)), ((), ())), preferred_element_type=F32)


def _store_values(v_ref, v):
    ones = jnp.ones((v.shape[0], PAIR_W), BF16)
    for pair in range(N_HEADS // 2):
        v_ref[:, 2 * pair * PAIR_W:(2 * pair + 1) * PAIR_W] = v[:, pair * PAIR_W:(pair + 1) * PAIR_W].astype(BF16)
        v_ref[:, (2 * pair + 1) * PAIR_W:(2 * pair + 2) * PAIR_W] = ones


def _group_of(tok0):
    return jnp.where(tok0 < T_CTX, 0, 1 + (tok0 - T_CTX) // DEC_SEQ)


def _params(sem):
    return pltpu.CompilerParams(dimension_semantics=sem, vmem_limit_bytes=VMEM_LIMIT)


def _ada_kernel(c_ref, w_ref, b_ref, o_ref):
    x = c_ref[...]
    s = x * _sigmoid(x)
    w = w_ref[...]
    o_ref[...] = jnp.zeros_like(o_ref)
    for g in range(N_GROUPS):
        o_ref[g:g + 1, :] = jnp.sum(w * s[:, g:g + 1], axis=0, keepdims=True) + b_ref[...]


def _ada_call(cvec, w_ada, b_ada):
    tn = 2304
    n_out = 9 * D_MODEL
    return pl.pallas_call(
        _ada_kernel,
        out_shape=jax.ShapeDtypeStruct((DEPTH, 8, n_out), F32),
        grid=(DEPTH, n_out // tn),
        in_specs=[
            pl.BlockSpec((D_MODEL, 8), lambda l, j: (0, 0)),
            pl.BlockSpec((None, D_MODEL, tn), lambda l, j: (l, 0, j)),
            pl.BlockSpec((None, 1, tn), lambda l, j: (l, 0, j)),
        ],
        out_specs=pl.BlockSpec((None, 8, tn), lambda l, j: (l, 0, j)),
        compiler_params=_params(("arbitrary", "arbitrary")),
        name="ada_modulation",
    )(cvec, w_ada, b_ada.reshape(DEPTH, 1, n_out))


def _ffn_kernel(*refs, j0, split_in, final):
    refs = list(refs)
    x_refs = [refs.pop(0) for _ in range(2 if split_in else 1)]
    mod_ref, ng_ref, wup_ref, wd_ref = refs[:4]
    refs = refs[4:]
    fg_ref = refs.pop(0) if final else None
    o_refs = [refs.pop(0) for _ in range(2 if final else 1)]
    wup_sc, wd_sc, hid_sc = refs
    s = pl.program_id(0)
    n_ctx_tiles = T_CTX // TM_FFN

    @pl.when(s < FFN_LOAD)
    def _():
        wup_sc[pl.ds(pl.multiple_of(s * FFN_UP_ROWS, FFN_UP_ROWS), FFN_UP_ROWS), :] = wup_ref[...].astype(BF16)
        wd_sc[pl.ds(pl.multiple_of(s * FFN_DN_ROWS, FFN_DN_ROWS), FFN_DN_ROWS), :] = wd_ref[...].astype(BF16)

    @pl.when(s >= FFN_LOAD)
    def _():
        t = s - FFN_LOAD
        m = mod_ref[_group_of(t * TM_FFN)]
        ys = []
        for sub in range(FFN_SUB):
            rows = slice(sub * (TM_FFN // FFN_SUB), (sub + 1) * (TM_FFN // FFN_SUB))
            if split_in:
                x = jnp.where(t < n_ctx_tiles, x_refs[0][rows, :], x_refs[1][rows, :])
            else:
                x = x_refs[0][rows, :]
            n = (_rms(x, ng_ref[...]) * (1.0 + m[j0 + 1:j0 + 2]) + m[j0:j0 + 1]).astype(BF16)
            for c in range(D_FF // TF_FFN):
                cols = slice(c * TF_FFN, (c + 1) * TF_FFN)
                g = _dot(n, wup_sc[:, cols])
                u = _dot(n, wup_sc[:, D_FF + c * TF_FFN:D_FF + (c + 1) * TF_FFN])
                hid_sc[rows, cols] = (g * _sigmoid(g) * u).astype(BF16)
            y = x + 0.5 * m[j0 + 2:j0 + 3] * _dot(hid_sc[rows, :], wd_sc[...])
            if final:
                y = _rms(y, fg_ref[...])
            ys.append((rows, y))
        if final:
            @pl.when(t < n_ctx_tiles)
            def _():
                for rows, y in ys:
                    o_refs[0][rows, :] = y

            @pl.when(t >= n_ctx_tiles)
            def _():
                for rows, y in ys:
                    o_refs[1][rows, :] = y
        else:
            for rows, y in ys:
                o_refs[0][rows, :] = y


def _ffn_call(xs, mods, ng, w_up, w_down, l, f, j0, final_g=None):
    split_in = len(xs) == 2
    final = final_g is not None
    n_ctx_tiles = T_CTX // TM_FFN
    tile = lambda s: jnp.maximum(s - FFN_LOAD, 0)
    chunk = lambda s: jnp.minimum(s, FFN_LOAD - 1)
    ctx_blk = lambda s: (jnp.minimum(tile(s), n_ctx_tiles - 1), 0)
    lat_blk = lambda s: (jnp.maximum(tile(s) - n_ctx_tiles, 0), 0)
    row = pl.BlockSpec((1, D_MODEL), lambda s: (0, 0))
    tok = lambda index_map: pl.BlockSpec((TM_FFN, D_MODEL), index_map)
    if split_in:
        in_specs = [tok(ctx_blk), tok(lat_blk)]
    else:
        in_specs = [tok(lambda s: (tile(s), 0))]
    in_specs += [
        pl.BlockSpec((N_GROUPS, 9, D_MODEL), lambda s: (0, 0, 0)),
        row,
        pl.BlockSpec((None, None, FFN_UP_ROWS, 2 * D_FF), lambda s: (l, f, chunk(s), 0)),
        pl.BlockSpec((None, None, FFN_DN_ROWS, D_MODEL), lambda s: (l, f, chunk(s), 0)),
    ]
    args = list(xs) + [mods, ng, w_up, w_down]
    if final:
        in_specs.append(row)
        args.append(final_g)
        out_shape = (jax.ShapeDtypeStruct((T_CTX, D_MODEL), F32), jax.ShapeDtypeStruct((T_LAT, D_MODEL), F32))
        out_specs = (tok(ctx_blk), tok(lat_blk))
    else:
        out_shape = jax.ShapeDtypeStruct((T_ALL, D_MODEL), F32)
        out_specs = tok(lambda s: (tile(s), 0))
    return pl.pallas_call(
        functools.partial(_ffn_kernel, j0=j0, split_in=split_in, final=final),
        out_shape=out_shape,
        grid=(FFN_LOAD + T_ALL // TM_FFN,),
        in_specs=in_specs,
        out_specs=out_specs,
        scratch_shapes=[pltpu.VMEM((D_MODEL, 2 * D_FF), BF16), pltpu.VMEM((D_FF, D_MODEL), BF16),
                        pltpu.VMEM((TM_FFN, D_FF), BF16)],
        compiler_params=_params(("arbitrary",)),
        name="swiglu_half_step",
    )(*args)


def _rope_partner(x):
    lane = lax.broadcasted_iota(jnp.int32, (1, HEAD_SLOT), 1)
    first = (lane % (2 * ROPE_HALF)) < ROPE_HALF
    return jnp.where(first, -pltpu.roll(x, HEAD_SLOT - ROPE_HALF, 1), pltpu.roll(x, ROPE_HALF, 1))


def _mixin_kernel(h_ref, mod_ref, ng_ref, w_ref, qg_ref, kvg_ref, wq_ref, wqs_ref, wk_ref, wv_ref,
                  fch_ref, cq_ref, sq_ref, ck_ref, sk_ref,
                  hy_ref, q_ref, k_ref, v_ref, ckv_ref, kr_ref, fcs_ref):
    i = pl.program_id(0)
    m = mod_ref[_group_of(i * TM_MIX)]
    n = (_rms(h_ref[...], ng_ref[...]) * (1.0 + m[4:5]) + m[3:4]).astype(BF16)
    proj = _dot_nt(n, w_ref[...])
    hy_ref[...] = proj[:, MIX_HY:MIX_QA]
    q_a = proj[:, MIX_QA:MIX_KVA]
    kv_a = proj[:, MIX_KVA:MIX_FN]
    fn = proj[:, MIX_FN:MIX_KR]
    k_r = proj[:, MIX_KR:MIX_W]

    qn = _rms(q_a, qg_ref[...]).astype(BF16)
    q = _dot(qn, wq_ref[...])
    q_partner = _dot(qn, wqs_ref[...])
    cos_q = cq_ref[...]
    sin_q = sq_ref[...]
    for h in range(N_HEADS):
        hs = slice(h * HEAD_SLOT, (h + 1) * HEAD_SLOT)
        q_ref[:, hs] = (q[:, hs] * cos_q + q_partner[:, hs] * sin_q).astype(BF16)

    ckv = _rms(kv_a, kvg_ref[...])

    @pl.when(i < T_CTX // TM_MIX)
    def _():
        ckv_ref[...] = ckv
        kr_ref[...] = k_r[:, :ROPE_DIM]

    ckv_b = ckv.astype(BF16)
    k_rot = k_r * ck_ref[...] + _rope_partner(k_r) * sk_ref[...]
    k_rope = pltpu.roll(k_rot, NOPE_DIM, 1)
    k_nope = _dot(ckv_b, wk_ref[...])
    for h in range(N_HEADS):
        hs = slice(h * HEAD_SLOT, (h + 1) * HEAD_SLOT)
        k_ref[:, hs] = (k_nope[:, hs] + k_rope).astype(BF16)
    _store_values(v_ref, _dot(ckv_b, wv_ref[...]))
    fcs_ref[...] = _dot(fn.astype(BF16), fch_ref[...]).astype(BF16)


def _mixin_call(h, mods, ng, w_mix, l, qg, kvg, wq, wqs, wk, wv, fch, cosq, sinq, cosk, sink):
    tm = TM_MIX

    def pos_block(i):
        tok0 = i * tm
        return jnp.where(tok0 < T_CTX, DEC_SEQ // tm, ((tok0 - T_CTX) % DEC_SEQ) // tm)

    def full(shape):
        return pl.BlockSpec(shape, lambda i: (0,) * len(shape))

    def tok(width):
        return pl.BlockSpec((tm, width), lambda i: (i, 0))

    def pos(width):
        return pl.BlockSpec((tm, width), lambda i: (pos_block(i), 0))

    out_shape = (
        jax.ShapeDtypeStruct((T_ALL, 3 * HY_W), F32),
        jax.ShapeDtypeStruct((T_ALL, QK_W), BF16),
        jax.ShapeDtypeStruct((T_ALL, QK_W), BF16),
        jax.ShapeDtypeStruct((T_ALL, V_EXT_W), BF16),
        jax.ShapeDtypeStruct((T_CTX, KV_LORA), F32),
        jax.ShapeDtypeStruct((T_CTX, ROPE_DIM), F32),
        jax.ShapeDtypeStruct((T_ALL, 2 * FN_W), BF16),
    )
    return pl.pallas_call(
        _mixin_kernel,
        out_shape=out_shape,
        grid=(T_ALL // tm,),
        in_specs=[
            tok(D_MODEL), full((N_GROUPS, 9, D_MODEL)), full((1, D_MODEL)),
            pl.BlockSpec((None, MIX_W, D_MODEL), lambda i: (l, 0, 0)),
            full((1, Q_LORA)), full((1, KV_LORA)), full((Q_LORA, QK_W)), full((Q_LORA, QK_W)),
            full((KV_LORA, QK_W)), full((KV_LORA, N_HEADS * V_DIM)),
            full((FN_W, 2 * FN_W)), pos(HEAD_SLOT), pos(HEAD_SLOT), pos(HEAD_SLOT), pos(HEAD_SLOT),
        ],
        out_specs=tuple(
            tok(s.shape[1]) if s.shape[0] == T_ALL else
            pl.BlockSpec((tm, s.shape[1]), lambda i: (jnp.minimum(i, T_CTX // tm - 1), 0))
            for s in out_shape),
        compiler_params=_params(("arbitrary",)),
        name="mixer_input_proj",
    )(h, mods, ng, w_mix, qg, kvg, wq, wqs, wk, wv, fch, cosq, sinq, cosk, sink)


def _attn_kernel(q_ref, k_ref, v_ref, *rest, has_cache, group, seq_len):
    if has_cache:
        cckv_ref, ckr_ref, wk_ref, wv_ref, ek_ref, o_ref, kc_sc, vc_sc = rest

        @pl.when(pl.program_id(1) == 0)
        def _():
            ckv_b = cckv_ref[...].astype(BF16)
            kc_sc[...] = (_dot(ckv_b, wk_ref[...]) + _dot(ckr_ref[...].astype(BF16), ek_ref[...])).astype(BF16)
            _store_values(vc_sc, _dot(ckv_b, wv_ref[...]))
    else:
        (o_ref,) = rest
    scale2 = math.log2(math.e) / math.sqrt(NOPE_DIM + ROPE_DIM)
    first_half = lax.broadcasted_iota(jnp.int32, (1, PAIR_W), 1) < V_DIM
    for sb in range(group):
        rows = slice(sb * seq_len, (sb + 1) * seq_len) if group > 1 else slice(None)
        for pair in range(N_HEADS // 2):
            vs = slice(2 * pair * PAIR_W, (2 * pair + 2) * PAIR_W)
            outs = []
            for e in range(2):
                hs = slice((2 * pair + e) * HEAD_SLOT, (2 * pair + e + 1) * HEAD_SLOT)
                qh = q_ref[rows, hs]
                s = _dot_nt(qh, k_ref[rows, hs])
                mx = jnp.max(s, axis=-1, keepdims=True)
                if has_cache:
                    sc = _dot_nt(qh, kc_sc[:, hs])
                    mx = jnp.maximum(mx, jnp.max(sc, axis=-1, keepdims=True))
                p = jnp.exp2((s - mx) * scale2)
                if has_cache:
                    o = _dot(p.astype(BF16), v_ref[rows, vs])
                    o = o + _dot(jnp.exp2((sc - mx) * scale2).astype(BF16), vc_sc[:, vs])
                    outs.append(o[:, :PAIR_W] / o[:, PAIR_W:PAIR_W + 1])
                else:
                    o = _dot(p.astype(BF16), v_ref[rows, vs.start:vs.start + PAIR_W])
                    outs.append(o / jnp.sum(p, axis=-1, keepdims=True))
            o_ref[rows, pair * PAIR_W:(pair + 1) * PAIR_W] = jnp.where(first_half, outs[0], outs[1]).astype(BF16)


def _attn_call(q, k, v, cache, *, n_seq, seq_len, tok_off, tq, group=1):
    has_cache = cache is not None
    assert group == 1 or (tq == seq_len and not has_cache)
    nq = seq_len // tq
    kv_rows = group * seq_len
    q_rows = group * tq
    kv_blk0 = tok_off // kv_rows
    q_blk0 = tok_off // q_rows
    vw = N_HEADS * V_DIM
    in_specs = [
        pl.BlockSpec((q_rows, QK_W), lambda b, j: (q_blk0 + b * nq + j, 0)),
        pl.BlockSpec((kv_rows, QK_W), lambda b, j: (kv_blk0 + b, 0)),
        pl.BlockSpec((kv_rows, V_EXT_W), lambda b, j: (kv_blk0 + b, 0)),
    ]
    args = [q, k, v]
    scratch = []
    if has_cache:
        cache_ckv, cache_krope, l, wk, wv, ek = cache
        full = lambda shape: pl.BlockSpec(shape, lambda b, j: (0,) * len(shape))
        in_specs += [pl.BlockSpec((None, None, PAST_LEN, KV_LORA), lambda b, j: (b, l, 0, 0)),
                     pl.BlockSpec((None, None, PAST_LEN, ROPE_DIM), lambda b, j: (b, l, 0, 0)),
                     full((KV_LORA, QK_W)), full((KV_LORA, vw)), full((ROPE_DIM, QK_W))]
        args += [cache_ckv, cache_krope, wk, wv, ek]
        scratch = [pltpu.VMEM((PAST_LEN, QK_W), BF16), pltpu.VMEM((PAST_LEN, V_EXT_W), BF16)]
    return pl.pallas_call(
        functools.partial(_attn_kernel, has_cache=has_cache, group=group, seq_len=seq_len),
        out_shape=jax.ShapeDtypeStruct((n_seq * seq_len, vw), BF16),
        grid=(n_seq // group, nq),
        in_specs=in_specs,
        out_specs=pl.BlockSpec((q_rows, vw), lambda b, j: (b * nq + j, 0)),
        scratch_shapes=scratch,
        compiler_params=_params(("arbitrary", "arbitrary")),
        name="mla_attention_cache" if has_cache else "mla_attention",
    )(*args)


def _hyfilt_kernel(feat_ref, w1_ref, b1_ref, w2_ref, b2_ref, w3_ref, dec_ref, alt_ref, c_ref, s_ref,
                   hre_ref, him_ref, nyq_ref, a_sc, b_sc, *, L):
    j = pl.program_id(0)
    half = HY_ORDER * HY_W

    @pl.when(j == 0)
    def _():
        h = jnp.sin(_dot_hi(feat_ref[...], w1_ref[...]) + b1_ref[...])
        h = jnp.sin(_dot_hi(h, w2_ref[...]) + b2_ref[...])
        h = _dot_x3(h, w3_ref[...])
        dec = dec_ref[...]
        rows = lax.broadcasted_iota(jnp.int32, (L, 1), 0)
        h_fwd = h[:, :half] * dec
        h_bwd = jnp.where(rows > 0, h[:, half:] * dec, 0.0)
        norm = (jnp.sum(jnp.abs(h_fwd), axis=0, keepdims=True)
                + jnp.sum(jnp.abs(h_bwd), axis=0, keepdims=True) + EPS)
        inv = 1.0 / norm
        a = (h_fwd + h_bwd) * inv
        b = (h_fwd - h_bwd) * inv
        a_sc[...] = a.astype(BF16)
        b_sc[...] = b.astype(BF16)
        nyq_ref[...] = jnp.sum(a * alt_ref[...], axis=0, keepdims=True)

    hre_ref[...] = _dot(c_ref[...], a_sc[...])
    him_ref[...] = -_dot(s_ref[...], b_sc[...])


def _hyfilt_call(L, feats, w1, b1, w2, b2, w3, decay, alt, ctab, stab):
    tk = min(L, 512)
    half = HY_ORDER * HY_W

    def full(shape):
        return pl.BlockSpec(shape, lambda j: (0,) * len(shape))

    return pl.pallas_call(
        functools.partial(_hyfilt_kernel, L=L),
        out_shape=(jax.ShapeDtypeStruct((L, half), F32), jax.ShapeDtypeStruct((L, half), F32),
                   jax.ShapeDtypeStruct((1, half), F32)),
        grid=(L // tk,),
        in_specs=[
            full((L, POS_PAD)), full((POS_PAD, FILT_HID)), full((1, FILT_HID)), full((FILT_HID, FILT_HID)),
            full((1, FILT_HID)), full((FILT_HID, 2 * half)), full((L, half)), full((L, 1)),
            pl.BlockSpec((tk, L), lambda j: (j, 0)), pl.BlockSpec((tk, L), lambda j: (j, 0)),
        ],
        out_specs=(pl.BlockSpec((tk, half), lambda j: (j, 0)), pl.BlockSpec((tk, half), lambda j: (j, 0)),
                   full((1, half))),
        scratch_shapes=[pltpu.VMEM((L, half), BF16), pltpu.VMEM((L, half), BF16)],
        compiler_params=_params(("arbitrary",)),
        name="hyena_filter_spectrum_%d" % L,
    )(feats, w1, b1, w2, b2, w3, decay, alt, ctab, stab)


def _hyconv_kernel(hy_ref, cw_ref, cb_ref, skip_ref, hre_ref, him_ref, nyq_ref, alt_ref, c_ref, s_ref, o_ref,
                   x1_sc, x2_sc, v_sc, vb_sc, yre_sc, yim_sc, nv_sc, *, n_seq, L, tk):
    p = pl.program_id(0)
    ph = pl.program_id(1)
    j = pl.program_id(2)
    cols = [slice(b * HY_W, (b + 1) * HY_W) for b in range(n_seq)]
    blk = pl.ds(pl.multiple_of(j * tk, tk), tk)

    @pl.when((p == 0) & (ph == 0) & (j == 0))
    def _():
        rows = lax.broadcasted_iota(jnp.int32, (L, 1), 0)
        w = cw_ref[...]
        for b in range(n_seq):
            x = hy_ref[b * L:(b + 1) * L, :]
            prev = jnp.where(rows > 0, pltpu.roll(x, 1, 0), 0.0)
            nxt = jnp.where(rows < L - 1, pltpu.roll(x, L - 1, 0), 0.0)
            u = prev * w[0:1] + x * w[1:2] + nxt * w[2:3] + cb_ref[...]
            x1_sc[:, cols[b]] = u[:, :HY_W]
            x2_sc[:, cols[b]] = u[:, HY_W:2 * HY_W]
            v_sc[:, cols[b]] = u[:, 2 * HY_W:]

    @pl.when((ph == 0) & (j == 0))
    def _():
        v = v_sc[...]
        vb_sc[...] = v.astype(BF16)
        nv_sc[...] = jnp.sum(v * alt_ref[...], axis=0, keepdims=True)

    @pl.when(ph == 0)
    def _():
        vb = vb_sc[...]
        v_re = _dot(c_ref[...], vb)
        v_im = -_dot(s_ref[...], vb)
        freq = j * tk + lax.broadcasted_iota(jnp.int32, (tk, 1), 0)
        wk = jnp.where(freq == 0, 0.5, 1.0)
        h_re = hre_ref[...] * wk
        h_im = him_ref[...] * wk
        for b in range(n_seq):
            yre_sc[blk, cols[b]] = (v_re[:, cols[b]] * h_re - v_im[:, cols[b]] * h_im).astype(BF16)
            yim_sc[blk, cols[b]] = (v_re[:, cols[b]] * h_im + v_im[:, cols[b]] * h_re).astype(BF16)

    @pl.when(ph == 1)
    def _():
        acc = _dot(c_ref[...], yre_sc[...]) - _dot(s_ref[...], yim_sc[...])
        alt = alt_ref[blk, :]
        for b in range(n_seq):
            v = v_sc[blk, cols[b]]
            nyq = nv_sc[:, cols[b]] * nyq_ref[...]
            y = acc[:, cols[b]] * (1.0 / L) + (0.5 / L) * alt * nyq + v * skip_ref[...]

            @pl.when(p == 0)
            def _():
                v_sc[blk, cols[b]] = x1_sc[blk, cols[b]] * y

            @pl.when(p == 1)
            def _():
                o_ref[pl.ds(pl.multiple_of(b * L + j * tk, tk), tk), :] = (x2_sc[blk, cols[b]] * y).astype(BF16)


def _hyconv_call(hy, conv_w, conv_b, skip, hre, him, nyq, alt, ctab, stab, *, n_seq, L, tok_off):
    tk = min(L, 512)
    nk = L // tk
    seg = tok_off // (n_seq * L)
    width = n_seq * HY_W

    def full(shape):
        return pl.BlockSpec(shape, lambda p, ph, j: (0,) * len(shape))

    def filt(p, ph, j):
        return (jnp.where(ph == 0, j, nk - 1), p)

    vm = lambda dtype: pltpu.VMEM((L, width), dtype)
    return pl.pallas_call(
        functools.partial(_hyconv_kernel, n_seq=n_seq, L=L, tk=tk),
        out_shape=jax.ShapeDtypeStruct((n_seq * L, HY_W), BF16),
        grid=(HY_ORDER, 2, nk),
        in_specs=[
            pl.BlockSpec((n_seq * L, 3 * HY_W), lambda p, ph, j: (seg, 0), pipeline_mode=pl.Buffered(1)),
            full((3, 3 * HY_W)), full((1, 3 * HY_W)),
            pl.BlockSpec((None, 1, HY_W), lambda p, ph, j: (p, 0, 0)),
            pl.BlockSpec((tk, HY_W), filt), pl.BlockSpec((tk, HY_W), filt),
            pl.BlockSpec((None, 1, HY_W), lambda p, ph, j: (p, 0, 0)),
            full((L, 1)),
            pl.BlockSpec((tk, L), lambda p, ph, j: (j, 0)), pl.BlockSpec((tk, L), lambda p, ph, j: (j, 0)),
        ],
        out_specs=pl.BlockSpec((n_seq * L, HY_W), lambda p, ph, j: (0, 0)),
        scratch_shapes=[vm(F32), vm(F32), vm(F32), vm(BF16), vm(BF16), vm(BF16), pltpu.VMEM((1, width), F32)],
        compiler_params=_params(("arbitrary", "arbitrary", "arbitrary")),
        name="hyena_long_conv_%d" % L,
    )(hy, conv_w, conv_b, skip, hre, him, nyq, alt, ctab, stab)


def _fnet_kernel(fcs_ref, c_ref, s_ref, o_ref, fc_sc, fs_sc, *, n_seq, L, tk, scale):
    j = pl.program_id(0)
    cols = [slice(b * FN_W, (b + 1) * FN_W) for b in range(n_seq)]

    @pl.when(j == 0)
    def _():
        for b in range(n_seq):
            fc_sc[:, cols[b]] = fcs_ref[b * L:(b + 1) * L, :FN_W]
            fs_sc[:, cols[b]] = fcs_ref[b * L:(b + 1) * L, FN_W:]

    y = (_dot(c_ref[...], fc_sc[...]) - _dot(s_ref[...], fs_sc[...])) * scale
    for b in range(n_seq):
        o_ref[pl.ds(pl.multiple_of(b * L + j * tk, tk), tk), :] = y[:, cols[b]].astype(BF16)


def _fnet_call(fcs, ctab, stab, *, n_seq, L, tok_off):
    tk = min(L, 512)
    seg = tok_off // (n_seq * L)
    vm = lambda: pltpu.VMEM((L, n_seq * FN_W), BF16)
    return pl.pallas_call(
        functools.partial(_fnet_kernel, n_seq=n_seq, L=L, tk=tk, scale=1.0 / math.sqrt(L * FN_GROUP_W)),
        out_shape=jax.ShapeDtypeStruct((n_seq * L, FN_W), BF16),
        grid=(L // tk,),
        in_specs=[
            pl.BlockSpec((n_seq * L, 2 * FN_W), lambda j: (seg, 0)),
            pl.BlockSpec((tk, L), lambda j: (j, 0)), pl.BlockSpec((tk, L), lambda j: (j, 0)),
        ],
        out_specs=pl.BlockSpec((n_seq * L, FN_W), lambda j: (0, 0)),
        scratch_shapes=[vm(), vm()],
        compiler_params=_params(("arbitrary",)),
        name="fnet_position_dft_%d" % L,
    )(fcs, ctab, stab)


def _mixout_kernel(h_ref, mod_ref, ng_ref, wg_ref, zc_ref, zl_ref, ac_ref, al_ref, fc_ref, fl_ref,
                   wa_ref, wb_ref, wc_ref, wo_ref, o_ref):
    i = pl.program_id(0)
    is_ctx = i < T_CTX // TM_MIX
    m = mod_ref[_group_of(i * TM_MIX)]
    h = h_ref[...]
    n = (_rms(h, ng_ref[...]) * (1.0 + m[4:5]) + m[3:4]).astype(BF16)
    z = jnp.where(is_ctx, zc_ref[...], zl_ref[...])
    a = jnp.where(is_ctx, ac_ref[...], al_ref[...])
    f = jnp.where(is_ctx, fc_ref[...], fl_ref[...])
    acc = _sigmoid(_dot_nt(n, wg_ref[0:D_MODEL, :])) * _dot(z, wa_ref[...])
    acc = acc + _sigmoid(_dot_nt(n, wg_ref[D_MODEL:2 * D_MODEL, :])) * _dot(a, wb_ref[...])
    acc = acc + _sigmoid(_dot_nt(n, wg_ref[2 * D_MODEL:, :])) * _dot(f, wc_ref[...])
    y = _dot(acc.astype(BF16), wo_ref[...])
    o_ref[...] = h + m[5:6] * y


def _mixout_call(h, mods, ng, w_gate, l, z, a, f, wa, wb, wc, wo):
    tm = TM_MIX
    n_ctx = T_CTX // tm

    def full(shape):
        return pl.BlockSpec(shape, lambda i: (0,) * len(shape))

    def tok(width):
        return pl.BlockSpec((tm, width), lambda i: (i, 0))

    def pair(width):
        return [pl.BlockSpec((tm, width), lambda i: (jnp.minimum(i, n_ctx - 1), 0)),
                pl.BlockSpec((tm, width), lambda i: (jnp.maximum(i - n_ctx, 0), 0))]

    return pl.pallas_call(
        _mixout_kernel,
        out_shape=jax.ShapeDtypeStruct((T_ALL, D_MODEL), F32),
        grid=(T_ALL // tm,),
        in_specs=[
            tok(D_MODEL), full((N_GROUPS, 9, D_MODEL)), full((1, D_MODEL)),
            pl.BlockSpec((None, N_BRANCH * D_MODEL, D_MODEL), lambda i: (l, 0, 0)),
            *pair(HY_W), *pair(N_HEADS * V_DIM), *pair(FN_W),
            full((HY_W, D_MODEL)), full((N_HEADS * V_DIM, D_MODEL)), full((FN_W, D_MODEL)),
            full((D_MODEL, D_MODEL)),
        ],
        out_specs=tok(D_MODEL),
        compiler_params=_params(("parallel",)),
        name="gated_merge_out_proj",
    )(h, mods, ng, w_gate, *z, *a, *f, wa, wb, wc, wo)


def _wprep_kernel(w_ref, mix_ref, gate_ref):
    r = pl.program_id(1)
    n_gate = N_BRANCH * D_MODEL
    g0 = IN_GATE - PREP_ROWS
    last = n_gate - (3 * PREP_ROWS - IN_GATE)

    @pl.when(r == 0)
    def _():
        mix_ref[:MIX_FN, :] = w_ref[...].astype(BF16)

    @pl.when(r == 1)
    def _():
        mix_ref[MIX_FN:MIX_KR, :] = w_ref[ROPE_DIM:g0, :].astype(BF16)
        mix_ref[MIX_KR:MIX_KR + ROPE_DIM, :] = w_ref[0:ROPE_DIM, :].astype(BF16)
        mix_ref[MIX_KR + ROPE_DIM:, :] = jnp.zeros((MIX_W - MIX_KR - ROPE_DIM, D_MODEL), BF16)
        gate_ref[:PREP_ROWS - g0, :] = w_ref[g0:, :].astype(BF16)

    @pl.when(r == 2)
    def _():
        gate_ref[PREP_ROWS - g0:2 * PREP_ROWS - g0, :] = w_ref[...].astype(BF16)

    @pl.when(r == 3)
    def _():
        gate_ref[2 * PREP_ROWS - g0:, :] = w_ref[:last, :].astype(BF16)


def _wprep_call(w_in_t):
    assert IN_KR == PREP_ROWS and 4 * PREP_ROWS >= IN_COLS
    return pl.pallas_call(
        _wprep_kernel,
        out_shape=(jax.ShapeDtypeStruct((DEPTH, MIX_W, D_MODEL), BF16),
                   jax.ShapeDtypeStruct((DEPTH, N_BRANCH * D_MODEL, D_MODEL), BF16)),
        grid=(DEPTH, 4),
        in_specs=[pl.BlockSpec((None, PREP_ROWS, D_MODEL), lambda l, r: (l, r, 0))],
        out_specs=(pl.BlockSpec((None, MIX_W, D_MODEL), lambda l, r: (l, 0, 0)),
                   pl.BlockSpec((None, N_BRANCH * D_MODEL, D_MODEL), lambda l, r: (l, 0, 0))),
        compiler_params=_params(("arbitrary", "arbitrary")),
        name="input_proj_weight_relayout",
    )(w_in_t)


def _layer_weights(l, w_qb, w_kvb):
    wq3 = w_qb[l].reshape(Q_LORA, N_HEADS, NOPE_DIM + ROPE_DIM)
    slot_pad = HEAD_SLOT - NOPE_DIM - ROPE_DIM
    wq = jnp.pad(wq3, ((0, 0), (0, 0), (0, slot_pad))).reshape(Q_LORA, QK_W).astype(BF16)
    rope = wq3[:, :, NOPE_DIM:].reshape(Q_LORA, N_HEADS, 2, 2, ROPE_HALF)
    partner = jnp.stack([-rope[:, :, :, 1], rope[:, :, :, 0]], axis=3).reshape(Q_LORA, N_HEADS, ROPE_DIM)
    wqs = jnp.pad(partner, ((0, 0), (0, 0), (NOPE_DIM, slot_pad))).reshape(Q_LORA, QK_W).astype(BF16)

    wkv3 = w_kvb[l].reshape(KV_LORA, N_HEADS, NOPE_DIM + V_DIM)
    wk = jnp.pad(wkv3[:, :, :NOPE_DIM], ((0, 0), (0, 0), (0, HEAD_SLOT - NOPE_DIM))).reshape(KV_LORA, QK_W).astype(BF16)
    wv = wkv3[:, :, NOPE_DIM:].reshape(KV_LORA, N_HEADS * V_DIM).astype(BF16)
    return wq, wqs, wk, wv


def kernel(x_prompt, x_sample, cache_ckv, cache_krope, c, c_ctx, w_ada, b_ada, norm_g, w_ffn_up, w_ffn_down,
           w_in, hy_conv_w, hy_conv_b, hy_filt_w1, hy_filt_b1, hy_filt_w2, hy_filt_b2, hy_filt_w3, hy_skip,
           w_hy_out, q_norm_g, w_qb, kv_norm_g, w_kvb, w_mla_o, w_fnet, w_out, final_g):
    tabs = _tables()
    cosq, sinq, cosk, sink = (jnp.asarray(t) for t in tabs["rope"])
    fch = jnp.asarray(tabs["fnch"]).astype(BF16)
    ek = jnp.asarray(tabs["ropeexp"]).astype(BF16)
    dft = {}
    for L in (SEQ, DEC_SEQ):
        dft[("hy", L)] = tuple(jnp.asarray(t).astype(BF16) for t in tabs[("hy", L)])
        dft[("fn", L)] = tuple(jnp.asarray(t).astype(BF16) for t in tabs[("fn", L)])

    cvec = jnp.concatenate([c_ctx[None, :], c, jnp.zeros((8 - N_GROUPS, D_MODEL), F32)], axis=0).T
    ada = _ada_call(cvec, w_ada, b_ada)
    mods_all = ada[:, :N_GROUPS].reshape(DEPTH, N_GROUPS, 9, D_MODEL)

    w_mix, w_gate = _wprep_call(jnp.swapaxes(w_in, 1, 2))
    w_hy_out, w_mla_o, w_fnet, w_out = (w.astype(BF16) for w in (w_hy_out, w_mla_o, w_fnet, w_out))
    hs = (x_prompt.reshape(T_CTX, D_MODEL), x_sample.reshape(T_LAT, D_MODEL))
    segs = ((BATCH, SEQ, 0), (DEC_BATCH, DEC_SEQ, T_CTX))
    ckv_out = []
    kr_out = []
    for l in range(DEPTH):
        mods = mods_all[l]
        ng = norm_g[l]
        wq, wqs, wk, wv = _layer_weights(l, w_qb, w_kvb)

        h = _ffn_call(hs, mods, ng[0:1], w_ffn_up, w_ffn_down, l, 0, 0)

        hy, q, k, v, ckv, k_r, fcs = _mixin_call(
            h, mods, ng[1:2], w_mix, l, q_norm_g[l][None, :], kv_norm_g[l][None, :], wq, wqs, wk, wv, fch,
            cosq, sinq, cosk, sink)
        ckv_out.append(ckv.reshape(BATCH, SEQ, KV_LORA))
        kr_out.append(k_r.reshape(BATCH, SEQ, ROPE_DIM))

        w1 = jnp.pad(hy_filt_w1[l], ((0, POS_PAD - POS_EMB), (0, 0)))
        skip = hy_skip[l].reshape(HY_ORDER, 1, HY_W)
        z_parts, a_parts, f_parts = [], [], []
        for n_seq, L, off in segs:
            feats, decay, alt = (jnp.asarray(t) for t in tabs[("filt", L)])
            c_hy, s_hy = dft[("hy", L)]
            hre, him, nyq = _hyfilt_call(L, feats, w1, hy_filt_b1[l][None, :], hy_filt_w2[l],
                                         hy_filt_b2[l][None, :], hy_filt_w3[l], decay, alt, c_hy, s_hy)
            z_parts.append(_hyconv_call(hy, hy_conv_w[l], hy_conv_b[l][None, :], skip, hre, him,
                                        nyq.reshape(HY_ORDER, 1, HY_W), alt, c_hy, s_hy,
                                        n_seq=n_seq, L=L, tok_off=off))
            if off:
                a_parts.append(_attn_call(q, k, v, (cache_ckv, cache_krope, l, wk, wv, ek),
                                          n_seq=n_seq, seq_len=L, tok_off=off, tq=512))
            else:
                a_parts.append(_attn_call(q, k, v, None, n_seq=n_seq, seq_len=L, tok_off=off, tq=L, group=4))
            c_fn, s_fn = dft[("fn", L)]
            f_parts.append(_fnet_call(fcs, c_fn, s_fn, n_seq=n_seq, L=L, tok_off=off))
        h = _mixout_call(h, mods, ng[1:2], w_gate, l, z_parts, a_parts, f_parts,
                         w_hy_out[l], w_mla_o[l], w_fnet[l], w_out[l])

        last = l == DEPTH - 1
        out = _ffn_call((h,), mods, ng[2:3], w_ffn_up, w_ffn_down, l, 1, 6, final_g[None, :] if last else None)
        hs = out if last else (out,)

    y_prompt = hs[0].reshape(BATCH, SEQ, D_MODEL)
    y_sample = hs[1].reshape(DEC_BATCH, DEC_SEQ, D_MODEL)
    return y_prompt, y_sample, jnp.stack(ckv_out, axis=1), jnp.stack(kr_out, axis=1)
```

```python
import functools
import math

import numpy as np
import jax
import jax.numpy as jnp
from jax import lax
from jax.experimental import pallas as pl
from jax.experimental.pallas import tpu as pltpu

F32 = jnp.float32
BF16 = jnp.bfloat16
HIGHEST = lax.Precision.HIGHEST

D_MODEL = 1024
BATCH = 16
SEQ = 256
DEPTH = 2
DEC_BATCH = 2
DEC_SEQ = 2048
PAST_LEN = 512
GRID_W = 64
HY_W = 256
HY_ORDER = 2
N_BANDS = 8
POS_EMB = 1 + 2 * N_BANDS
FILT_HID = 64
N_HEADS = 8
Q_LORA = 256
KV_LORA = 128
NOPE_DIM = 64
ROPE_DIM = 32
V_DIM = 64
ROPE_BASE = 10000.0
FN_GROUPS = 4
FN_GROUP_W = 64
FN_W = FN_GROUPS * FN_GROUP_W
N_BRANCH = 3
D_FF = 2816
EPS = 1e-6

T_CTX = BATCH * SEQ
T_LAT = DEC_BATCH * DEC_SEQ
T_ALL = T_CTX + T_LAT
N_GROUPS = 1 + DEC_BATCH
HEAD_SLOT = 128
QK_W = N_HEADS * HEAD_SLOT
PAIR_W = 2 * V_DIM
V_EXT_W = N_HEADS * PAIR_W
POS_PAD = 128

VMEM_LIMIT = 56 * 1024 * 1024

TM_FFN = 512
TF_FFN = 256
FFN_LOAD = 8
FFN_UP_ROWS = D_MODEL // FFN_LOAD
FFN_DN_ROWS = D_FF // FFN_LOAD
FFN_SUB = 2
TM_MIX = 512
MIX_SUB = 2
ADA_TN = 4608
ATTN_TQ = 512
ATTN_GROUP = 8
PREP_ROWS = 1152

IN_KR = 3 * HY_W + Q_LORA + KV_LORA
IN_FN = IN_KR + ROPE_DIM
IN_GATE = IN_FN + FN_W
IN_COLS = IN_GATE + N_BRANCH * D_MODEL

MIX_HY = 0
MIX_QA = 3 * HY_W
MIX_KVA = MIX_QA + Q_LORA
MIX_FN = MIX_KVA + KV_LORA
MIX_KR = MIX_FN + FN_W
MIX_W = MIX_KR + 128
ROPE_HALF = ROPE_DIM // 4


def _dft_tables(L, half):
    k = np.arange(L, dtype=np.int64)
    period = 2 * L if half else L
    m = (k[:, None] * k[None, :]) % period
    ang = 2.0 * np.pi * m.astype(np.float64) / period
    return np.cos(ang).astype(np.float32), np.sin(ang).astype(np.float32)


def _filter_tables(L):
    t = np.arange(L, dtype=np.float64)
    t_norm = t / (L - 1)
    w = 2.0 * np.pi * t / L
    bands = np.linspace(1e-4, N_BANDS - 1, N_BANDS)
    ang = w[:, None] * bands[None, :]
    feats = np.concatenate([t_norm[:, None], np.cos(ang), -np.sin(ang)], axis=-1)
    feats = np.pad(feats, ((0, 0), (0, POS_PAD - POS_EMB)))
    deltas = np.linspace(math.log(1e-2) / 1.5, math.log(1e-2) / 0.3, HY_W)
    decay = np.exp(-t_norm[:, None] * np.abs(deltas)[None, :])
    decay = np.concatenate([decay, decay], axis=1)
    alt = np.where(np.arange(L) % 2 == 0, 1.0, -1.0)[:, None]
    return feats.astype(np.float32), decay.astype(np.float32), alt.astype(np.float32)


def _rope_tables():
    t = np.arange(DEC_SEQ)
    row = (t // GRID_W).astype(np.float64)
    col = (t % GRID_W).astype(np.float64)
    nf = ROPE_DIM // 4
    inv = ROPE_BASE ** (-np.arange(nf, dtype=np.float64) / nf)
    ar = row[:, None] * inv[None, :]
    ac = col[:, None] * inv[None, :]
    cos32 = np.concatenate([np.cos(ar), np.cos(ar), np.cos(ac), np.cos(ac)], axis=1)
    sin32 = np.concatenate([np.sin(ar), np.sin(ar), np.sin(ac), np.sin(ac)], axis=1)
    cos32 = np.concatenate([cos32, np.ones((TM_MIX, ROPE_DIM))], axis=0)
    sin32 = np.concatenate([sin32, np.zeros((TM_MIX, ROPE_DIM))], axis=0)
    n = cos32.shape[0]
    cosq = np.ones((n, HEAD_SLOT))
    sinq = np.zeros((n, HEAD_SLOT))
    cosq[:, NOPE_DIM:NOPE_DIM + ROPE_DIM] = cos32
    sinq[:, NOPE_DIM:NOPE_DIM + ROPE_DIM] = sin32
    cosk = np.zeros((n, HEAD_SLOT))
    sink = np.zeros((n, HEAD_SLOT))
    cosk[:, :ROPE_DIM] = cos32
    sink[:, :ROPE_DIM] = sin32
    return tuple(t.astype(np.float32) for t in (cosq, sinq, cosk, sink))


def _fnet_channel_table():
    j = np.arange(FN_GROUP_W)
    ang = 2.0 * np.pi * ((j[:, None] * j[None, :]) % FN_GROUP_W) / FN_GROUP_W
    out = np.zeros((FN_W, 2 * FN_W))
    for g in range(FN_GROUPS):
        sl = slice(g * FN_GROUP_W, (g + 1) * FN_GROUP_W)
        out[sl, sl] = np.cos(ang)
        out[sl, FN_W + g * FN_GROUP_W:FN_W + (g + 1) * FN_GROUP_W] = np.sin(ang)
    return out.astype(np.float32)


def _rope_expand_table():
    e = np.zeros((ROPE_DIM, N_HEADS, HEAD_SLOT), np.float32)
    for j in range(ROPE_DIM):
        e[j, :, NOPE_DIM + j] = 1.0
    return e.reshape(ROPE_DIM, QK_W)


_TABLES = {}


def _tables():
    if not _TABLES:
        for L in (SEQ, DEC_SEQ):
            _TABLES[("hy", L)] = _dft_tables(L, True)
            _TABLES[("fn", L)] = _dft_tables(L, False)
            _TABLES[("filt", L)] = _filter_tables(L)
        _TABLES["rope"] = _rope_tables()
        _TABLES["fnch"] = _fnet_channel_table()
        _TABLES["ropeexp"] = _rope_expand_table()
    return _TABLES


def _rms(x, g):
    ms = jnp.mean(x * x, axis=-1, keepdims=True)
    return x * lax.rsqrt(ms + EPS) * g


def _sigmoid(x):
    return 1.0 / (1.0 + jnp.exp(-x))


def _dot(a, b):
    return jnp.dot(a, b, preferred_element_type=F32)


def _dot_hi(a, b):
    return jnp.dot(a, b, precision=HIGHEST, preferred_element_type=F32)


def _dot_x3(a, b):
    a_hi = a.astype(BF16)
    b_hi = b.astype(BF16)
    a_lo = (a - a_hi.astype(F32)).astype(BF16)
    b_lo = (b - b_hi.astype(F32)).astype(BF16)
    return _dot(a_hi, b_hi) + (_dot(a_hi, b_lo) + _dot(a_lo, b_hi))


def _dot_nt(a, b):
    return lax.dot_general(a, b, (((1,), (1,)), ((), ())), preferred_element_type=F32)


def _store_values(v_ref, v):
    ones = jnp.ones((v.shape[0], PAIR_W), BF16)
    for pair in range(N_HEADS // 2):
        v_ref[:, 2 * pair * PAIR_W:(2 * pair + 1) * PAIR_W] = v[:, pair * PAIR_W:(pair + 1) * PAIR_W].astype(BF16)
        v_ref[:, (2 * pair + 1) * PAIR_W:(2 * pair + 2) * PAIR_W] = ones


def _group_of(tok0):
    return jnp.where(tok0 < T_CTX, 0, 1 + (tok0 - T_CTX) // DEC_SEQ)


def _slab(shape, *lead):
    tail = tuple(shape[len(lead):])
    return pl.BlockSpec((None,) * len(lead) + tail, lambda *_: tuple(lead) + (0,) * len(tail))


def _params(sem):
    return pltpu.CompilerParams(dimension_semantics=sem, vmem_limit_bytes=VMEM_LIMIT)


def _ada_kernel(c_ref, w_ref, b_ref, o_ref):
    x = c_ref[...]
    s = x * _sigmoid(x)
    w = w_ref[...]
    o_ref[...] = jnp.zeros_like(o_ref)
    for g in range(N_GROUPS):
        o_ref[g:g + 1, :] = jnp.sum(w * s[:, g:g + 1], axis=0, keepdims=True) + b_ref[...]


def _ada_call(cvec, w_ada, b_ada):
    tn = ADA_TN
    n_out = 9 * D_MODEL
    return pl.pallas_call(
        _ada_kernel,
        out_shape=jax.ShapeDtypeStruct((DEPTH, 8, n_out), F32),
        grid=(DEPTH, n_out // tn),
        in_specs=[
            pl.BlockSpec((D_MODEL, 8), lambda l, j: (0, 0)),
            pl.BlockSpec((None, D_MODEL, tn), lambda l, j: (l, 0, j)),
            pl.BlockSpec((None, 1, tn), lambda l, j: (l, 0, j)),
        ],
        out_specs=pl.BlockSpec((None, 8, tn), lambda l, j: (l, 0, j)),
        compiler_params=_params(("arbitrary", "arbitrary")),
        name="ada_modulation",
    )(cvec, w_ada, b_ada.reshape(DEPTH, 1, n_out))


def _ffn_kernel(*refs, j0, split_in, final):
    refs = list(refs)
    x_refs = [refs.pop(0) for _ in range(2 if split_in else 1)]
    mod_ref, ng_ref, wup_ref, wd_ref = refs[:4]
    refs = refs[4:]
    fg_ref = refs.pop(0) if final else None
    o_refs = [refs.pop(0) for _ in range(2 if final else 1)]
    wup_sc, wd_sc, hid_sc = refs
    s = pl.program_id(0)
    n_ctx_tiles = T_CTX // TM_FFN

    @pl.when(s < FFN_LOAD)
    def _():
        wup_sc[pl.ds(pl.multiple_of(s * FFN_UP_ROWS, FFN_UP_ROWS), FFN_UP_ROWS), :] = wup_ref[...].astype(BF16)
        wd_sc[pl.ds(pl.multiple_of(s * FFN_DN_ROWS, FFN_DN_ROWS), FFN_DN_ROWS), :] = wd_ref[...].astype(BF16)

    @pl.when(s >= FFN_LOAD)
    def _():
        t = s - FFN_LOAD
        m = mod_ref[_group_of(t * TM_FFN)]
        ys = []
        for sub in range(FFN_SUB):
            rows = slice(sub * (TM_FFN // FFN_SUB), (sub + 1) * (TM_FFN // FFN_SUB))
            if split_in:
                x = jnp.where(t < n_ctx_tiles, x_refs[0][rows, :], x_refs[1][rows, :])
            else:
                x = x_refs[0][rows, :]
            n = (_rms(x, ng_ref[...]) * (1.0 + m[j0 + 1:j0 + 2]) + m[j0:j0 + 1]).astype(BF16)
            for c in range(D_FF // TF_FFN):
                cols = slice(c * TF_FFN, (c + 1) * TF_FFN)
                g = _dot(n, wup_sc[:, cols])
                u = _dot(n, wup_sc[:, D_FF + c * TF_FFN:D_FF + (c + 1) * TF_FFN])
                hid_sc[rows, cols] = (g * _sigmoid(g) * u).astype(BF16)
            y = x + 0.5 * m[j0 + 2:j0 + 3] * _dot(hid_sc[rows, :], wd_sc[...])
            if final:
                y = _rms(y, fg_ref[...])
            ys.append((rows, y))
        if final:
            @pl.when(t < n_ctx_tiles)
            def _():
                for rows, y in ys:
                    o_refs[0][rows, :] = y

            @pl.when(t >= n_ctx_tiles)
            def _():
                for rows, y in ys:
                    o_refs[1][rows, :] = y
        else:
            for rows, y in ys:
                o_refs[0][rows, :] = y


def _ffn_call(xs, mods, ng, w_up, w_down, l, f, j0, final_g=None):
    split_in = len(xs) == 2
    final = final_g is not None
    n_ctx_tiles = T_CTX // TM_FFN
    tile = lambda s: jnp.maximum(s - FFN_LOAD, 0)
    chunk = lambda s: jnp.minimum(s, FFN_LOAD - 1)
    ctx_blk = lambda s: (jnp.minimum(tile(s), n_ctx_tiles - 1), 0)
    lat_blk = lambda s: (jnp.maximum(tile(s) - n_ctx_tiles, 0), 0)
    row = pl.BlockSpec((1, D_MODEL), lambda s: (0, 0))
    tok = lambda index_map: pl.BlockSpec((TM_FFN, D_MODEL), index_map)
    if split_in:
        in_specs = [tok(ctx_blk), tok(lat_blk)]
    else:
        in_specs = [tok(lambda s: (tile(s), 0))]
    in_specs += [
        _slab(mods.shape, l), _slab(ng.shape, l, j0 // 3),
        pl.BlockSpec((None, None, FFN_UP_ROWS, 2 * D_FF), lambda s: (l, f, chunk(s), 0)),
        pl.BlockSpec((None, None, FFN_DN_ROWS, D_MODEL), lambda s: (l, f, chunk(s), 0)),
    ]
    args = list(xs) + [mods, ng, w_up, w_down]
    if final:
        in_specs.append(row)
        args.append(final_g)
        out_shape = (jax.ShapeDtypeStruct((T_CTX, D_MODEL), F32), jax.ShapeDtypeStruct((T_LAT, D_MODEL), F32))
        out_specs = (tok(ctx_blk), tok(lat_blk))
    else:
        out_shape = jax.ShapeDtypeStruct((T_ALL, D_MODEL), F32)
        out_specs = tok(lambda s: (tile(s), 0))
    return pl.pallas_call(
        functools.partial(_ffn_kernel, j0=j0, split_in=split_in, final=final),
        out_shape=out_shape,
        grid=(FFN_LOAD + T_ALL // TM_FFN,),
        in_specs=in_specs,
        out_specs=out_specs,
        scratch_shapes=[pltpu.VMEM((D_MODEL, 2 * D_FF), BF16), pltpu.VMEM((D_FF, D_MODEL), BF16),
                        pltpu.VMEM((TM_FFN, D_FF), BF16)],
        compiler_params=_params(("arbitrary",)),
        name="swiglu_half_step",
    )(*args)


def _rope_partner(x):
    lane = lax.broadcasted_iota(jnp.int32, (1, HEAD_SLOT), 1)
    first = (lane % (2 * ROPE_HALF)) < ROPE_HALF
    return jnp.where(first, -pltpu.roll(x, HEAD_SLOT - ROPE_HALF, 1), pltpu.roll(x, ROPE_HALF, 1))


def _mixin_kernel(h_ref, mod_ref, ng_ref, w_ref, qg_ref, kvg_ref, wq_ref, wqs_ref, wk_ref, wv_ref,
                  fch_ref, cq_ref, sq_ref, ck_ref, sk_ref,
                  hy_ref, q_ref, k_ref, v_ref, ckv_ref, kr_ref, fcs_ref):
    i = pl.program_id(0)
    m = mod_ref[_group_of(i * TM_MIX)]
    n = (_rms(h_ref[...], ng_ref[...]) * (1.0 + m[4:5]) + m[3:4]).astype(BF16)
    proj = _dot_nt(n, w_ref[...])
    hy_ref[...] = proj[:, MIX_HY:MIX_QA]
    q_a = proj[:, MIX_QA:MIX_KVA]
    kv_a = proj[:, MIX_KVA:MIX_FN]
    fn = proj[:, MIX_FN:MIX_KR]
    k_r = proj[:, MIX_KR:MIX_W]

    qn = _rms(q_a, qg_ref[...]).astype(BF16)
    q = _dot(qn, wq_ref[...])
    q_partner = _dot(qn, wqs_ref[...])
    cos_q = cq_ref[...]
    sin_q = sq_ref[...]
    for h in range(N_HEADS):
        hs = slice(h * HEAD_SLOT, (h + 1) * HEAD_SLOT)
        q_ref[:, hs] = (q[:, hs] * cos_q + q_partner[:, hs] * sin_q).astype(BF16)

    ckv = _rms(kv_a, kvg_ref[...])

    @pl.when(i < T_CTX // TM_MIX)
    def _():
        ckv_ref[...] = ckv
        kr_ref[...] = k_r[:, :ROPE_DIM]

    ckv_b = ckv.astype(BF16)
    k_rot = k_r * ck_ref[...] + _rope_partner(k_r) * sk_ref[...]
    k_rope = pltpu.roll(k_rot, NOPE_DIM, 1)
    k_nope = _dot(ckv_b, wk_ref[...])
    for h in range(N_HEADS):
        hs = slice(h * HEAD_SLOT, (h + 1) * HEAD_SLOT)
        k_ref[:, hs] = (k_nope[:, hs] + k_rope).astype(BF16)
    _store_values(v_ref, _dot(ckv_b, wv_ref[...]))
    fcs_ref[...] = _dot(fn.astype(BF16), fch_ref[...]).astype(BF16)


def _mixin_call(h, mods, ng, w_mix, l, qg, kvg, wq, wqs, wk, wv, fch, cosq, sinq, cosk, sink):
    tm = TM_MIX

    def pos_block(i):
        tok0 = i * tm
        return jnp.where(tok0 < T_CTX, DEC_SEQ // tm, ((tok0 - T_CTX) % DEC_SEQ) // tm)

    def full(shape):
        return pl.BlockSpec(shape, lambda i: (0,) * len(shape))

    def tok(width):
        return pl.BlockSpec((tm, width), lambda i: (i, 0))

    def pos(width):
        return pl.BlockSpec((tm, width), lambda i: (pos_block(i), 0))

    out_shape = (
        jax.ShapeDtypeStruct((T_ALL, 3 * HY_W), F32),
        jax.ShapeDtypeStruct((T_ALL, QK_W), BF16),
        jax.ShapeDtypeStruct((T_ALL, QK_W), BF16),
        jax.ShapeDtypeStruct((T_ALL, V_EXT_W), BF16),
        jax.ShapeDtypeStruct((T_CTX, KV_LORA), F32),
        jax.ShapeDtypeStruct((T_CTX, ROPE_DIM), F32),
        jax.ShapeDtypeStruct((T_ALL, 2 * FN_W), BF16),
    )
    return pl.pallas_call(
        _mixin_kernel,
        out_shape=out_shape,
        grid=(T_ALL // tm,),
        in_specs=[
            tok(D_MODEL), _slab(mods.shape, l), _slab(ng.shape, l, 1), _slab(w_mix.shape, l),
            _slab(qg.shape, l), _slab(kvg.shape, l), _slab(wq.shape, l), _slab(wqs.shape, l),
            _slab(wk.shape, l), _slab(wv.shape, l),
            full((FN_W, 2 * FN_W)), pos(HEAD_SLOT), pos(HEAD_SLOT), pos(HEAD_SLOT), pos(HEAD_SLOT),
        ],
        out_specs=tuple(
            tok(s.shape[1]) if s.shape[0] == T_ALL else
            pl.BlockSpec((tm, s.shape[1]), lambda i: (jnp.minimum(i, T_CTX // tm - 1), 0))
            for s in out_shape),
        compiler_params=_params(("arbitrary",)),
        name="mixer_input_proj",
    )(h, mods, ng, w_mix, qg, kvg, wq, wqs, wk, wv, fch, cosq, sinq, cosk, sink)


def _attn_kernel(q_ref, k_ref, v_ref, *rest, has_cache, group, seq_len):
    if has_cache:
        cckv_ref, ckr_ref, wk_ref, wv_ref, ek_ref, o_ref, kc_sc, vc_sc = rest

        @pl.when(pl.program_id(1) == 0)
        def _():
            ckv_b = cckv_ref[...].astype(BF16)
            kc_sc[...] = (_dot(ckv_b, wk_ref[...]) + _dot(ckr_ref[...].astype(BF16), ek_ref[...])).astype(BF16)
            _store_values(vc_sc, _dot(ckv_b, wv_ref[...]))
    else:
        (o_ref,) = rest
    scale2 = math.log2(math.e) / math.sqrt(NOPE_DIM + ROPE_DIM)
    first_half = lax.broadcasted_iota(jnp.int32, (1, PAIR_W), 1) < V_DIM
    for sb in range(group):
        rows = slice(sb * seq_len, (sb + 1) * seq_len) if group > 1 else slice(None)
        for pair in range(N_HEADS // 2):
            vs = slice(2 * pair * PAIR_W, (2 * pair + 2) * PAIR_W)
            outs = []
            for e in range(2):
                hs = slice((2 * pair + e) * HEAD_SLOT, (2 * pair + e + 1) * HEAD_SLOT)
                qh = q_ref[rows, hs]
                s = _dot_nt(qh, k_ref[rows, hs])
                mx = jnp.max(s, axis=-1, keepdims=True)
                if has_cache:
                    sc = _dot_nt(qh, kc_sc[:, hs])
                    mx = jnp.maximum(mx, jnp.max(sc, axis=-1, keepdims=True))
                p = jnp.exp2((s - mx) * scale2)
                if has_cache:
                    o = _dot(p.astype(BF16), v_ref[rows, vs])
                    o = o + _dot(jnp.exp2((sc - mx) * scale2).astype(BF16), vc_sc[:, vs])
                    outs.append(o[:, :PAIR_W] / o[:, PAIR_W:PAIR_W + 1])
                else:
                    o = _dot(p.astype(BF16), v_ref[rows, vs.start:vs.start + PAIR_W])
                    outs.append(o / jnp.sum(p, axis=-1, keepdims=True))
            o_ref[rows, pair * PAIR_W:(pair + 1) * PAIR_W] = jnp.where(first_half, outs[0], outs[1]).astype(BF16)


def _attn_call(q, k, v, cache, *, n_seq, seq_len, tok_off, tq, group=1):
    has_cache = cache is not None
    assert group == 1 or (tq == seq_len and not has_cache)
    nq = seq_len // tq
    kv_rows = group * seq_len
    q_rows = group * tq
    kv_blk0 = tok_off // kv_rows
    q_blk0 = tok_off // q_rows
    vw = N_HEADS * V_DIM
    in_specs = [
        pl.BlockSpec((q_rows, QK_W), lambda b, j: (q_blk0 + b * nq + j, 0)),
        pl.BlockSpec((kv_rows, QK_W), lambda b, j: (kv_blk0 + b, 0)),
        pl.BlockSpec((kv_rows, V_EXT_W), lambda b, j: (kv_blk0 + b, 0)),
    ]
    args = [q, k, v]
    scratch = []
    if has_cache:
        cache_ckv, cache_krope, l, wk, wv, ek = cache
        full = lambda shape: pl.BlockSpec(shape, lambda b, j: (0,) * len(shape))
        in_specs += [pl.BlockSpec((None, None, PAST_LEN, KV_LORA), lambda b, j: (b, l, 0, 0)),
                     pl.BlockSpec((None, None, PAST_LEN, ROPE_DIM), lambda b, j: (b, l, 0, 0)),
                     _slab(wk.shape, l), _slab(wv.shape, l), full((ROPE_DIM, QK_W))]
        args += [cache_ckv, cache_krope, wk, wv, ek]
        scratch = [pltpu.VMEM((PAST_LEN, QK_W), BF16), pltpu.VMEM((PAST_LEN, V_EXT_W), BF16)]
    return pl.pallas_call(
        functools.partial(_attn_kernel, has_cache=has_cache, group=group, seq_len=seq_len),
        out_shape=jax.ShapeDtypeStruct((n_seq * seq_len, vw), BF16),
        grid=(n_seq // group, nq),
        in_specs=in_specs,
        out_specs=pl.BlockSpec((q_rows, vw), lambda b, j: (b * nq + j, 0)),
        scratch_shapes=scratch,
        compiler_params=_params(("arbitrary", "arbitrary")),
        name="mla_attention_cache" if has_cache else "mla_attention",
    )(*args)


def _hyfilt_kernel(feat_ref, w1_ref, b1_ref, w2_ref, b2_ref, w3_ref, dec_ref, alt_ref, c_ref, s_ref,
                   hre_ref, him_ref, nyq_ref, a_sc, b_sc, *, L):
    j = pl.program_id(0)
    half = HY_ORDER * HY_W

    @pl.when(j == 0)
    def _():
        h = jnp.sin(_dot_hi(feat_ref[...], w1_ref[...]) + b1_ref[...])
        h = jnp.sin(_dot_hi(h, w2_ref[...]) + b2_ref[...])
        h = _dot_x3(h, w3_ref[...])
        dec = dec_ref[...]
        rows = lax.broadcasted_iota(jnp.int32, (L, 1), 0)
        h_fwd = h[:, :half] * dec
        h_bwd = jnp.where(rows > 0, h[:, half:] * dec, 0.0)
        norm = (jnp.sum(jnp.abs(h_fwd), axis=0, keepdims=True)
                + jnp.sum(jnp.abs(h_bwd), axis=0, keepdims=True) + EPS)
        inv = 1.0 / norm
        a = (h_fwd + h_bwd) * inv
        b = (h_fwd - h_bwd) * inv
        a_sc[...] = a.astype(BF16)
        b_sc[...] = b.astype(BF16)
        nyq = jnp.sum(a * alt_ref[...], axis=0, keepdims=True)
        for o in range(HY_ORDER):
            nyq_ref[o] = nyq[:, o * HY_W:(o + 1) * HY_W]

    hre_ref[...] = _dot(c_ref[...], a_sc[...])
    him_ref[...] = -_dot(s_ref[...], b_sc[...])


def _hyfilt_call(L, l, feats, w1, b1, w2, b2, w3, decay, alt, ctab, stab):
    tk = min(L, 512)
    half = HY_ORDER * HY_W

    def full(shape):
        return pl.BlockSpec(shape, lambda j: (0,) * len(shape))

    return pl.pallas_call(
        functools.partial(_hyfilt_kernel, L=L),
        out_shape=(jax.ShapeDtypeStruct((L, half), F32), jax.ShapeDtypeStruct((L, half), F32),
                   jax.ShapeDtypeStruct((HY_ORDER, 1, HY_W), F32)),
        grid=(L // tk,),
        in_specs=[
            full((L, POS_PAD)), _slab(w1.shape, l), _slab(b1.shape, l), _slab(w2.shape, l),
            _slab(b2.shape, l), _slab(w3.shape, l), full((L, half)), full((L, 1)),
            pl.BlockSpec((tk, L), lambda j: (j, 0)), pl.BlockSpec((tk, L), lambda j: (j, 0)),
        ],
        out_specs=(pl.BlockSpec((tk, half), lambda j: (j, 0)), pl.BlockSpec((tk, half), lambda j: (j, 0)),
                   full((HY_ORDER, 1, HY_W))),
        scratch_shapes=[pltpu.VMEM((L, half), BF16), pltpu.VMEM((L, half), BF16)],
        compiler_params=_params(("arbitrary",)),
        name="hyena_filter_spectrum_%d" % L,
    )(feats, w1, b1, w2, b2, w3, decay, alt, ctab, stab)


def _hyconv_kernel(hy_ref, cw_ref, cb_ref, skip_ref, hre_ref, him_ref, nyq_ref, alt_ref, c_ref, s_ref, o_ref,
                   x1_sc, x2_sc, v_sc, vb_sc, yre_sc, yim_sc, nv_sc, *, n_seq, L, tk):
    p = pl.program_id(0)
    ph = pl.program_id(1)
    j = pl.program_id(2)
    cols = [slice(b * HY_W, (b + 1) * HY_W) for b in range(n_seq)]
    blk = pl.ds(pl.multiple_of(j * tk, tk), tk)

    @pl.when((p == 0) & (ph == 0) & (j == 0))
    def _():
        rows = lax.broadcasted_iota(jnp.int32, (L, 1), 0)
        w = cw_ref[...]
        for b in range(n_seq):
            x = hy_ref[b * L:(b + 1) * L, :]
            prev = jnp.where(rows > 0, pltpu.roll(x, 1, 0), 0.0)
            nxt = jnp.where(rows < L - 1, pltpu.roll(x, L - 1, 0), 0.0)
            u = prev * w[0:1] + x * w[1:2] + nxt * w[2:3] + cb_ref[...]
            x1_sc[:, cols[b]] = u[:, :HY_W]
            x2_sc[:, cols[b]] = u[:, HY_W:2 * HY_W]
            v_sc[:, cols[b]] = u[:, 2 * HY_W:]

    @pl.when((ph == 0) & (j == 0))
    def _():
        v = v_sc[...]
        vb_sc[...] = v.astype(BF16)
        nv_sc[...] = jnp.sum(v * alt_ref[...], axis=0, keepdims=True)

    @pl.when(ph == 0)
    def _():
        vb = vb_sc[...]
        v_re = _dot(c_ref[...], vb)
        v_im = -_dot(s_ref[...], vb)
        freq = j * tk + lax.broadcasted_iota(jnp.int32, (tk, 1), 0)
        wk = jnp.where(freq == 0, 0.5, 1.0)
        h_re = hre_ref[...] * wk
        h_im = him_ref[...] * wk
        for b in range(n_seq):
            yre_sc[blk, cols[b]] = (v_re[:, cols[b]] * h_re - v_im[:, cols[b]] * h_im).astype(BF16)
            yim_sc[blk, cols[b]] = (v_re[:, cols[b]] * h_im + v_im[:, cols[b]] * h_re).astype(BF16)

    @pl.when(ph == 1)
    def _():
        acc = _dot(c_ref[...], yre_sc[...]) - _dot(s_ref[...], yim_sc[...])
        alt = alt_ref[blk, :]
        for b in range(n_seq):
            v = v_sc[blk, cols[b]]
            nyq = nv_sc[:, cols[b]] * nyq_ref[...]
            y = acc[:, cols[b]] * (1.0 / L) + (0.5 / L) * alt * nyq + v * skip_ref[...]

            @pl.when(p == 0)
            def _():
                v_sc[blk, cols[b]] = x1_sc[blk, cols[b]] * y

            @pl.when(p == 1)
            def _():
                o_ref[pl.ds(pl.multiple_of(b * L + j * tk, tk), tk), :] = (x2_sc[blk, cols[b]] * y).astype(BF16)


def _hyconv_call(hy, l, conv_w, conv_b, skip, hre, him, nyq, alt, ctab, stab, *, n_seq, L, tok_off):
    tk = min(L, 512)
    nk = L // tk
    seg = tok_off // (n_seq * L)
    width = n_seq * HY_W

    def full(shape):
        return pl.BlockSpec(shape, lambda p, ph, j: (0,) * len(shape))

    def filt(p, ph, j):
        return (jnp.where(ph == 0, j, nk - 1), p)

    vm = lambda dtype: pltpu.VMEM((L, width), dtype)
    return pl.pallas_call(
        functools.partial(_hyconv_kernel, n_seq=n_seq, L=L, tk=tk),
        out_shape=jax.ShapeDtypeStruct((n_seq * L, HY_W), BF16),
        grid=(HY_ORDER, 2, nk),
        in_specs=[
            pl.BlockSpec((n_seq * L, 3 * HY_W), lambda p, ph, j: (seg, 0), pipeline_mode=pl.Buffered(1)),
            _slab(conv_w.shape, l), _slab(conv_b.shape, l),
            pl.BlockSpec((None, None, 1, HY_W), lambda p, ph, j: (l, p, 0, 0)),
            pl.BlockSpec((tk, HY_W), filt), pl.BlockSpec((tk, HY_W), filt),
            pl.BlockSpec((None, 1, HY_W), lambda p, ph, j: (p, 0, 0)),
            full((L, 1)),
            pl.BlockSpec((tk, L), lambda p, ph, j: (j, 0)), pl.BlockSpec((tk, L), lambda p, ph, j: (j, 0)),
        ],
        out_specs=pl.BlockSpec((n_seq * L, HY_W), lambda p, ph, j: (0, 0)),
        scratch_shapes=[vm(F32), vm(F32), vm(F32), vm(BF16), vm(BF16), vm(BF16), pltpu.VMEM((1, width), F32)],
        compiler_params=_params(("arbitrary", "arbitrary", "arbitrary")),
        name="hyena_long_conv_%d" % L,
    )(hy, conv_w, conv_b, skip, hre, him, nyq, alt, ctab, stab)


def _fnet_kernel(fcs_ref, c_ref, s_ref, o_ref, *, n_seq, L, tk, scale):
    j = pl.program_id(0)
    c = c_ref[...]
    s = s_ref[...]
    for b in range(n_seq):
        seq = slice(b * L, (b + 1) * L)
        y = (_dot(c, fcs_ref[seq, :FN_W]) - _dot(s, fcs_ref[seq, FN_W:])) * scale
        o_ref[pl.ds(pl.multiple_of(b * L + j * tk, tk), tk), :] = y.astype(BF16)


def _fnet_call(fcs, ctab, stab, *, n_seq, L, tok_off):
    tk = min(L, 512)
    seg = tok_off // (n_seq * L)
    return pl.pallas_call(
        functools.partial(_fnet_kernel, n_seq=n_seq, L=L, tk=tk, scale=1.0 / math.sqrt(L * FN_GROUP_W)),
        out_shape=jax.ShapeDtypeStruct((n_seq * L, FN_W), BF16),
        grid=(L // tk,),
        in_specs=[
            pl.BlockSpec((n_seq * L, 2 * FN_W), lambda j: (seg, 0)),
            pl.BlockSpec((tk, L), lambda j: (j, 0)), pl.BlockSpec((tk, L), lambda j: (j, 0)),
        ],
        out_specs=pl.BlockSpec((n_seq * L, FN_W), lambda j: (0, 0)),
        compiler_params=_params(("arbitrary",)),
        name="fnet_position_dft_%d" % L,
    )(fcs, ctab, stab)


def _mixout_kernel(h_ref, mod_ref, ng_ref, wg_ref, zc_ref, zl_ref, ac_ref, al_ref, fc_ref, fl_ref,
                   wa_ref, wb_ref, wc_ref, wo_ref, o_ref):
    i = pl.program_id(0)
    is_ctx = i < T_CTX // TM_MIX
    m = mod_ref[_group_of(i * TM_MIX)]
    for sub in range(MIX_SUB):
        rows = slice(sub * (TM_MIX // MIX_SUB), (sub + 1) * (TM_MIX // MIX_SUB))
        h = h_ref[rows, :]
        n = (_rms(h, ng_ref[...]) * (1.0 + m[4:5]) + m[3:4]).astype(BF16)
        z = jnp.where(is_ctx, zc_ref[rows, :], zl_ref[rows, :])
        a = jnp.where(is_ctx, ac_ref[rows, :], al_ref[rows, :])
        f = jnp.where(is_ctx, fc_ref[rows, :], fl_ref[rows, :])
        acc = _sigmoid(_dot_nt(n, wg_ref[0:D_MODEL, :])) * _dot(z, wa_ref[...])
        acc = acc + _sigmoid(_dot_nt(n, wg_ref[D_MODEL:2 * D_MODEL, :])) * _dot(a, wb_ref[...])
        acc = acc + _sigmoid(_dot_nt(n, wg_ref[2 * D_MODEL:, :])) * _dot(f, wc_ref[...])
        y = _dot(acc.astype(BF16), wo_ref[...])
        o_ref[rows, :] = h + m[5:6] * y


def _mixout_call(h, mods, ng, w_gate, l, z, a, f, wa, wb, wc, wo):
    tm = TM_MIX
    n_ctx = T_CTX // tm

    def full(shape):
        return pl.BlockSpec(shape, lambda i: (0,) * len(shape))

    def tok(width):
        return pl.BlockSpec((tm, width), lambda i: (i, 0))

    def pair(width):
        return [pl.BlockSpec((tm, width), lambda i: (jnp.minimum(i, n_ctx - 1), 0)),
                pl.BlockSpec((tm, width), lambda i: (jnp.maximum(i - n_ctx, 0), 0))]

    return pl.pallas_call(
        _mixout_kernel,
        out_shape=jax.ShapeDtypeStruct((T_ALL, D_MODEL), F32),
        grid=(T_ALL // tm,),
        in_specs=[
            tok(D_MODEL), _slab(mods.shape, l), _slab(ng.shape, l, 1), _slab(w_gate.shape, l),
            *pair(HY_W), *pair(N_HEADS * V_DIM), *pair(FN_W),
            _slab(wa.shape, l), _slab(wb.shape, l), _slab(wc.shape, l), _slab(wo.shape, l),
        ],
        out_specs=tok(D_MODEL),
        compiler_params=_params(("parallel",)),
        name="gated_merge_out_proj",
    )(h, mods, ng, w_gate, *z, *a, *f, wa, wb, wc, wo)


def _wprep_kernel(w_ref, mix_ref, gate_ref):
    r = pl.program_id(1)
    n_gate = N_BRANCH * D_MODEL
    g0 = IN_GATE - PREP_ROWS
    last = n_gate - (3 * PREP_ROWS - IN_GATE)

    @pl.when(r == 0)
    def _():
        mix_ref[:MIX_FN, :] = w_ref[...].astype(BF16)

    @pl.when(r == 1)
    def _():
        mix_ref[MIX_FN:MIX_KR, :] = w_ref[ROPE_DIM:g0, :].astype(BF16)
        mix_ref[MIX_KR:MIX_KR + ROPE_DIM, :] = w_ref[0:ROPE_DIM, :].astype(BF16)
        mix_ref[MIX_KR + ROPE_DIM:, :] = jnp.zeros((MIX_W - MIX_KR - ROPE_DIM, D_MODEL), BF16)
        gate_ref[:PREP_ROWS - g0, :] = w_ref[g0:, :].astype(BF16)

    @pl.when(r == 2)
    def _():
        gate_ref[PREP_ROWS - g0:2 * PREP_ROWS - g0, :] = w_ref[...].astype(BF16)

    @pl.when(r == 3)
    def _():
        gate_ref[2 * PREP_ROWS - g0:, :] = w_ref[:last, :].astype(BF16)


def _wprep_call(w_in_t):
    assert IN_KR == PREP_ROWS and 4 * PREP_ROWS >= IN_COLS
    return pl.pallas_call(
        _wprep_kernel,
        out_shape=(jax.ShapeDtypeStruct((DEPTH, MIX_W, D_MODEL), BF16),
                   jax.ShapeDtypeStruct((DEPTH, N_BRANCH * D_MODEL, D_MODEL), BF16)),
        grid=(DEPTH, 4),
        in_specs=[pl.BlockSpec((None, PREP_ROWS, D_MODEL), lambda l, r: (l, r, 0))],
        out_specs=(pl.BlockSpec((None, MIX_W, D_MODEL), lambda l, r: (l, 0, 0)),
                   pl.BlockSpec((None, N_BRANCH * D_MODEL, D_MODEL), lambda l, r: (l, 0, 0))),
        compiler_params=_params(("arbitrary", "arbitrary")),
        name="input_proj_weight_relayout",
    )(w_in_t)


def _head_slot_weights(w_qb, w_kvb):
    wq3 = w_qb.reshape(DEPTH, Q_LORA, N_HEADS, NOPE_DIM + ROPE_DIM)
    slot_pad = HEAD_SLOT - NOPE_DIM - ROPE_DIM
    no_pad = ((0, 0),) * 3
    wq = jnp.pad(wq3, no_pad + ((0, slot_pad),)).reshape(DEPTH, Q_LORA, QK_W).astype(BF16)
    rope = wq3[..., NOPE_DIM:].reshape(DEPTH, Q_LORA, N_HEADS, 2, 2, ROPE_HALF)
    partner = jnp.stack([-rope[..., 1, :], rope[..., 0, :]], axis=4).reshape(DEPTH, Q_LORA, N_HEADS, ROPE_DIM)
    wqs = jnp.pad(partner, no_pad + ((NOPE_DIM, slot_pad),)).reshape(DEPTH, Q_LORA, QK_W).astype(BF16)

    wkv3 = w_kvb.reshape(DEPTH, KV_LORA, N_HEADS, NOPE_DIM + V_DIM)
    wk = jnp.pad(wkv3[..., :NOPE_DIM], no_pad + ((0, HEAD_SLOT - NOPE_DIM),))
    wk = wk.reshape(DEPTH, KV_LORA, QK_W).astype(BF16)
    wv = wkv3[..., NOPE_DIM:].reshape(DEPTH, KV_LORA, N_HEADS * V_DIM).astype(BF16)
    return wq, wqs, wk, wv


def kernel(x_prompt, x_sample, cache_ckv, cache_krope, c, c_ctx, w_ada, b_ada, norm_g, w_ffn_up, w_ffn_down,
           w_in, hy_conv_w, hy_conv_b, hy_filt_w1, hy_filt_b1, hy_filt_w2, hy_filt_b2, hy_filt_w3, hy_skip,
           w_hy_out, q_norm_g, w_qb, kv_norm_g, w_kvb, w_mla_o, w_fnet, w_out, final_g):
    tabs = _tables()
    cosq, sinq, cosk, sink = (jnp.asarray(t) for t in tabs["rope"])
    fch = jnp.asarray(tabs["fnch"]).astype(BF16)
    ek = jnp.asarray(tabs["ropeexp"]).astype(BF16)
    dft = {}
    for L in (SEQ, DEC_SEQ):
        dft[("hy", L)] = tuple(jnp.asarray(t).astype(BF16) for t in tabs[("hy", L)])
        dft[("fn", L)] = tuple(jnp.asarray(t).astype(BF16) for t in tabs[("fn", L)])

    cvec = jnp.concatenate([c_ctx[None, :], c, jnp.zeros((8 - N_GROUPS, D_MODEL), F32)], axis=0).T
    ada = _ada_call(cvec, w_ada, b_ada)
    mods = ada[:, :N_GROUPS].reshape(DEPTH, N_GROUPS, 9, D_MODEL)
    ng = norm_g.reshape(DEPTH, 3, 1, D_MODEL)
    qg = q_norm_g.reshape(DEPTH, 1, Q_LORA)
    kvg = kv_norm_g.reshape(DEPTH, 1, KV_LORA)
    wq, wqs, wk, wv = _head_slot_weights(w_qb, w_kvb)
    w1 = jnp.pad(hy_filt_w1, ((0, 0), (0, POS_PAD - POS_EMB), (0, 0)))
    b1 = hy_filt_b1.reshape(DEPTH, 1, FILT_HID)
    b2 = hy_filt_b2.reshape(DEPTH, 1, FILT_HID)
    conv_b = hy_conv_b.reshape(DEPTH, 1, 3 * HY_W)
    skip = hy_skip.reshape(DEPTH, HY_ORDER, 1, HY_W)

    w_mix, w_gate = _wprep_call(jnp.swapaxes(w_in, 1, 2))
    w_hy_out, w_mla_o, w_fnet, w_out = (w.astype(BF16) for w in (w_hy_out, w_mla_o, w_fnet, w_out))
    hs = (x_prompt.reshape(T_CTX, D_MODEL), x_sample.reshape(T_LAT, D_MODEL))
    segs = ((BATCH, SEQ, 0), (DEC_BATCH, DEC_SEQ, T_CTX))
    ckv_out = []
    kr_out = []
    for l in range(DEPTH):
        h = _ffn_call(hs, mods, ng, w_ffn_up, w_ffn_down, l, 0, 0)

        hy, q, k, v, ckv, k_r, fcs = _mixin_call(
            h, mods, ng, w_mix, l, qg, kvg, wq, wqs, wk, wv, fch, cosq, sinq, cosk, sink)
        ckv_out.append(ckv.reshape(BATCH, SEQ, KV_LORA))
        kr_out.append(k_r.reshape(BATCH, SEQ, ROPE_DIM))

        z_parts, a_parts, f_parts = [], [], []
        for n_seq, L, off in segs:
            feats, decay, alt = (jnp.asarray(t) for t in tabs[("filt", L)])
            c_hy, s_hy = dft[("hy", L)]
            hre, him, nyq = _hyfilt_call(L, l, feats, w1, b1, hy_filt_w2, b2, hy_filt_w3, decay, alt, c_hy, s_hy)
            z_parts.append(_hyconv_call(hy, l, hy_conv_w, conv_b, skip, hre, him, nyq, alt, c_hy, s_hy,
                                        n_seq=n_seq, L=L, tok_off=off))
            if off:
                a_parts.append(_attn_call(q, k, v, (cache_ckv, cache_krope, l, wk, wv, ek),
                                          n_seq=n_seq, seq_len=L, tok_off=off, tq=ATTN_TQ))
            else:
                a_parts.append(_attn_call(q, k, v, None, n_seq=n_seq, seq_len=L, tok_off=off, tq=L,
                                          group=ATTN_GROUP))
            c_fn, s_fn = dft[("fn", L)]
            f_parts.append(_fnet_call(fcs, c_fn, s_fn, n_seq=n_seq, L=L, tok_off=off))
        h = _mixout_call(h, mods, ng, w_gate, l, z_parts, a_parts, f_parts, w_hy_out, w_mla_o, w_fnet, w_out)

        last = l == DEPTH - 1
        out = _ffn_call((h,), mods, ng, w_ffn_up, w_ffn_down, l, 1, 6, final_g[None, :] if last else None)
        hs = out if last else (out,)

    y_prompt = hs[0].reshape(BATCH, SEQ, D_MODEL)
    y_sample = hs[1].reshape(DEC_BATCH, DEC_SEQ, D_MODEL)
    return y_prompt, y_sample, jnp.stack(ckv_out, axis=1), jnp.stack(kr_out, axis=1)
```

```python
import functools
import math

import numpy as np
import jax
import jax.numpy as jnp
from jax import lax
from jax.experimental import pallas as pl
from jax.experimental.pallas import tpu as pltpu

F32 = jnp.float32
BF16 = jnp.bfloat16
HIGHEST = lax.Precision.HIGHEST

D_MODEL = 1024
BATCH = 16
SEQ = 256
DEPTH = 2
DEC_BATCH = 2
DEC_SEQ = 2048
PAST_LEN = 512
GRID_W = 64
HY_W = 256
HY_ORDER = 2
N_BANDS = 8
POS_EMB = 1 + 2 * N_BANDS
FILT_HID = 64
N_HEADS = 8
Q_LORA = 256
KV_LORA = 128
NOPE_DIM = 64
ROPE_DIM = 32
V_DIM = 64
ROPE_BASE = 10000.0
FN_GROUPS = 4
FN_GROUP_W = 64
FN_W = FN_GROUPS * FN_GROUP_W
N_BRANCH = 3
D_FF = 2816
EPS = 1e-6

T_CTX = BATCH * SEQ
T_LAT = DEC_BATCH * DEC_SEQ
T_ALL = T_CTX + T_LAT
N_GROUPS = 1 + DEC_BATCH
LANES = 128
SUBLANES = 8
VMEM_BYTES = 64 * 1024 * 1024
VMEM_LIMIT = VMEM_BYTES - 8 * 1024 * 1024

HEAD_SLOT = LANES
QK_W = N_HEADS * HEAD_SLOT
PAIR_W = 2 * V_DIM
V_EXT_W = N_HEADS * PAIR_W
POS_PAD = LANES
assert PAIR_W == LANES and NOPE_DIM + ROPE_DIM <= HEAD_SLOT

TM_FFN = 512
TF_FFN = 256
FFN_LOAD = 8
FFN_UP_ROWS = D_MODEL // FFN_LOAD
FFN_DN_ROWS = D_FF // FFN_LOAD
TM_FFN_WIDE = 1024
FFN_SUB_ROWS = 256
TM_MIX = 512
TM_OUT = 1024
OUT_SUB_ROWS = 256
ADA_TN = 2304
ATTN_TQ = 512
ATTN_GROUP = 4
PREP_ROWS = 1152

IN_KR = 3 * HY_W + Q_LORA + KV_LORA
IN_FN = IN_KR + ROPE_DIM
IN_GATE = IN_FN + FN_W
IN_COLS = IN_GATE + N_BRANCH * D_MODEL

MIX_HY = 0
MIX_QA = 3 * HY_W
MIX_KVA = MIX_QA + Q_LORA
MIX_FN = MIX_KVA + KV_LORA
MIX_KR = MIX_FN + FN_W
MIX_W = MIX_KR + LANES
ROPE_HALF = ROPE_DIM // 4


def _dft_tables(L, half):
    k = np.arange(L, dtype=np.int64)
    period = 2 * L if half else L
    m = (k[:, None] * k[None, :]) % period
    ang = 2.0 * np.pi * m.astype(np.float64) / period
    return np.cos(ang).astype(np.float32), np.sin(ang).astype(np.float32)


def _filter_tables(L):
    t = np.arange(L, dtype=np.float64)
    t_norm = t / (L - 1)
    w = 2.0 * np.pi * t / L
    bands = np.linspace(1e-4, N_BANDS - 1, N_BANDS)
    ang = w[:, None] * bands[None, :]
    feats = np.concatenate([t_norm[:, None], np.cos(ang), -np.sin(ang)], axis=-1)
    feats = np.pad(feats, ((0, 0), (0, POS_PAD - POS_EMB)))
    deltas = np.linspace(math.log(1e-2) / 1.5, math.log(1e-2) / 0.3, HY_W)
    decay = np.exp(-t_norm[:, None] * np.abs(deltas)[None, :])
    decay = np.concatenate([decay, decay], axis=1)
    alt = np.where(np.arange(L) % 2 == 0, 1.0, -1.0)[:, None]
    return feats.astype(np.float32), decay.astype(np.float32), alt.astype(np.float32)


def _rope_tables():
    t = np.arange(DEC_SEQ)
    row = (t // GRID_W).astype(np.float64)
    col = (t % GRID_W).astype(np.float64)
    nf = ROPE_DIM // 4
    inv = ROPE_BASE ** (-np.arange(nf, dtype=np.float64) / nf)
    ar = row[:, None] * inv[None, :]
    ac = col[:, None] * inv[None, :]
    cos32 = np.concatenate([np.cos(ar), np.cos(ar), np.cos(ac), np.cos(ac)], axis=1)
    sin32 = np.concatenate([np.sin(ar), np.sin(ar), np.sin(ac), np.sin(ac)], axis=1)
    cos32 = np.concatenate([cos32, np.ones((TM_MIX, ROPE_DIM))], axis=0)
    sin32 = np.concatenate([sin32, np.zeros((TM_MIX, ROPE_DIM))], axis=0)
    n = cos32.shape[0]
    cosq = np.ones((n, HEAD_SLOT))
    sinq = np.zeros((n, HEAD_SLOT))
    cosq[:, NOPE_DIM:NOPE_DIM + ROPE_DIM] = cos32
    sinq[:, NOPE_DIM:NOPE_DIM + ROPE_DIM] = sin32
    cosk = np.zeros((n, HEAD_SLOT))
    sink = np.zeros((n, HEAD_SLOT))
    cosk[:, :ROPE_DIM] = cos32
    sink[:, :ROPE_DIM] = sin32
    return tuple(t.astype(np.float32) for t in (cosq, sinq, cosk, sink))


def _fnet_channel_table():
    j = np.arange(FN_GROUP_W)
    ang = 2.0 * np.pi * ((j[:, None] * j[None, :]) % FN_GROUP_W) / FN_GROUP_W
    out = np.zeros((FN_W, 2 * FN_W))
    for g in range(FN_GROUPS):
        sl = slice(g * FN_GROUP_W, (g + 1) * FN_GROUP_W)
        out[sl, sl] = np.cos(ang)
        out[sl, FN_W + g * FN_GROUP_W:FN_W + (g + 1) * FN_GROUP_W] = np.sin(ang)
    return out.astype(np.float32)


def _rope_expand_table():
    e = np.zeros((ROPE_DIM, N_HEADS, HEAD_SLOT), np.float32)
    for j in range(ROPE_DIM):
        e[j, :, NOPE_DIM + j] = 1.0
    return e.reshape(ROPE_DIM, QK_W)


_TABLES = {}


def _tables():
    if not _TABLES:
        for L in (SEQ, DEC_SEQ):
            _TABLES[("hy", L)] = _dft_tables(L, True)
            _TABLES[("fn", L)] = _dft_tables(L, False)
            _TABLES[("filt", L)] = _filter_tables(L)
        _TABLES["rope"] = _rope_tables()
        _TABLES["fnch"] = _fnet_channel_table()
        _TABLES["ropeexp"] = _rope_expand_table()
    return _TABLES


def _rms(x, g):
    ms = jnp.mean(x * x, axis=-1, keepdims=True)
    return x * lax.rsqrt(ms + EPS) * g


def _sigmoid(x):
    return 1.0 / (1.0 + jnp.exp(-x))


def _dot(a, b):
    return jnp.dot(a, b, preferred_element_type=F32)


def _dot_hi(a, b):
    return jnp.dot(a, b, precision=HIGHEST, preferred_element_type=F32)


def _dot_x3(a, b):
    a_hi = a.astype(BF16)
    b_hi = b.astype(BF16)
    a_lo = (a - a_hi.astype(F32)).astype(BF16)
    b_lo = (b - b_hi.astype(F32)).astype(BF16)
    return _dot(a_hi, b_hi) + (_dot(a_hi, b_lo) + _dot(a_lo, b_hi))


def _dot_nt(a, b):
    return lax.dot_general(a, b, (((1,), (1,)), ((), ())), preferred_element_type=F32)


def _store_values(v_ref, v):
    ones = jnp.ones((v.shape[0], PAIR_W), BF16)
    for pair in range(N_HEADS // 2):
        v_ref[:, 2 * pair * PAIR_W:(2 * pair + 1) * PAIR_W] = v[:, pair * PAIR_W:(pair + 1) * PAIR_W].astype(BF16)
        v_ref[:, (2 * pair + 1) * PAIR_W:(2 * pair + 2) * PAIR_W] = ones


def _group_of(tok0):
    return jnp.where(tok0 < T_CTX, 0, 1 + (tok0 - T_CTX) // DEC_SEQ)


def _slab(shape, *lead, single_buffer=False):
    tail = tuple(shape[len(lead):])
    mode = dict(pipeline_mode=pl.Buffered(1)) if single_buffer else {}
    return pl.BlockSpec((None,) * len(lead) + tail, lambda *_: tuple(lead) + (0,) * len(tail), **mode)


def _params(sem):
    return pltpu.CompilerParams(dimension_semantics=sem, vmem_limit_bytes=VMEM_LIMIT)


def _ada_kernel(c_ref, w_ref, b_ref, o_ref):
    x = c_ref[...]
    s = x * _sigmoid(x)
    w = w_ref[...]
    o_ref[...] = jnp.zeros_like(o_ref)
    for g in range(N_GROUPS):
        o_ref[g:g + 1, :] = jnp.sum(w * s[:, g:g + 1], axis=0, keepdims=True) + b_ref[...]


def _ada_call(cvec, w_ada, b_ada):
    tn = ADA_TN
    n_out = 9 * D_MODEL
    return pl.pallas_call(
        _ada_kernel,
        out_shape=jax.ShapeDtypeStruct((DEPTH, SUBLANES, n_out), F32),
        grid=(DEPTH, n_out // tn),
        in_specs=[
            pl.BlockSpec((D_MODEL, SUBLANES), lambda l, j: (0, 0)),
            pl.BlockSpec((None, D_MODEL, tn), lambda l, j: (l, 0, j)),
            pl.BlockSpec((None, 1, tn), lambda l, j: (l, 0, j)),
        ],
        out_specs=pl.BlockSpec((None, SUBLANES, tn), lambda l, j: (l, 0, j)),
        compiler_params=_params(("arbitrary", "arbitrary")),
        name="ada_modulation",
    )(cvec, w_ada, b_ada.reshape(DEPTH, 1, n_out))


def _ffn_kernel(*refs, j0, split_in, final, tm):
    refs = list(refs)
    x_refs = [refs.pop(0) for _ in range(2 if split_in else 1)]
    mod_ref, ng_ref, wup_ref, wd_ref = refs[:4]
    refs = refs[4:]
    fg_ref = refs.pop(0) if final else None
    o_refs = [refs.pop(0) for _ in range(2 if final else 1)]
    wup_sc, wd_sc, hid_sc = refs
    s = pl.program_id(0)
    n_ctx_tiles = T_CTX // tm

    @pl.when(s < FFN_LOAD)
    def _():
        wup_sc[pl.ds(pl.multiple_of(s * FFN_UP_ROWS, FFN_UP_ROWS), FFN_UP_ROWS), :] = wup_ref[...].astype(BF16)
        wd_sc[pl.ds(pl.multiple_of(s * FFN_DN_ROWS, FFN_DN_ROWS), FFN_DN_ROWS), :] = wd_ref[...].astype(BF16)

    @pl.when(s >= FFN_LOAD)
    def _():
        t = s - FFN_LOAD
        m = mod_ref[_group_of(t * tm)]
        ys = []
        for sub in range(tm // FFN_SUB_ROWS):
            rows = slice(sub * FFN_SUB_ROWS, (sub + 1) * FFN_SUB_ROWS)
            if split_in:
                x = jnp.where(t < n_ctx_tiles, x_refs[0][rows, :], x_refs[1][rows, :])
            else:
                x = x_refs[0][rows, :]
            n = (_rms(x, ng_ref[...]) * (1.0 + m[j0 + 1:j0 + 2]) + m[j0:j0 + 1]).astype(BF16)
            for c in range(D_FF // TF_FFN):
                cols = slice(c * TF_FFN, (c + 1) * TF_FFN)
                g = _dot(n, wup_sc[:, cols])
                u = _dot(n, wup_sc[:, D_FF + c * TF_FFN:D_FF + (c + 1) * TF_FFN])
                hid_sc[rows, cols] = (g * _sigmoid(g) * u).astype(BF16)
            y = x + 0.5 * m[j0 + 2:j0 + 3] * _dot(hid_sc[rows, :], wd_sc[...])
            if final:
                y = _rms(y, fg_ref[...])
            ys.append((rows, y))
        if final:
            @pl.when(t < n_ctx_tiles)
            def _():
                for rows, y in ys:
                    o_refs[0][rows, :] = y

            @pl.when(t >= n_ctx_tiles)
            def _():
                for rows, y in ys:
                    o_refs[1][rows, :] = y
        else:
            for rows, y in ys:
                o_refs[0][rows, :] = y


def _ffn_call(xs, mods, ng, w_up, w_down, l, f, j0, final_g=None):
    split_in = len(xs) == 2
    final = final_g is not None
    tm = TM_FFN if (split_in or final) else TM_FFN_WIDE
    n_ctx_tiles = T_CTX // tm
    tile = lambda s: jnp.maximum(s - FFN_LOAD, 0)
    chunk = lambda s: jnp.minimum(s, FFN_LOAD - 1)
    ctx_blk = lambda s: (jnp.minimum(tile(s), n_ctx_tiles - 1), 0)
    lat_blk = lambda s: (jnp.maximum(tile(s) - n_ctx_tiles, 0), 0)
    row = pl.BlockSpec((1, D_MODEL), lambda s: (0, 0))
    tok = lambda index_map: pl.BlockSpec((tm, D_MODEL), index_map)
    if split_in:
        in_specs = [tok(ctx_blk), tok(lat_blk)]
    else:
        in_specs = [tok(lambda s: (tile(s), 0))]
    in_specs += [
        _slab(mods.shape, l), _slab(ng.shape, l, j0 // 3),
        pl.BlockSpec((None, None, FFN_UP_ROWS, 2 * D_FF), lambda s: (l, f, chunk(s), 0)),
        pl.BlockSpec((None, None, FFN_DN_ROWS, D_MODEL), lambda s: (l, f, chunk(s), 0)),
    ]
    args = list(xs) + [mods, ng, w_up, w_down]
    if final:
        in_specs.append(row)
        args.append(final_g)
        out_shape = (jax.ShapeDtypeStruct((T_CTX, D_MODEL), F32), jax.ShapeDtypeStruct((T_LAT, D_MODEL), F32))
        out_specs = (tok(ctx_blk), tok(lat_blk))
    else:
        out_shape = jax.ShapeDtypeStruct((T_ALL, D_MODEL), F32)
        out_specs = tok(lambda s: (tile(s), 0))
    return pl.pallas_call(
        functools.partial(_ffn_kernel, j0=j0, split_in=split_in, final=final, tm=tm),
        out_shape=out_shape,
        grid=(FFN_LOAD + T_ALL // tm,),
        in_specs=in_specs,
        out_specs=out_specs,
        scratch_shapes=[pltpu.VMEM((D_MODEL, 2 * D_FF), BF16), pltpu.VMEM((D_FF, D_MODEL), BF16),
                        pltpu.VMEM((tm, D_FF), BF16)],
        compiler_params=_params(("arbitrary",)),
        name="swiglu_half_step",
    )(*args)


def _rope_partner(x):
    lane = lax.broadcasted_iota(jnp.int32, (1, HEAD_SLOT), 1)
    first = (lane % (2 * ROPE_HALF)) < ROPE_HALF
    return jnp.where(first, -pltpu.roll(x, HEAD_SLOT - ROPE_HALF, 1), pltpu.roll(x, ROPE_HALF, 1))


def _mixin_kernel(h_ref, mod_ref, ng_ref, w_ref, qg_ref, kvg_ref, wq_ref, wqs_ref, wk_ref, wv_ref,
                  fch_ref, cq_ref, sq_ref, ck_ref, sk_ref,
                  hy_ref, q_ref, k_ref, v_ref, ckv_ref, kr_ref, fcs_ref):
    i = pl.program_id(0)
    m = mod_ref[_group_of(i * TM_MIX)]
    n = (_rms(h_ref[...], ng_ref[...]) * (1.0 + m[4:5]) + m[3:4]).astype(BF16)
    proj = _dot_nt(n, w_ref[...])
    hy_ref[...] = proj[:, MIX_HY:MIX_QA]
    q_a = proj[:, MIX_QA:MIX_KVA]
    kv_a = proj[:, MIX_KVA:MIX_FN]
    fn = proj[:, MIX_FN:MIX_KR]
    k_r = proj[:, MIX_KR:MIX_W]

    qn = _rms(q_a, qg_ref[...]).astype(BF16)
    q = _dot(qn, wq_ref[...])
    q_partner = _dot(qn, wqs_ref[...])
    cos_q = cq_ref[...]
    sin_q = sq_ref[...]
    for h in range(N_HEADS):
        hs = slice(h * HEAD_SLOT, (h + 1) * HEAD_SLOT)
        q_ref[:, hs] = (q[:, hs] * cos_q + q_partner[:, hs] * sin_q).astype(BF16)

    ckv = _rms(kv_a, kvg_ref[...])

    @pl.when(i < T_CTX // TM_MIX)
    def _():
        ckv_ref[...] = ckv
        kr_ref[...] = k_r[:, :ROPE_DIM]

    ckv_b = ckv.astype(BF16)
    k_rot = k_r * ck_ref[...] + _rope_partner(k_r) * sk_ref[...]
    k_rope = pltpu.roll(k_rot, NOPE_DIM, 1)
    k_nope = _dot(ckv_b, wk_ref[...])
    for h in range(N_HEADS):
        hs = slice(h * HEAD_SLOT, (h + 1) * HEAD_SLOT)
        k_ref[:, hs] = (k_nope[:, hs] + k_rope).astype(BF16)
    _store_values(v_ref, _dot(ckv_b, wv_ref[...]))
    fcs_ref[...] = _dot(fn.astype(BF16), fch_ref[...]).astype(BF16)


def _mixin_call(h, mods, ng, w_mix, l, qg, kvg, wq, wqs, wk, wv, fch, cosq, sinq, cosk, sink):
    tm = TM_MIX

    def pos_block(i):
        tok0 = i * tm
        return jnp.where(tok0 < T_CTX, DEC_SEQ // tm, ((tok0 - T_CTX) % DEC_SEQ) // tm)

    def full(shape):
        return pl.BlockSpec(shape, lambda i: (0,) * len(shape))

    def tok(width):
        return pl.BlockSpec((tm, width), lambda i: (i, 0))

    def pos(width):
        return pl.BlockSpec((tm, width), lambda i: (pos_block(i), 0))

    out_shape = (
        jax.ShapeDtypeStruct((T_ALL, 3 * HY_W), F32),
        jax.ShapeDtypeStruct((T_ALL, QK_W), BF16),
        jax.ShapeDtypeStruct((T_ALL, QK_W), BF16),
        jax.ShapeDtypeStruct((T_ALL, V_EXT_W), BF16),
        jax.ShapeDtypeStruct((T_CTX, KV_LORA), F32),
        jax.ShapeDtypeStruct((T_CTX, ROPE_DIM), F32),
        jax.ShapeDtypeStruct((T_ALL, 2 * FN_W), BF16),
    )
    return pl.pallas_call(
        _mixin_kernel,
        out_shape=out_shape,
        grid=(T_ALL // tm,),
        in_specs=[
            tok(D_MODEL), _slab(mods.shape, l), _slab(ng.shape, l, 1), _slab(w_mix.shape, l),
            _slab(qg.shape, l), _slab(kvg.shape, l), _slab(wq.shape, l), _slab(wqs.shape, l),
            _slab(wk.shape, l), _slab(wv.shape, l),
            full((FN_W, 2 * FN_W)), pos(HEAD_SLOT), pos(HEAD_SLOT), pos(HEAD_SLOT), pos(HEAD_SLOT),
        ],
        out_specs=tuple(
            tok(s.shape[1]) if s.shape[0] == T_ALL else
            pl.BlockSpec((tm, s.shape[1]), lambda i: (jnp.minimum(i, T_CTX // tm - 1), 0))
            for s in out_shape),
        compiler_params=_params(("arbitrary",)),
        name="mixer_input_proj",
    )(h, mods, ng, w_mix, qg, kvg, wq, wqs, wk, wv, fch, cosq, sinq, cosk, sink)


def _attn_kernel(q_ref, k_ref, v_ref, *rest, has_cache, group, seq_len):
    if has_cache:
        cckv_ref, ckr_ref, wk_ref, wv_ref, ek_ref, o_ref, kc_sc, vc_sc = rest

        @pl.when(pl.program_id(1) == 0)
        def _():
            ckv_b = cckv_ref[...].astype(BF16)
            kc_sc[...] = (_dot(ckv_b, wk_ref[...]) + _dot(ckr_ref[...].astype(BF16), ek_ref[...])).astype(BF16)
            _store_values(vc_sc, _dot(ckv_b, wv_ref[...]))
    else:
        (o_ref,) = rest
    scale2 = math.log2(math.e) / math.sqrt(NOPE_DIM + ROPE_DIM)
    first_half = lax.broadcasted_iota(jnp.int32, (1, PAIR_W), 1) < V_DIM
    for sb in range(group):
        rows = slice(sb * seq_len, (sb + 1) * seq_len) if group > 1 else slice(None)
        for pair in range(N_HEADS // 2):
            vs = slice(2 * pair * PAIR_W, (2 * pair + 2) * PAIR_W)
            outs = []
            for e in range(2):
                hs = slice((2 * pair + e) * HEAD_SLOT, (2 * pair + e + 1) * HEAD_SLOT)
                qh = q_ref[rows, hs]
                s = _dot_nt(qh, k_ref[rows, hs])
                mx = jnp.max(s, axis=-1, keepdims=True)
                if has_cache:
                    sc = _dot_nt(qh, kc_sc[:, hs])
                    mx = jnp.maximum(mx, jnp.max(sc, axis=-1, keepdims=True))
                p = jnp.exp2((s - mx) * scale2)
                if has_cache:
                    o = _dot(p.astype(BF16), v_ref[rows, vs])
                    o = o + _dot(jnp.exp2((sc - mx) * scale2).astype(BF16), vc_sc[:, vs])
                    outs.append(o[:, :PAIR_W] / o[:, PAIR_W:PAIR_W + 1])
                else:
                    o = _dot(p.astype(BF16), v_ref[rows, vs.start:vs.start + PAIR_W])
                    outs.append(o / jnp.sum(p, axis=-1, keepdims=True))
            o_ref[rows, pair * PAIR_W:(pair + 1) * PAIR_W] = jnp.where(first_half, outs[0], outs[1]).astype(BF16)


def _attn_call(q, k, v, cache, *, n_seq, seq_len, tok_off, tq, group=1):
    has_cache = cache is not None
    assert group == 1 or (tq == seq_len and not has_cache)
    nq = seq_len // tq
    kv_rows = group * seq_len
    q_rows = group * tq
    kv_blk0 = tok_off // kv_rows
    q_blk0 = tok_off // q_rows
    vw = N_HEADS * V_DIM
    in_specs = [
        pl.BlockSpec((q_rows, QK_W), lambda b, j: (q_blk0 + b * nq + j, 0)),
        pl.BlockSpec((kv_rows, QK_W), lambda b, j: (kv_blk0 + b, 0)),
        pl.BlockSpec((kv_rows, V_EXT_W), lambda b, j: (kv_blk0 + b, 0)),
    ]
    args = [q, k, v]
    scratch = []
    if has_cache:
        cache_ckv, cache_krope, l, wk, wv, ek = cache
        full = lambda shape: pl.BlockSpec(shape, lambda b, j: (0,) * len(shape))
        in_specs += [pl.BlockSpec((None, None, PAST_LEN, KV_LORA), lambda b, j: (b, l, 0, 0)),
                     pl.BlockSpec((None, None, PAST_LEN, ROPE_DIM), lambda b, j: (b, l, 0, 0)),
                     _slab(wk.shape, l), _slab(wv.shape, l), full((ROPE_DIM, QK_W))]
        args += [cache_ckv, cache_krope, wk, wv, ek]
        scratch = [pltpu.VMEM((PAST_LEN, QK_W), BF16), pltpu.VMEM((PAST_LEN, V_EXT_W), BF16)]
    return pl.pallas_call(
        functools.partial(_attn_kernel, has_cache=has_cache, group=group, seq_len=seq_len),
        out_shape=jax.ShapeDtypeStruct((n_seq * seq_len, vw), BF16),
        grid=(n_seq // group, nq),
        in_specs=in_specs,
        out_specs=pl.BlockSpec((q_rows, vw), lambda b, j: (b * nq + j, 0)),
        scratch_shapes=scratch,
        compiler_params=_params(("arbitrary", "arbitrary")),
        name="mla_attention_cache" if has_cache else "mla_attention",
    )(*args)


def _hyfilt_kernel(feat_ref, w1_ref, b1_ref, w2_ref, b2_ref, w3_ref, dec_ref, alt_ref, c_ref, s_ref,
                   hre_ref, him_ref, nyq_ref, a_sc, b_sc, *, L):
    j = pl.program_id(0)
    half = HY_ORDER * HY_W

    @pl.when(j == 0)
    def _():
        h = jnp.sin(_dot_hi(feat_ref[...], w1_ref[...]) + b1_ref[...])
        h = jnp.sin(_dot_hi(h, w2_ref[...]) + b2_ref[...])
        h = _dot_x3(h, w3_ref[...])
        dec = dec_ref[...]
        rows = lax.broadcasted_iota(jnp.int32, (L, 1), 0)
        h_fwd = h[:, :half] * dec
        h_bwd = jnp.where(rows > 0, h[:, half:] * dec, 0.0)
        norm = (jnp.sum(jnp.abs(h_fwd), axis=0, keepdims=True)
                + jnp.sum(jnp.abs(h_bwd), axis=0, keepdims=True) + EPS)
        inv = 1.0 / norm
        a = (h_fwd + h_bwd) * inv
        b = (h_fwd - h_bwd) * inv
        a_sc[...] = a.astype(BF16)
        b_sc[...] = b.astype(BF16)
        nyq = jnp.sum(a * alt_ref[...], axis=0, keepdims=True)
        for o in range(HY_ORDER):
            nyq_ref[o] = nyq[:, o * HY_W:(o + 1) * HY_W]

    hre_ref[...] = _dot(c_ref[...], a_sc[...])
    him_ref[...] = -_dot(s_ref[...], b_sc[...])


def _hyfilt_call(L, l, feats, w1, b1, w2, b2, w3, decay, alt, ctab, stab):
    tk = min(L, 512)
    half = HY_ORDER * HY_W

    def full(shape):
        return pl.BlockSpec(shape, lambda j: (0,) * len(shape))

    return pl.pallas_call(
        functools.partial(_hyfilt_kernel, L=L),
        out_shape=(jax.ShapeDtypeStruct((L, half), F32), jax.ShapeDtypeStruct((L, half), F32),
                   jax.ShapeDtypeStruct((HY_ORDER, 1, HY_W), F32)),
        grid=(L // tk,),
        in_specs=[
            full((L, POS_PAD)), _slab(w1.shape, l), _slab(b1.shape, l), _slab(w2.shape, l),
            _slab(b2.shape, l), _slab(w3.shape, l), full((L, half)), full((L, 1)),
            pl.BlockSpec((tk, L), lambda j: (j, 0)), pl.BlockSpec((tk, L), lambda j: (j, 0)),
        ],
        out_specs=(pl.BlockSpec((tk, half), lambda j: (j, 0)), pl.BlockSpec((tk, half), lambda j: (j, 0)),
                   full((HY_ORDER, 1, HY_W))),
        scratch_shapes=[pltpu.VMEM((L, half), BF16), pltpu.VMEM((L, half), BF16)],
        compiler_params=_params(("arbitrary",)),
        name="hyena_filter_spectrum_%d" % L,
    )(feats, w1, b1, w2, b2, w3, decay, alt, ctab, stab)


def _hyconv_kernel(hy_ref, cw_ref, cb_ref, skip_ref, hre_ref, him_ref, nyq_ref, alt_ref, c_ref, s_ref, o_ref,
                   x1_sc, x2_sc, v_sc, vb_sc, yre_sc, yim_sc, nv_sc, *, n_seq, L, tk):
    p = pl.program_id(0)
    ph = pl.program_id(1)
    j = pl.program_id(2)
    cols = [slice(b * HY_W, (b + 1) * HY_W) for b in range(n_seq)]
    blk = pl.ds(pl.multiple_of(j * tk, tk), tk)

    @pl.when((p == 0) & (ph == 0) & (j == 0))
    def _():
        rows = lax.broadcasted_iota(jnp.int32, (L, 1), 0)
        w = cw_ref[...]
        for b in range(n_seq):
            x = hy_ref[b * L:(b + 1) * L, :]
            prev = jnp.where(rows > 0, pltpu.roll(x, 1, 0), 0.0)
            nxt = jnp.where(rows < L - 1, pltpu.roll(x, L - 1, 0), 0.0)
            u = prev * w[0:1] + x * w[1:2] + nxt * w[2:3] + cb_ref[...]
            x1_sc[:, cols[b]] = u[:, :HY_W]
            x2_sc[:, cols[b]] = u[:, HY_W:2 * HY_W]
            v_sc[:, cols[b]] = u[:, 2 * HY_W:]

    @pl.when((ph == 0) & (j == 0))
    def _():
        v = v_sc[...]
        vb_sc[...] = v.astype(BF16)
        nv_sc[...] = jnp.sum(v * alt_ref[...], axis=0, keepdims=True)

    @pl.when(ph == 0)
    def _():
        vb = vb_sc[...]
        v_re = _dot(c_ref[...], vb)
        v_im = -_dot(s_ref[...], vb)
        freq = j * tk + lax.broadcasted_iota(jnp.int32, (tk, 1), 0)
        wk = jnp.where(freq == 0, 0.5, 1.0)
        h_re = hre_ref[...] * wk
        h_im = him_ref[...] * wk
        for b in range(n_seq):
            yre_sc[blk, cols[b]] = (v_re[:, cols[b]] * h_re - v_im[:, cols[b]] * h_im).astype(BF16)
            yim_sc[blk, cols[b]] = (v_re[:, cols[b]] * h_im + v_im[:, cols[b]] * h_re).astype(BF16)

    @pl.when(ph == 1)
    def _():
        acc = _dot(c_ref[...], yre_sc[...]) - _dot(s_ref[...], yim_sc[...])
        alt = alt_ref[blk, :]
        for b in range(n_seq):
            v = v_sc[blk, cols[b]]
            nyq = nv_sc[:, cols[b]] * nyq_ref[...]
            y = acc[:, cols[b]] * (1.0 / L) + (0.5 / L) * alt * nyq + v * skip_ref[...]

            @pl.when(p == 0)
            def _():
                v_sc[blk, cols[b]] = x1_sc[blk, cols[b]] * y

            @pl.when(p == 1)
            def _():
                o_ref[pl.ds(pl.multiple_of(b * L + j * tk, tk), tk), :] = (x2_sc[blk, cols[b]] * y).astype(BF16)


def _hyconv_call(hy, l, conv_w, conv_b, skip, hre, him, nyq, alt, ctab, stab, *, n_seq, L, tok_off):
    tk = min(L, 512)
    nk = L // tk
    seg = tok_off // (n_seq * L)
    width = n_seq * HY_W

    def full(shape):
        return pl.BlockSpec(shape, lambda p, ph, j: (0,) * len(shape))

    def filt(p, ph, j):
        return (jnp.where(ph == 0, j, nk - 1), p)

    vm = lambda dtype: pltpu.VMEM((L, width), dtype)
    return pl.pallas_call(
        functools.partial(_hyconv_kernel, n_seq=n_seq, L=L, tk=tk),
        out_shape=jax.ShapeDtypeStruct((n_seq * L, HY_W), BF16),
        grid=(HY_ORDER, 2, nk),
        in_specs=[
            pl.BlockSpec((n_seq * L, 3 * HY_W), lambda p, ph, j: (seg, 0), pipeline_mode=pl.Buffered(1)),
            _slab(conv_w.shape, l), _slab(conv_b.shape, l),
            pl.BlockSpec((None, None, 1, HY_W), lambda p, ph, j: (l, p, 0, 0)),
            pl.BlockSpec((tk, HY_W), filt), pl.BlockSpec((tk, HY_W), filt),
            pl.BlockSpec((None, 1, HY_W), lambda p, ph, j: (p, 0, 0)),
            full((L, 1)),
            pl.BlockSpec((tk, L), lambda p, ph, j: (j, 0)), pl.BlockSpec((tk, L), lambda p, ph, j: (j, 0)),
        ],
        out_specs=pl.BlockSpec((n_seq * L, HY_W), lambda p, ph, j: (0, 0)),
        scratch_shapes=[vm(F32), vm(F32), vm(F32), vm(BF16), vm(BF16), vm(BF16), pltpu.VMEM((1, width), F32)],
        compiler_params=_params(("arbitrary", "arbitrary", "arbitrary")),
        name="hyena_long_conv_%d" % L,
    )(hy, conv_w, conv_b, skip, hre, him, nyq, alt, ctab, stab)


def _fnet_kernel(fcs_ref, c_ref, s_ref, o_ref, *, n_seq, L, tk, scale):
    j = pl.program_id(0)
    c = c_ref[...]
    s = s_ref[...]
    for b in range(n_seq):
        seq = slice(b * L, (b + 1) * L)
        y = (_dot(c, fcs_ref[seq, :FN_W]) - _dot(s, fcs_ref[seq, FN_W:])) * scale
        o_ref[pl.ds(pl.multiple_of(b * L + j * tk, tk), tk), :] = y.astype(BF16)


def _fnet_call(fcs, ctab, stab, *, n_seq, L, tok_off):
    tk = min(L, 512)
    seg = tok_off // (n_seq * L)
    return pl.pallas_call(
        functools.partial(_fnet_kernel, n_seq=n_seq, L=L, tk=tk, scale=1.0 / math.sqrt(L * FN_GROUP_W)),
        out_shape=jax.ShapeDtypeStruct((n_seq * L, FN_W), BF16),
        grid=(L // tk,),
        in_specs=[
            pl.BlockSpec((n_seq * L, 2 * FN_W), lambda j: (seg, 0)),
            pl.BlockSpec((tk, L), lambda j: (j, 0)), pl.BlockSpec((tk, L), lambda j: (j, 0)),
        ],
        out_specs=pl.BlockSpec((n_seq * L, FN_W), lambda j: (0, 0)),
        compiler_params=_params(("arbitrary",)),
        name="fnet_position_dft_%d" % L,
    )(fcs, ctab, stab)


def _mixout_kernel(h_ref, mod_ref, ng_ref, wg_ref, zc_ref, zl_ref, ac_ref, al_ref, fc_ref, fl_ref,
                   wa_ref, wb_ref, wc_ref, wo_ref, o_ref):
    i = pl.program_id(0)
    is_ctx = i < T_CTX // TM_OUT
    m = mod_ref[_group_of(i * TM_OUT)]
    for sub in range(TM_OUT // OUT_SUB_ROWS):
        rows = slice(sub * OUT_SUB_ROWS, (sub + 1) * OUT_SUB_ROWS)
        h = h_ref[rows, :]
        n = (_rms(h, ng_ref[...]) * (1.0 + m[4:5]) + m[3:4]).astype(BF16)
        z = jnp.where(is_ctx, zc_ref[rows, :], zl_ref[rows, :])
        a = jnp.where(is_ctx, ac_ref[rows, :], al_ref[rows, :])
        f = jnp.where(is_ctx, fc_ref[rows, :], fl_ref[rows, :])
        acc = _sigmoid(_dot_nt(n, wg_ref[0:D_MODEL, :])) * _dot(z, wa_ref[...])
        acc = acc + _sigmoid(_dot_nt(n, wg_ref[D_MODEL:2 * D_MODEL, :])) * _dot(a, wb_ref[...])
        acc = acc + _sigmoid(_dot_nt(n, wg_ref[2 * D_MODEL:, :])) * _dot(f, wc_ref[...])
        y = _dot(acc.astype(BF16), wo_ref[...])
        o_ref[rows, :] = h + m[5:6] * y


def _mixout_call(h, mods, ng, w_gate, l, z, a, f, wa, wb, wc, wo):
    tm = TM_OUT
    n_ctx = T_CTX // tm
    weight = functools.partial(_slab, single_buffer=True)

    def full(shape):
        return pl.BlockSpec(shape, lambda i: (0,) * len(shape))

    def tok(width):
        return pl.BlockSpec((tm, width), lambda i: (i, 0))

    def pair(width):
        return [pl.BlockSpec((tm, width), lambda i: (jnp.minimum(i, n_ctx - 1), 0)),
                pl.BlockSpec((tm, width), lambda i: (jnp.maximum(i - n_ctx, 0), 0))]

    return pl.pallas_call(
        _mixout_kernel,
        out_shape=jax.ShapeDtypeStruct((T_ALL, D_MODEL), F32),
        grid=(T_ALL // tm,),
        in_specs=[
            tok(D_MODEL), _slab(mods.shape, l), _slab(ng.shape, l, 1), weight(w_gate.shape, l),
            *pair(HY_W), *pair(N_HEADS * V_DIM), *pair(FN_W),
            weight(wa.shape, l), weight(wb.shape, l), weight(wc.shape, l), weight(wo.shape, l),
        ],
        out_specs=tok(D_MODEL),
        compiler_params=_params(("parallel",)),
        name="gated_merge_out_proj",
    )(h, mods, ng, w_gate, *z, *a, *f, wa, wb, wc, wo)


def _wprep_kernel(w_ref, mix_ref, gate_ref):
    r = pl.program_id(1)
    n_gate = N_BRANCH * D_MODEL
    g0 = IN_GATE - PREP_ROWS
    last = n_gate - (3 * PREP_ROWS - IN_GATE)

    @pl.when(r == 0)
    def _():
        mix_ref[:MIX_FN, :] = w_ref[...].astype(BF16)

    @pl.when(r == 1)
    def _():
        mix_ref[MIX_FN:MIX_KR, :] = w_ref[ROPE_DIM:g0, :].astype(BF16)
        mix_ref[MIX_KR:MIX_KR + ROPE_DIM, :] = w_ref[0:ROPE_DIM, :].astype(BF16)
        mix_ref[MIX_KR + ROPE_DIM:, :] = jnp.zeros((MIX_W - MIX_KR - ROPE_DIM, D_MODEL), BF16)
        gate_ref[:PREP_ROWS - g0, :] = w_ref[g0:, :].astype(BF16)

    @pl.when(r == 2)
    def _():
        gate_ref[PREP_ROWS - g0:2 * PREP_ROWS - g0, :] = w_ref[...].astype(BF16)

    @pl.when(r == 3)
    def _():
        gate_ref[2 * PREP_ROWS - g0:, :] = w_ref[:last, :].astype(BF16)


def _wprep_call(w_in_t):
    assert IN_KR == PREP_ROWS and 4 * PREP_ROWS >= IN_COLS
    return pl.pallas_call(
        _wprep_kernel,
        out_shape=(jax.ShapeDtypeStruct((DEPTH, MIX_W, D_MODEL), BF16),
                   jax.ShapeDtypeStruct((DEPTH, N_BRANCH * D_MODEL, D_MODEL), BF16)),
        grid=(DEPTH, 4),
        in_specs=[pl.BlockSpec((None, PREP_ROWS, D_MODEL), lambda l, r: (l, r, 0))],
        out_specs=(pl.BlockSpec((None, MIX_W, D_MODEL), lambda l, r: (l, 0, 0)),
                   pl.BlockSpec((None, N_BRANCH * D_MODEL, D_MODEL), lambda l, r: (l, 0, 0))),
        compiler_params=_params(("arbitrary", "arbitrary")),
        name="input_proj_weight_relayout",
    )(w_in_t)


def _head_slot_weights(w_qb, w_kvb):
    wq3 = w_qb.reshape(DEPTH, Q_LORA, N_HEADS, NOPE_DIM + ROPE_DIM)
    slot_pad = HEAD_SLOT - NOPE_DIM - ROPE_DIM
    no_pad = ((0, 0),) * 3
    wq = jnp.pad(wq3, no_pad + ((0, slot_pad),)).reshape(DEPTH, Q_LORA, QK_W).astype(BF16)
    rope = wq3[..., NOPE_DIM:].reshape(DEPTH, Q_LORA, N_HEADS, 2, 2, ROPE_HALF)
    partner = jnp.stack([-rope[..., 1, :], rope[..., 0, :]], axis=4).reshape(DEPTH, Q_LORA, N_HEADS, ROPE_DIM)
    wqs = jnp.pad(partner, no_pad + ((NOPE_DIM, slot_pad),)).reshape(DEPTH, Q_LORA, QK_W).astype(BF16)

    wkv3 = w_kvb.reshape(DEPTH, KV_LORA, N_HEADS, NOPE_DIM + V_DIM)
    wk = jnp.pad(wkv3[..., :NOPE_DIM], no_pad + ((0, HEAD_SLOT - NOPE_DIM),))
    wk = wk.reshape(DEPTH, KV_LORA, QK_W).astype(BF16)
    wv = wkv3[..., NOPE_DIM:].reshape(DEPTH, KV_LORA, N_HEADS * V_DIM).astype(BF16)
    return wq, wqs, wk, wv


def kernel(x_prompt, x_sample, cache_ckv, cache_krope, c, c_ctx, w_ada, b_ada, norm_g, w_ffn_up, w_ffn_down,
           w_in, hy_conv_w, hy_conv_b, hy_filt_w1, hy_filt_b1, hy_filt_w2, hy_filt_b2, hy_filt_w3, hy_skip,
           w_hy_out, q_norm_g, w_qb, kv_norm_g, w_kvb, w_mla_o, w_fnet, w_out, final_g):
    tabs = _tables()
    cosq, sinq, cosk, sink = (jnp.asarray(t) for t in tabs["rope"])
    fch = jnp.asarray(tabs["fnch"]).astype(BF16)
    ek = jnp.asarray(tabs["ropeexp"]).astype(BF16)
    dft = {}
    for L in (SEQ, DEC_SEQ):
        dft[("hy", L)] = tuple(jnp.asarray(t).astype(BF16) for t in tabs[("hy", L)])
        dft[("fn", L)] = tuple(jnp.asarray(t).astype(BF16) for t in tabs[("fn", L)])

    cvec = jnp.concatenate([c_ctx[None, :], c, jnp.zeros((SUBLANES - N_GROUPS, D_MODEL), F32)], axis=0).T
    ada = _ada_call(cvec, w_ada, b_ada)
    mods = ada[:, :N_GROUPS].reshape(DEPTH, N_GROUPS, 9, D_MODEL)
    ng = norm_g.reshape(DEPTH, 3, 1, D_MODEL)
    qg = q_norm_g.reshape(DEPTH, 1, Q_LORA)
    kvg = kv_norm_g.reshape(DEPTH, 1, KV_LORA)
    wq, wqs, wk, wv = _head_slot_weights(w_qb, w_kvb)
    w1 = jnp.pad(hy_filt_w1, ((0, 0), (0, POS_PAD - POS_EMB), (0, 0)))
    b1 = hy_filt_b1.reshape(DEPTH, 1, FILT_HID)
    b2 = hy_filt_b2.reshape(DEPTH, 1, FILT_HID)
    conv_b = hy_conv_b.reshape(DEPTH, 1, 3 * HY_W)
    skip = hy_skip.reshape(DEPTH, HY_ORDER, 1, HY_W)

    w_mix, w_gate = _wprep_call(jnp.swapaxes(w_in, 1, 2))
    w_hy_out, w_mla_o, w_fnet, w_out = (w.astype(BF16) for w in (w_hy_out, w_mla_o, w_fnet, w_out))
    hs = (x_prompt.reshape(T_CTX, D_MODEL), x_sample.reshape(T_LAT, D_MODEL))
    segs = ((BATCH, SEQ, 0), (DEC_BATCH, DEC_SEQ, T_CTX))
    ckv_out = []
    kr_out = []
    for l in range(DEPTH):
        h = _ffn_call(hs, mods, ng, w_ffn_up, w_ffn_down, l, 0, 0)

        hy, q, k, v, ckv, k_r, fcs = _mixin_call(
            h, mods, ng, w_mix, l, qg, kvg, wq, wqs, wk, wv, fch, cosq, sinq, cosk, sink)
        ckv_out.append(ckv.reshape(BATCH, SEQ, KV_LORA))
        kr_out.append(k_r.reshape(BATCH, SEQ, ROPE_DIM))

        z_parts, a_parts, f_parts = [], [], []
        for n_seq, L, off in segs:
            feats, decay, alt = (jnp.asarray(t) for t in tabs[("filt", L)])
            c_hy, s_hy = dft[("hy", L)]
            hre, him, nyq = _hyfilt_call(L, l, feats, w1, b1, hy_filt_w2, b2, hy_filt_w3, decay, alt, c_hy, s_hy)
            z_parts.append(_hyconv_call(hy, l, hy_conv_w, conv_b, skip, hre, him, nyq, alt, c_hy, s_hy,
                                        n_seq=n_seq, L=L, tok_off=off))
            if off:
                a_parts.append(_attn_call(q, k, v, (cache_ckv, cache_krope, l, wk, wv, ek),
                                          n_seq=n_seq, seq_len=L, tok_off=off, tq=ATTN_TQ))
            else:
                a_parts.append(_attn_call(q, k, v, None, n_seq=n_seq, seq_len=L, tok_off=off, tq=L,
                                          group=ATTN_GROUP))
            c_fn, s_fn = dft[("fn", L)]
            f_parts.append(_fnet_call(fcs, c_fn, s_fn, n_seq=n_seq, L=L, tok_off=off))
        h = _mixout_call(h, mods, ng, w_gate, l, z_parts, a_parts, f_parts, w_hy_out, w_mla_o, w_fnet, w_out)

        last = l == DEPTH - 1
        out = _ffn_call((h,), mods, ng, w_ffn_up, w_ffn_down, l, 1, 6, final_g[None, :] if last else None)
        hs = out if last else (out,)

    y_prompt = hs[0].reshape(BATCH, SEQ, D_MODEL)
    y_sample = hs[1].reshape(DEC_BATCH, DEC_SEQ, D_MODEL)
    return y_prompt, y_sample, jnp.stack(ckv_out, axis=1), jnp.stack(kr_out, axis=1)
```

```python
import functools
import math

import numpy as np
import jax
import jax.numpy as jnp
from jax import lax
from jax.experimental import pallas as pl
from jax.experimental.pallas import tpu as pltpu

F32 = jnp.float32
BF16 = jnp.bfloat16
HIGHEST = lax.Precision.HIGHEST

D_MODEL = 1024
BATCH = 16
SEQ = 256
DEPTH = 2
DEC_BATCH = 2
DEC_SEQ = 2048
PAST_LEN = 512
GRID_W = 64
HY_W = 256
HY_ORDER = 2
N_BANDS = 8
POS_EMB = 1 + 2 * N_BANDS
FILT_HID = 64
N_HEADS = 8
Q_LORA = 256
KV_LORA = 128
NOPE_DIM = 64
ROPE_DIM = 32
V_DIM = 64
ROPE_BASE = 10000.0
FN_GROUPS = 4
FN_GROUP_W = 64
FN_W = FN_GROUPS * FN_GROUP_W
N_BRANCH = 3
D_FF = 2816
EPS = 1e-6

T_CTX = BATCH * SEQ
T_LAT = DEC_BATCH * DEC_SEQ
T_ALL = T_CTX + T_LAT
N_GROUPS = 1 + DEC_BATCH
LANES = 128
SUBLANES = 8
VMEM_BYTES = 64 * 1024 * 1024
VMEM_LIMIT = VMEM_BYTES - 8 * 1024 * 1024

HEAD_SLOT = LANES
QK_W = N_HEADS * HEAD_SLOT
PAIR_W = 2 * V_DIM
V_EXT_W = N_HEADS * PAIR_W
POS_PAD = LANES
assert PAIR_W == LANES and NOPE_DIM + ROPE_DIM <= HEAD_SLOT

TM_FFN = 512
TF_FFN = 256
FFN_LOAD = 8
FFN_UP_ROWS = D_MODEL // FFN_LOAD
FFN_DN_ROWS = D_FF // FFN_LOAD
TM_FFN_WIDE = 1024
FFN_SUB_ROWS = 256
TM_MIX = 512
TM_OUT = 1024
OUT_SUB_ROWS = 256
ADA_TN = 2304
ATTN_TQ = 512
ATTN_GROUP = 4
PREP_ROWS = 1152

IN_KR = 3 * HY_W + Q_LORA + KV_LORA
IN_FN = IN_KR + ROPE_DIM
IN_GATE = IN_FN + FN_W
IN_COLS = IN_GATE + N_BRANCH * D_MODEL

MIX_HY = 0
MIX_QA = 3 * HY_W
MIX_KVA = MIX_QA + Q_LORA
MIX_FN = MIX_KVA + KV_LORA
MIX_KR = MIX_FN + FN_W
MIX_W = MIX_KR + LANES
ROPE_HALF = ROPE_DIM // 4


def _dft_tables(L, half):
    k = np.arange(L, dtype=np.int64)
    period = 2 * L if half else L
    m = (k[:, None] * k[None, :]) % period
    ang = 2.0 * np.pi * m.astype(np.float64) / period
    return np.cos(ang).astype(np.float32), np.sin(ang).astype(np.float32)


def _filter_tables(L):
    t = np.arange(L, dtype=np.float64)
    t_norm = t / (L - 1)
    w = 2.0 * np.pi * t / L
    bands = np.linspace(1e-4, N_BANDS - 1, N_BANDS)
    ang = w[:, None] * bands[None, :]
    feats = np.concatenate([t_norm[:, None], np.cos(ang), -np.sin(ang)], axis=-1)
    feats = np.pad(feats, ((0, 0), (0, POS_PAD - POS_EMB)))
    deltas = np.linspace(math.log(1e-2) / 1.5, math.log(1e-2) / 0.3, HY_W)
    decay = np.exp(-t_norm[:, None] * np.abs(deltas)[None, :])
    decay = np.concatenate([decay, decay], axis=1)
    alt = np.where(np.arange(L) % 2 == 0, 1.0, -1.0)[:, None]
    return feats.astype(np.float32), decay.astype(np.float32), alt.astype(np.float32)


def _rope_tables():
    t = np.arange(DEC_SEQ)
    row = (t // GRID_W).astype(np.float64)
    col = (t % GRID_W).astype(np.float64)
    nf = ROPE_DIM // 4
    inv = ROPE_BASE ** (-np.arange(nf, dtype=np.float64) / nf)
    ar = row[:, None] * inv[None, :]
    ac = col[:, None] * inv[None, :]
    cos32 = np.concatenate([np.cos(ar), np.cos(ar), np.cos(ac), np.cos(ac)], axis=1)
    sin32 = np.concatenate([np.sin(ar), np.sin(ar), np.sin(ac), np.sin(ac)], axis=1)
    cos32 = np.concatenate([cos32, np.ones((TM_MIX, ROPE_DIM))], axis=0)
    sin32 = np.concatenate([sin32, np.zeros((TM_MIX, ROPE_DIM))], axis=0)
    n = cos32.shape[0]
    cosq = np.ones((n, HEAD_SLOT))
    sinq = np.zeros((n, HEAD_SLOT))
    cosq[:, NOPE_DIM:NOPE_DIM + ROPE_DIM] = cos32
    sinq[:, NOPE_DIM:NOPE_DIM + ROPE_DIM] = sin32
    cosk = np.zeros((n, HEAD_SLOT))
    sink = np.zeros((n, HEAD_SLOT))
    cosk[:, :ROPE_DIM] = cos32
    sink[:, :ROPE_DIM] = sin32
    return tuple(t.astype(np.float32) for t in (cosq, sinq, cosk, sink))


def _fnet_channel_table():
    j = np.arange(FN_GROUP_W)
    ang = 2.0 * np.pi * ((j[:, None] * j[None, :]) % FN_GROUP_W) / FN_GROUP_W
    out = np.zeros((FN_W, 2 * FN_W))
    for g in range(FN_GROUPS):
        sl = slice(g * FN_GROUP_W, (g + 1) * FN_GROUP_W)
        out[sl, sl] = np.cos(ang)
        out[sl, FN_W + g * FN_GROUP_W:FN_W + (g + 1) * FN_GROUP_W] = np.sin(ang)
    return out.astype(np.float32)


def _rope_expand_table():
    e = np.zeros((ROPE_DIM, N_HEADS, HEAD_SLOT), np.float32)
    for j in range(ROPE_DIM):
        e[j, :, NOPE_DIM + j] = 1.0
    return e.reshape(ROPE_DIM, QK_W)


_TABLES = {}


def _tables():
    if not _TABLES:
        for L in (SEQ, DEC_SEQ):
            _TABLES[("hy", L)] = _dft_tables(L, True)
            _TABLES[("fn", L)] = _dft_tables(L, False)
            _TABLES[("filt", L)] = _filter_tables(L)
        _TABLES["rope"] = _rope_tables()
        _TABLES["fnch"] = _fnet_channel_table()
        _TABLES["ropeexp"] = _rope_expand_table()
    return _TABLES


def _rms(x, g):
    ms = jnp.mean(x * x, axis=-1, keepdims=True)
    return x * lax.rsqrt(ms + EPS) * g


def _sigmoid(x):
    return 1.0 / (1.0 + jnp.exp(-x))


def _dot(a, b):
    return jnp.dot(a, b, preferred_element_type=F32)


def _dot_hi(a, b):
    return jnp.dot(a, b, precision=HIGHEST, preferred_element_type=F32)


def _dot_x3(a, b):
    a_hi = a.astype(BF16)
    b_hi = b.astype(BF16)
    a_lo = (a - a_hi.astype(F32)).astype(BF16)
    b_lo = (b - b_hi.astype(F32)).astype(BF16)
    return _dot(a_hi, b_hi) + (_dot(a_hi, b_lo) + _dot(a_lo, b_hi))


def _dot_nt(a, b):
    return lax.dot_general(a, b, (((1,), (1,)), ((), ())), preferred_element_type=F32)


def _store_values(v_ref, v):
    ones = jnp.ones((v.shape[0], PAIR_W), BF16)
    for pair in range(N_HEADS // 2):
        v_ref[:, 2 * pair * PAIR_W:(2 * pair + 1) * PAIR_W] = v[:, pair * PAIR_W:(pair + 1) * PAIR_W].astype(BF16)
        v_ref[:, (2 * pair + 1) * PAIR_W:(2 * pair + 2) * PAIR_W] = ones


def _group_of(tok0):
    return jnp.where(tok0 < T_CTX, 0, 1 + (tok0 - T_CTX) // DEC_SEQ)


def _slab(shape, *lead, single_buffer=False):
    tail = tuple(shape[len(lead):])
    mode = dict(pipeline_mode=pl.Buffered(1)) if single_buffer else {}
    return pl.BlockSpec((None,) * len(lead) + tail, lambda *_: tuple(lead) + (0,) * len(tail), **mode)


def _params(sem):
    return pltpu.CompilerParams(dimension_semantics=sem, vmem_limit_bytes=VMEM_LIMIT)


def _ada_kernel(c_ref, w_ref, b_ref, o_ref):
    x = c_ref[...]
    s = x * _sigmoid(x)
    w = w_ref[...]
    o_ref[...] = jnp.zeros_like(o_ref)
    for g in range(N_GROUPS):
        o_ref[g:g + 1, :] = jnp.sum(w * s[:, g:g + 1], axis=0, keepdims=True) + b_ref[...]


def _ada_call(cvec, w_ada, b_ada):
    tn = ADA_TN
    n_out = 9 * D_MODEL
    return pl.pallas_call(
        _ada_kernel,
        out_shape=jax.ShapeDtypeStruct((DEPTH, SUBLANES, n_out), F32),
        grid=(DEPTH, n_out // tn),
        in_specs=[
            pl.BlockSpec((D_MODEL, SUBLANES), lambda l, j: (0, 0)),
            pl.BlockSpec((None, D_MODEL, tn), lambda l, j: (l, 0, j)),
            pl.BlockSpec((None, 1, tn), lambda l, j: (l, 0, j)),
        ],
        out_specs=pl.BlockSpec((None, SUBLANES, tn), lambda l, j: (l, 0, j)),
        compiler_params=_params(("arbitrary", "arbitrary")),
        name="ada_modulation",
    )(cvec, w_ada, b_ada.reshape(DEPTH, 1, n_out))


def _ffn_kernel(*refs, j0, split_in, final, tm):
    refs = list(refs)
    x_refs = [refs.pop(0) for _ in range(2 if split_in else 1)]
    mod_ref, ng_ref, wup_ref, wd_ref = refs[:4]
    refs = refs[4:]
    fg_ref = refs.pop(0) if final else None
    o_refs = [refs.pop(0) for _ in range(2 if final else 1)]
    wup_sc, wd_sc, hid_sc = refs
    s = pl.program_id(0)
    n_ctx_tiles = T_CTX // tm

    @pl.when(s < FFN_LOAD)
    def _():
        wup_sc[pl.ds(pl.multiple_of(s * FFN_UP_ROWS, FFN_UP_ROWS), FFN_UP_ROWS), :] = wup_ref[...].astype(BF16)
        wd_sc[pl.ds(pl.multiple_of(s * FFN_DN_ROWS, FFN_DN_ROWS), FFN_DN_ROWS), :] = wd_ref[...].astype(BF16)

    @pl.when(s >= FFN_LOAD)
    def _():
        t = s - FFN_LOAD
        m = mod_ref[_group_of(t * tm)]
        ys = []
        for sub in range(tm // FFN_SUB_ROWS):
            rows = slice(sub * FFN_SUB_ROWS, (sub + 1) * FFN_SUB_ROWS)
            if split_in:
                x = jnp.where(t < n_ctx_tiles, x_refs[0][rows, :], x_refs[1][rows, :])
            else:
                x = x_refs[0][rows, :]
            n = (_rms(x, ng_ref[...]) * (1.0 + m[j0 + 1:j0 + 2]) + m[j0:j0 + 1]).astype(BF16)
            for c in range(D_FF // TF_FFN):
                cols = slice(c * TF_FFN, (c + 1) * TF_FFN)
                g = _dot(n, wup_sc[:, cols])
                u = _dot(n, wup_sc[:, D_FF + c * TF_FFN:D_FF + (c + 1) * TF_FFN])
                hid_sc[rows, cols] = (g * _sigmoid(g) * u).astype(BF16)
            y = x + 0.5 * m[j0 + 2:j0 + 3] * _dot(hid_sc[rows, :], wd_sc[...])
            if final:
                y = _rms(y, fg_ref[...])
            ys.append((rows, y))
        if final:
            @pl.when(t < n_ctx_tiles)
            def _():
                for rows, y in ys:
                    o_refs[0][rows, :] = y

            @pl.when(t >= n_ctx_tiles)
            def _():
                for rows, y in ys:
                    o_refs[1][rows, :] = y
        else:
            for rows, y in ys:
                o_refs[0][rows, :] = y


def _ffn_call(xs, mods, ng, w_up, w_down, l, f, j0, final_g=None):
    split_in = len(xs) == 2
    final = final_g is not None
    tm = TM_FFN if (split_in or final) else TM_FFN_WIDE
    n_ctx_tiles = T_CTX // tm
    tile = lambda s: jnp.maximum(s - FFN_LOAD, 0)
    chunk = lambda s: jnp.minimum(s, FFN_LOAD - 1)
    ctx_blk = lambda s: (jnp.minimum(tile(s), n_ctx_tiles - 1), 0)
    lat_blk = lambda s: (jnp.maximum(tile(s) - n_ctx_tiles, 0), 0)
    row = pl.BlockSpec((1, D_MODEL), lambda s: (0, 0))
    tok = lambda index_map: pl.BlockSpec((tm, D_MODEL), index_map)
    if split_in:
        in_specs = [tok(ctx_blk), tok(lat_blk)]
    else:
        in_specs = [tok(lambda s: (tile(s), 0))]
    in_specs += [
        _slab(mods.shape, l), _slab(ng.shape, l, j0 // 3),
        pl.BlockSpec((None, None, FFN_UP_ROWS, 2 * D_FF), lambda s: (l, f, chunk(s), 0)),
        pl.BlockSpec((None, None, FFN_DN_ROWS, D_MODEL), lambda s: (l, f, chunk(s), 0)),
    ]
    args = list(xs) + [mods, ng, w_up, w_down]
    if final:
        in_specs.append(row)
        args.append(final_g)
        out_shape = (jax.ShapeDtypeStruct((T_CTX, D_MODEL), F32), jax.ShapeDtypeStruct((T_LAT, D_MODEL), F32))
        out_specs = (tok(ctx_blk), tok(lat_blk))
    else:
        out_shape = jax.ShapeDtypeStruct((T_ALL, D_MODEL), F32)
        out_specs = tok(lambda s: (tile(s), 0))
    return pl.pallas_call(
        functools.partial(_ffn_kernel, j0=j0, split_in=split_in, final=final, tm=tm),
        out_shape=out_shape,
        grid=(FFN_LOAD + T_ALL // tm,),
        in_specs=in_specs,
        out_specs=out_specs,
        scratch_shapes=[pltpu.VMEM((D_MODEL, 2 * D_FF), BF16), pltpu.VMEM((D_FF, D_MODEL), BF16),
                        pltpu.VMEM((tm, D_FF), BF16)],
        compiler_params=_params(("arbitrary",)),
        name="swiglu_half_step",
    )(*args)


def _rope_partner(x):
    lane = lax.broadcasted_iota(jnp.int32, (1, HEAD_SLOT), 1)
    first = (lane % (2 * ROPE_HALF)) < ROPE_HALF
    return jnp.where(first, -pltpu.roll(x, HEAD_SLOT - ROPE_HALF, 1), pltpu.roll(x, ROPE_HALF, 1))


def _mixin_kernel(h_ref, mod_ref, ng_ref, w_ref, qg_ref, kvg_ref, wq_ref, wqs_ref, wk_ref, wv_ref,
                  fch_ref, cq_ref, sq_ref, ck_ref, sk_ref,
                  hy_ref, q_ref, k_ref, v_ref, ckv_ref, kr_ref, fcs_ref):
    i = pl.program_id(0)
    m = mod_ref[_group_of(i * TM_MIX)]
    n = (_rms(h_ref[...], ng_ref[...]) * (1.0 + m[4:5]) + m[3:4]).astype(BF16)
    proj = _dot_nt(n, w_ref[...])
    hy_ref[...] = proj[:, MIX_HY:MIX_QA]
    q_a = proj[:, MIX_QA:MIX_KVA]
    kv_a = proj[:, MIX_KVA:MIX_FN]
    fn = proj[:, MIX_FN:MIX_KR]
    k_r = proj[:, MIX_KR:MIX_W]

    qn = _rms(q_a, qg_ref[...]).astype(BF16)
    q = _dot(qn, wq_ref[...])
    q_partner = _dot(qn, wqs_ref[...])
    cos_q = cq_ref[...]
    sin_q = sq_ref[...]
    for h in range(N_HEADS):
        hs = slice(h * HEAD_SLOT, (h + 1) * HEAD_SLOT)
        q_ref[:, hs] = (q[:, hs] * cos_q + q_partner[:, hs] * sin_q).astype(BF16)

    ckv = _rms(kv_a, kvg_ref[...])

    @pl.when(i < T_CTX // TM_MIX)
    def _():
        ckv_ref[...] = ckv
        kr_ref[...] = k_r[:, :ROPE_DIM]

    ckv_b = ckv.astype(BF16)
    k_rot = k_r * ck_ref[...] + _rope_partner(k_r) * sk_ref[...]
    k_rope = pltpu.roll(k_rot, NOPE_DIM, 1)
    k_nope = _dot(ckv_b, wk_ref[...])
    for h in range(N_HEADS):
        hs = slice(h * HEAD_SLOT, (h + 1) * HEAD_SLOT)
        k_ref[:, hs] = (k_nope[:, hs] + k_rope).astype(BF16)
    _store_values(v_ref, _dot(ckv_b, wv_ref[...]))
    fcs_ref[...] = _dot(fn.astype(BF16), fch_ref[...]).astype(BF16)


def _mixin_call(h, mods, ng, w_mix, l, qg, kvg, wq, wqs, wk, wv, fch, cosq, sinq, cosk, sink):
    tm = TM_MIX

    def pos_block(i):
        tok0 = i * tm
        return jnp.where(tok0 < T_CTX, DEC_SEQ // tm, ((tok0 - T_CTX) % DEC_SEQ) // tm)

    def full(shape):
        return pl.BlockSpec(shape, lambda i: (0,) * len(shape))

    def tok(width):
        return pl.BlockSpec((tm, width), lambda i: (i, 0))

    def pos(width):
        return pl.BlockSpec((tm, width), lambda i: (pos_block(i), 0))

    out_shape = (
        jax.ShapeDtypeStruct((T_ALL, 3 * HY_W), F32),
        jax.ShapeDtypeStruct((T_ALL, QK_W), BF16),
        jax.ShapeDtypeStruct((T_ALL, QK_W), BF16),
        jax.ShapeDtypeStruct((T_ALL, V_EXT_W), BF16),
        jax.ShapeDtypeStruct((T_CTX, KV_LORA), F32),
        jax.ShapeDtypeStruct((T_CTX, ROPE_DIM), F32),
        jax.ShapeDtypeStruct((T_ALL, 2 * FN_W), BF16),
    )
    return pl.pallas_call(
        _mixin_kernel,
        out_shape=out_shape,
        grid=(T_ALL // tm,),
        in_specs=[
            tok(D_MODEL), _slab(mods.shape, l), _slab(ng.shape, l, 1), _slab(w_mix.shape, l),
            _slab(qg.shape, l), _slab(kvg.shape, l), _slab(wq.shape, l), _slab(wqs.shape, l),
            _slab(wk.shape, l), _slab(wv.shape, l),
            full((FN_W, 2 * FN_W)), pos(HEAD_SLOT), pos(HEAD_SLOT), pos(HEAD_SLOT), pos(HEAD_SLOT),
        ],
        out_specs=tuple(
            tok(s.shape[1]) if s.shape[0] == T_ALL else
            pl.BlockSpec((tm, s.shape[1]), lambda i: (jnp.minimum(i, T_CTX // tm - 1), 0))
            for s in out_shape),
        compiler_params=_params(("arbitrary",)),
        name="mixer_input_proj",
    )(h, mods, ng, w_mix, qg, kvg, wq, wqs, wk, wv, fch, cosq, sinq, cosk, sink)


def _attn_kernel(q_ref, k_ref, v_ref, *rest, has_cache, group, seq_len):
    if has_cache:
        cckv_ref, ckr_ref, wk_ref, wv_ref, ek_ref, o_ref, kc_sc, vc_sc = rest

        @pl.when(pl.program_id(1) == 0)
        def _():
            ckv_b = cckv_ref[...].astype(BF16)
            kc_sc[...] = (_dot(ckv_b, wk_ref[...]) + _dot(ckr_ref[...].astype(BF16), ek_ref[...])).astype(BF16)
            _store_values(vc_sc, _dot(ckv_b, wv_ref[...]))
    else:
        (o_ref,) = rest
    scale2 = math.log2(math.e) / math.sqrt(NOPE_DIM + ROPE_DIM)
    first_half = lax.broadcasted_iota(jnp.int32, (1, PAIR_W), 1) < V_DIM
    for sb in range(group):
        rows = slice(sb * seq_len, (sb + 1) * seq_len) if group > 1 else slice(None)
        for pair in range(N_HEADS // 2):
            vs = slice(2 * pair * PAIR_W, (2 * pair + 2) * PAIR_W)
            outs = []
            for e in range(2):
                hs = slice((2 * pair + e) * HEAD_SLOT, (2 * pair + e + 1) * HEAD_SLOT)
                qh = q_ref[rows, hs]
                s = _dot_nt(qh, k_ref[rows, hs])
                mx = jnp.max(s, axis=-1, keepdims=True)
                if has_cache:
                    sc = _dot_nt(qh, kc_sc[:, hs])
                    mx = jnp.maximum(mx, jnp.max(sc, axis=-1, keepdims=True))
                p = jnp.exp2((s - mx) * scale2)
                if has_cache:
                    o = _dot(p.astype(BF16), v_ref[rows, vs])
                    o = o + _dot(jnp.exp2((sc - mx) * scale2).astype(BF16), vc_sc[:, vs])
                    outs.append(o[:, :PAIR_W] / o[:, PAIR_W:PAIR_W + 1])
                else:
                    o = _dot(p.astype(BF16), v_ref[rows, vs.start:vs.start + PAIR_W])
                    outs.append(o / jnp.sum(p, axis=-1, keepdims=True))
            o_ref[rows, pair * PAIR_W:(pair + 1) * PAIR_W] = jnp.where(first_half, outs[0], outs[1]).astype(BF16)


def _attn_call(q, k, v, cache, *, n_seq, seq_len, tok_off, tq, group=1):
    has_cache = cache is not None
    assert group == 1 or (tq == seq_len and not has_cache)
    nq = seq_len // tq
    kv_rows = group * seq_len
    q_rows = group * tq
    kv_blk0 = tok_off // kv_rows
    q_blk0 = tok_off // q_rows
    vw = N_HEADS * V_DIM
    in_specs = [
        pl.BlockSpec((q_rows, QK_W), lambda b, j: (q_blk0 + b * nq + j, 0)),
        pl.BlockSpec((kv_rows, QK_W), lambda b, j: (kv_blk0 + b, 0)),
        pl.BlockSpec((kv_rows, V_EXT_W), lambda b, j: (kv_blk0 + b, 0)),
    ]
    args = [q, k, v]
    scratch = []
    if has_cache:
        cache_ckv, cache_krope, l, wk, wv, ek = cache
        full = lambda shape: pl.BlockSpec(shape, lambda b, j: (0,) * len(shape))
        in_specs += [pl.BlockSpec((None, None, PAST_LEN, KV_LORA), lambda b, j: (b, l, 0, 0)),
                     pl.BlockSpec((None, None, PAST_LEN, ROPE_DIM), lambda b, j: (b, l, 0, 0)),
                     _slab(wk.shape, l), _slab(wv.shape, l), full((ROPE_DIM, QK_W))]
        args += [cache_ckv, cache_krope, wk, wv, ek]
        scratch = [pltpu.VMEM((PAST_LEN, QK_W), BF16), pltpu.VMEM((PAST_LEN, V_EXT_W), BF16)]
    return pl.pallas_call(
        functools.partial(_attn_kernel, has_cache=has_cache, group=group, seq_len=seq_len),
        out_shape=jax.ShapeDtypeStruct((n_seq * seq_len, vw), BF16),
        grid=(n_seq // group, nq),
        in_specs=in_specs,
        out_specs=pl.BlockSpec((q_rows, vw), lambda b, j: (b * nq + j, 0)),
        scratch_shapes=scratch,
        compiler_params=_params(("arbitrary", "arbitrary")),
        name="mla_attention_cache" if has_cache else "mla_attention",
    )(*args)


def _hyfilt_kernel(feat_ref, w1_ref, b1_ref, w2_ref, b2_ref, w3_ref, dec_ref, alt_ref, c_ref, s_ref,
                   hre_ref, him_ref, nyq_ref, a_sc, b_sc, *, L):
    j = pl.program_id(0)
    half = HY_ORDER * HY_W

    @pl.when(j == 0)
    def _():
        w1 = w1_ref[...]
        w2 = w2_ref[...]
        w3 = w3_ref[...]
        zero = jnp.zeros_like(w2)
        w2_pair = jnp.concatenate([jnp.concatenate([w2, zero], axis=1), jnp.concatenate([zero, w2], axis=1)], axis=0)
        b1_pair = jnp.concatenate([b1_ref[...], b1_ref[...]], axis=1)
        b2_pair = jnp.concatenate([b2_ref[...], b2_ref[...]], axis=1)
        h = jnp.concatenate([_dot_hi(feat_ref[:L // 2, :], w1), _dot_hi(feat_ref[L // 2:, :], w1)], axis=1)
        h = jnp.sin(h + b1_pair)
        h = jnp.sin(_dot_hi(h, w2_pair) + b2_pair)
        h = jnp.concatenate([_dot_x3(h[:, :FILT_HID], w3), _dot_x3(h[:, FILT_HID:], w3)], axis=0)
        dec = dec_ref[...]
        rows = lax.broadcasted_iota(jnp.int32, (L, 1), 0)
        h_fwd = h[:, :half] * dec
        h_bwd = jnp.where(rows > 0, h[:, half:] * dec, 0.0)
        norm = (jnp.sum(jnp.abs(h_fwd), axis=0, keepdims=True)
                + jnp.sum(jnp.abs(h_bwd), axis=0, keepdims=True) + EPS)
        inv = 1.0 / norm
        a = (h_fwd + h_bwd) * inv
        b = (h_fwd - h_bwd) * inv
        a_sc[...] = a.astype(BF16)
        b_sc[...] = b.astype(BF16)
        nyq = jnp.sum(a * alt_ref[...], axis=0, keepdims=True)
        for o in range(HY_ORDER):
            nyq_ref[o] = nyq[:, o * HY_W:(o + 1) * HY_W]

    hre_ref[...] = _dot(c_ref[...], a_sc[...])
    him_ref[...] = -_dot(s_ref[...], b_sc[...])


def _hyfilt_call(L, l, feats, w1, b1, w2, b2, w3, decay, alt, ctab, stab):
    tk = min(L, 512)
    half = HY_ORDER * HY_W

    def full(shape):
        return pl.BlockSpec(shape, lambda j: (0,) * len(shape))

    return pl.pallas_call(
        functools.partial(_hyfilt_kernel, L=L),
        out_shape=(jax.ShapeDtypeStruct((L, half), F32), jax.ShapeDtypeStruct((L, half), F32),
                   jax.ShapeDtypeStruct((HY_ORDER, 1, HY_W), F32)),
        grid=(L // tk,),
        in_specs=[
            full((L, POS_PAD)), _slab(w1.shape, l), _slab(b1.shape, l), _slab(w2.shape, l),
            _slab(b2.shape, l), _slab(w3.shape, l), full((L, half)), full((L, 1)),
            pl.BlockSpec((tk, L), lambda j: (j, 0)), pl.BlockSpec((tk, L), lambda j: (j, 0)),
        ],
        out_specs=(pl.BlockSpec((tk, half), lambda j: (j, 0)), pl.BlockSpec((tk, half), lambda j: (j, 0)),
                   full((HY_ORDER, 1, HY_W))),
        scratch_shapes=[pltpu.VMEM((L, half), BF16), pltpu.VMEM((L, half), BF16)],
        compiler_params=_params(("arbitrary",)),
        name="hyena_filter_spectrum_%d" % L,
    )(feats, w1, b1, w2, b2, w3, decay, alt, ctab, stab)


def _hyconv_kernel(hy_ref, cw_ref, cb_ref, skip_ref, hre_ref, him_ref, nyq_ref, alt_ref, c_ref, s_ref, o_ref,
                   x1_sc, x2_sc, v_sc, vb_sc, yre_sc, yim_sc, nv_sc, *, n_seq, L, tk):
    p = pl.program_id(0)
    ph = pl.program_id(1)
    j = pl.program_id(2)
    cols = [slice(b * HY_W, (b + 1) * HY_W) for b in range(n_seq)]
    blk = pl.ds(pl.multiple_of(j * tk, tk), tk)

    @pl.when((p == 0) & (ph == 0) & (j == 0))
    def _():
        rows = lax.broadcasted_iota(jnp.int32, (L, 1), 0)
        w = cw_ref[...]
        for b in range(n_seq):
            x = hy_ref[b * L:(b + 1) * L, :]
            prev = jnp.where(rows > 0, pltpu.roll(x, 1, 0), 0.0)
            nxt = jnp.where(rows < L - 1, pltpu.roll(x, L - 1, 0), 0.0)
            u = prev * w[0:1] + x * w[1:2] + nxt * w[2:3] + cb_ref[...]
            x1_sc[:, cols[b]] = u[:, :HY_W]
            x2_sc[:, cols[b]] = u[:, HY_W:2 * HY_W]
            v_sc[:, cols[b]] = u[:, 2 * HY_W:]

    @pl.when((ph == 0) & (j == 0))
    def _():
        v = v_sc[...]
        vb_sc[...] = v.astype(BF16)
        nv_sc[...] = jnp.sum(v * alt_ref[...], axis=0, keepdims=True)

    @pl.when(ph == 0)
    def _():
        vb = vb_sc[...]
        v_re = _dot(c_ref[...], vb)
        v_im = -_dot(s_ref[...], vb)
        freq = j * tk + lax.broadcasted_iota(jnp.int32, (tk, 1), 0)
        wk = jnp.where(freq == 0, 0.5, 1.0)
        h_re = hre_ref[...] * wk
        h_im = him_ref[...] * wk
        for b in range(n_seq):
            yre_sc[blk, cols[b]] = (v_re[:, cols[b]] * h_re - v_im[:, cols[b]] * h_im).astype(BF16)
            yim_sc[blk, cols[b]] = (v_re[:, cols[b]] * h_im + v_im[:, cols[b]] * h_re).astype(BF16)

    @pl.when(ph == 1)
    def _():
        acc = _dot(c_ref[...], yre_sc[...]) - _dot(s_ref[...], yim_sc[...])
        alt = alt_ref[blk, :]
        for b in range(n_seq):
            v = v_sc[blk, cols[b]]
            nyq = nv_sc[:, cols[b]] * nyq_ref[...]
            y = acc[:, cols[b]] * (1.0 / L) + (0.5 / L) * alt * nyq + v * skip_ref[...]

            @pl.when(p == 0)
            def _():
                v_sc[blk, cols[b]] = x1_sc[blk, cols[b]] * y

            @pl.when(p == 1)
            def _():
                o_ref[pl.ds(pl.multiple_of(b * L + j * tk, tk), tk), :] = (x2_sc[blk, cols[b]] * y).astype(BF16)


def _hyconv_call(hy, l, conv_w, conv_b, skip, hre, him, nyq, alt, ctab, stab, *, n_seq, L, tok_off):
    tk = min(L, 512)
    nk = L // tk
    seg = tok_off // (n_seq * L)
    width = n_seq * HY_W

    def full(shape):
        return pl.BlockSpec(shape, lambda p, ph, j: (0,) * len(shape))

    def filt(p, ph, j):
        return (jnp.where(ph == 0, j, nk - 1), p)

    vm = lambda dtype: pltpu.VMEM((L, width), dtype)
    return pl.pallas_call(
        functools.partial(_hyconv_kernel, n_seq=n_seq, L=L, tk=tk),
        out_shape=jax.ShapeDtypeStruct((n_seq * L, HY_W), BF16),
        grid=(HY_ORDER, 2, nk),
        in_specs=[
            pl.BlockSpec((n_seq * L, 3 * HY_W), lambda p, ph, j: (seg, 0), pipeline_mode=pl.Buffered(1)),
            _slab(conv_w.shape, l), _slab(conv_b.shape, l),
            pl.BlockSpec((None, None, 1, HY_W), lambda p, ph, j: (l, p, 0, 0)),
            pl.BlockSpec((tk, HY_W), filt), pl.BlockSpec((tk, HY_W), filt),
            pl.BlockSpec((None, 1, HY_W), lambda p, ph, j: (p, 0, 0)),
            full((L, 1)),
            pl.BlockSpec((tk, L), lambda p, ph, j: (j, 0)), pl.BlockSpec((tk, L), lambda p, ph, j: (j, 0)),
        ],
        out_specs=pl.BlockSpec((n_seq * L, HY_W), lambda p, ph, j: (0, 0)),
        scratch_shapes=[vm(F32), vm(F32), vm(F32), vm(BF16), vm(BF16), vm(BF16), pltpu.VMEM((1, width), F32)],
        compiler_params=_params(("arbitrary", "arbitrary", "arbitrary")),
        name="hyena_long_conv_%d" % L,
    )(hy, conv_w, conv_b, skip, hre, him, nyq, alt, ctab, stab)


def _fnet_kernel(fcs_ref, c_ref, s_ref, o_ref, *, n_seq, L, tk, scale):
    j = pl.program_id(0)
    c = c_ref[...]
    s = s_ref[...]
    for b in range(n_seq):
        seq = slice(b * L, (b + 1) * L)
        y = (_dot(c, fcs_ref[seq, :FN_W]) - _dot(s, fcs_ref[seq, FN_W:])) * scale
        o_ref[pl.ds(pl.multiple_of(b * L + j * tk, tk), tk), :] = y.astype(BF16)


def _fnet_call(fcs, ctab, stab, *, n_seq, L, tok_off):
    tk = min(L, 512)
    seg = tok_off // (n_seq * L)
    return pl.pallas_call(
        functools.partial(_fnet_kernel, n_seq=n_seq, L=L, tk=tk, scale=1.0 / math.sqrt(L * FN_GROUP_W)),
        out_shape=jax.ShapeDtypeStruct((n_seq * L, FN_W), BF16),
        grid=(L // tk,),
        in_specs=[
            pl.BlockSpec((n_seq * L, 2 * FN_W), lambda j: (seg, 0)),
            pl.BlockSpec((tk, L), lambda j: (j, 0)), pl.BlockSpec((tk, L), lambda j: (j, 0)),
        ],
        out_specs=pl.BlockSpec((n_seq * L, FN_W), lambda j: (0, 0)),
        compiler_params=_params(("arbitrary",)),
        name="fnet_position_dft_%d" % L,
    )(fcs, ctab, stab)


def _mixout_kernel(h_ref, mod_ref, ng_ref, wg_ref, zc_ref, zl_ref, ac_ref, al_ref, fc_ref, fl_ref,
                   wa_ref, wb_ref, wc_ref, wo_ref, o_ref):
    i = pl.program_id(0)
    is_ctx = i < T_CTX // TM_OUT
    m = mod_ref[_group_of(i * TM_OUT)]
    for sub in range(TM_OUT // OUT_SUB_ROWS):
        rows = slice(sub * OUT_SUB_ROWS, (sub + 1) * OUT_SUB_ROWS)
        h = h_ref[rows, :]
        n = (_rms(h, ng_ref[...]) * (1.0 + m[4:5]) + m[3:4]).astype(BF16)
        z = jnp.where(is_ctx, zc_ref[rows, :], zl_ref[rows, :])
        a = jnp.where(is_ctx, ac_ref[rows, :], al_ref[rows, :])
        f = jnp.where(is_ctx, fc_ref[rows, :], fl_ref[rows, :])
        acc = _sigmoid(_dot_nt(n, wg_ref[0:D_MODEL, :])) * _dot(z, wa_ref[...])
        acc = acc + _sigmoid(_dot_nt(n, wg_ref[D_MODEL:2 * D_MODEL, :])) * _dot(a, wb_ref[...])
        acc = acc + _sigmoid(_dot_nt(n, wg_ref[2 * D_MODEL:, :])) * _dot(f, wc_ref[...])
        y = _dot(acc.astype(BF16), wo_ref[...])
        o_ref[rows, :] = h + m[5:6] * y


def _mixout_call(h, mods, ng, w_gate, l, z, a, f, wa, wb, wc, wo):
    tm = TM_OUT
    n_ctx = T_CTX // tm
    weight = functools.partial(_slab, single_buffer=True)

    def full(shape):
        return pl.BlockSpec(shape, lambda i: (0,) * len(shape))

    def tok(width):
        return pl.BlockSpec((tm, width), lambda i: (i, 0))

    def pair(width):
        return [pl.BlockSpec((tm, width), lambda i: (jnp.minimum(i, n_ctx - 1), 0)),
                pl.BlockSpec((tm, width), lambda i: (jnp.maximum(i - n_ctx, 0), 0))]

    return pl.pallas_call(
        _mixout_kernel,
        out_shape=jax.ShapeDtypeStruct((T_ALL, D_MODEL), F32),
        grid=(T_ALL // tm,),
        in_specs=[
            tok(D_MODEL), _slab(mods.shape, l), _slab(ng.shape, l, 1), weight(w_gate.shape, l),
            *pair(HY_W), *pair(N_HEADS * V_DIM), *pair(FN_W),
            weight(wa.shape, l), weight(wb.shape, l), weight(wc.shape, l), weight(wo.shape, l),
        ],
        out_specs=tok(D_MODEL),
        compiler_params=_params(("parallel",)),
        name="gated_merge_out_proj",
    )(h, mods, ng, w_gate, *z, *a, *f, wa, wb, wc, wo)


def _wprep_kernel(w_ref, mix_ref, gate_ref):
    r = pl.program_id(1)
    n_gate = N_BRANCH * D_MODEL
    g0 = IN_GATE - PREP_ROWS
    last = n_gate - (3 * PREP_ROWS - IN_GATE)

    @pl.when(r == 0)
    def _():
        mix_ref[:MIX_FN, :] = w_ref[...].astype(BF16)

    @pl.when(r == 1)
    def _():
        mix_ref[MIX_FN:MIX_KR, :] = w_ref[ROPE_DIM:g0, :].astype(BF16)
        mix_ref[MIX_KR:MIX_KR + ROPE_DIM, :] = w_ref[0:ROPE_DIM, :].astype(BF16)
        mix_ref[MIX_KR + ROPE_DIM:, :] = jnp.zeros((MIX_W - MIX_KR - ROPE_DIM, D_MODEL), BF16)
        gate_ref[:PREP_ROWS - g0, :] = w_ref[g0:, :].astype(BF16)

    @pl.when(r == 2)
    def _():
        gate_ref[PREP_ROWS - g0:2 * PREP_ROWS - g0, :] = w_ref[...].astype(BF16)

    @pl.when(r == 3)
    def _():
        gate_ref[2 * PREP_ROWS - g0:, :] = w_ref[:last, :].astype(BF16)


def _wprep_call(w_in_t):
    assert IN_KR == PREP_ROWS and 4 * PREP_ROWS >= IN_COLS
    return pl.pallas_call(
        _wprep_kernel,
        out_shape=(jax.ShapeDtypeStruct((DEPTH, MIX_W, D_MODEL), BF16),
                   jax.ShapeDtypeStruct((DEPTH, N_BRANCH * D_MODEL, D_MODEL), BF16)),
        grid=(DEPTH, 4),
        in_specs=[pl.BlockSpec((None, PREP_ROWS, D_MODEL), lambda l, r: (l, r, 0))],
        out_specs=(pl.BlockSpec((None, MIX_W, D_MODEL), lambda l, r: (l, 0, 0)),
                   pl.BlockSpec((None, N_BRANCH * D_MODEL, D_MODEL), lambda l, r: (l, 0, 0))),
        compiler_params=_params(("arbitrary", "arbitrary")),
        name="input_proj_weight_relayout",
    )(w_in_t)


def _head_slot_weights(w_qb, w_kvb):
    wq3 = w_qb.reshape(DEPTH, Q_LORA, N_HEADS, NOPE_DIM + ROPE_DIM)
    slot_pad = HEAD_SLOT - NOPE_DIM - ROPE_DIM
    no_pad = ((0, 0),) * 3
    wq = jnp.pad(wq3, no_pad + ((0, slot_pad),)).reshape(DEPTH, Q_LORA, QK_W).astype(BF16)
    rope = wq3[..., NOPE_DIM:].reshape(DEPTH, Q_LORA, N_HEADS, 2, 2, ROPE_HALF)
    partner = jnp.stack([-rope[..., 1, :], rope[..., 0, :]], axis=4).reshape(DEPTH, Q_LORA, N_HEADS, ROPE_DIM)
    wqs = jnp.pad(partner, no_pad + ((NOPE_DIM, slot_pad),)).reshape(DEPTH, Q_LORA, QK_W).astype(BF16)

    wkv3 = w_kvb.reshape(DEPTH, KV_LORA, N_HEADS, NOPE_DIM + V_DIM)
    wk = jnp.pad(wkv3[..., :NOPE_DIM], no_pad + ((0, HEAD_SLOT - NOPE_DIM),))
    wk = wk.reshape(DEPTH, KV_LORA, QK_W).astype(BF16)
    wv = wkv3[..., NOPE_DIM:].reshape(DEPTH, KV_LORA, N_HEADS * V_DIM).astype(BF16)
    return wq, wqs, wk, wv


def kernel(x_prompt, x_sample, cache_ckv, cache_krope, c, c_ctx, w_ada, b_ada, norm_g, w_ffn_up, w_ffn_down,
           w_in, hy_conv_w, hy_conv_b, hy_filt_w1, hy_filt_b1, hy_filt_w2, hy_filt_b2, hy_filt_w3, hy_skip,
           w_hy_out, q_norm_g, w_qb, kv_norm_g, w_kvb, w_mla_o, w_fnet, w_out, final_g):
    tabs = _tables()
    cosq, sinq, cosk, sink = (jnp.asarray(t) for t in tabs["rope"])
    fch = jnp.asarray(tabs["fnch"]).astype(BF16)
    ek = jnp.asarray(tabs["ropeexp"]).astype(BF16)
    dft = {}
    for L in (SEQ, DEC_SEQ):
        dft[("hy", L)] = tuple(jnp.asarray(t).astype(BF16) for t in tabs[("hy", L)])
        dft[("fn", L)] = tuple(jnp.asarray(t).astype(BF16) for t in tabs[("fn", L)])

    cvec = jnp.concatenate([c_ctx[None, :], c, jnp.zeros((SUBLANES - N_GROUPS, D_MODEL), F32)], axis=0).T
    ada = _ada_call(cvec, w_ada, b_ada)
    mods = ada[:, :N_GROUPS].reshape(DEPTH, N_GROUPS, 9, D_MODEL)
    ng = norm_g.reshape(DEPTH, 3, 1, D_MODEL)
    qg = q_norm_g.reshape(DEPTH, 1, Q_LORA)
    kvg = kv_norm_g.reshape(DEPTH, 1, KV_LORA)
    wq, wqs, wk, wv = _head_slot_weights(w_qb, w_kvb)
    w1 = jnp.pad(hy_filt_w1, ((0, 0), (0, POS_PAD - POS_EMB), (0, 0)))
    b1 = hy_filt_b1.reshape(DEPTH, 1, FILT_HID)
    b2 = hy_filt_b2.reshape(DEPTH, 1, FILT_HID)
    conv_b = hy_conv_b.reshape(DEPTH, 1, 3 * HY_W)
    skip = hy_skip.reshape(DEPTH, HY_ORDER, 1, HY_W)

    w_mix, w_gate = _wprep_call(jnp.swapaxes(w_in, 1, 2))
    w_hy_out, w_mla_o, w_fnet, w_out = (w.astype(BF16) for w in (w_hy_out, w_mla_o, w_fnet, w_out))
    hs = (x_prompt.reshape(T_CTX, D_MODEL), x_sample.reshape(T_LAT, D_MODEL))
    segs = ((BATCH, SEQ, 0), (DEC_BATCH, DEC_SEQ, T_CTX))
    ckv_out = []
    kr_out = []
    for l in range(DEPTH):
        h = _ffn_call(hs, mods, ng, w_ffn_up, w_ffn_down, l, 0, 0)

        hy, q, k, v, ckv, k_r, fcs = _mixin_call(
            h, mods, ng, w_mix, l, qg, kvg, wq, wqs, wk, wv, fch, cosq, sinq, cosk, sink)
        ckv_out.append(ckv.reshape(BATCH, SEQ, KV_LORA))
        kr_out.append(k_r.reshape(BATCH, SEQ, ROPE_DIM))

        z_parts, a_parts, f_parts = [], [], []
        for n_seq, L, off in segs:
            feats, decay, alt = (jnp.asarray(t) for t in tabs[("filt", L)])
            c_hy, s_hy = dft[("hy", L)]
            hre, him, nyq = _hyfilt_call(L, l, feats, w1, b1, hy_filt_w2, b2, hy_filt_w3, decay, alt, c_hy, s_hy)
            z_parts.append(_hyconv_call(hy, l, hy_conv_w, conv_b, skip, hre, him, nyq, alt, c_hy, s_hy,
                                        n_seq=n_seq, L=L, tok_off=off))
            if off:
                a_parts.append(_attn_call(q, k, v, (cache_ckv, cache_krope, l, wk, wv, ek),
                                          n_seq=n_seq, seq_len=L, tok_off=off, tq=ATTN_TQ))
            else:
                a_parts.append(_attn_call(q, k, v, None, n_seq=n_seq, seq_len=L, tok_off=off, tq=L,
                                          group=ATTN_GROUP))
            c_fn, s_fn = dft[("fn", L)]
            f_parts.append(_fnet_call(fcs, c_fn, s_fn, n_seq=n_seq, L=L, tok_off=off))
        h = _mixout_call(h, mods, ng, w_gate, l, z_parts, a_parts, f_parts, w_hy_out, w_mla_o, w_fnet, w_out)

        last = l == DEPTH - 1
        out = _ffn_call((h,), mods, ng, w_ffn_up, w_ffn_down, l, 1, 6, final_g[None, :] if last else None)
        hs = out if last else (out,)

    y_prompt = hs[0].reshape(BATCH, SEQ, D_MODEL)
    y_sample = hs[1].reshape(DEC_BATCH, DEC_SEQ, D_MODEL)
    return y_prompt, y_sample, jnp.stack(ckv_out, axis=1), jnp.stack(kr_out, axis=1)
```

```python
import functools
import math

import numpy as np
import jax
import jax.numpy as jnp
from jax import lax
from jax.experimental import pallas as pl
from jax.experimental.pallas import tpu as pltpu

F32 = jnp.float32
BF16 = jnp.bfloat16
HIGHEST = lax.Precision.HIGHEST

D_MODEL = 1024
BATCH = 16
SEQ = 256
DEPTH = 2
DEC_BATCH = 2
DEC_SEQ = 2048
PAST_LEN = 512
GRID_W = 64
HY_W = 256
HY_ORDER = 2
N_BANDS = 8
POS_EMB = 1 + 2 * N_BANDS
FILT_HID = 64
N_HEADS = 8
Q_LORA = 256
KV_LORA = 128
NOPE_DIM = 64
ROPE_DIM = 32
V_DIM = 64
ROPE_BASE = 10000.0
FN_GROUPS = 4
FN_GROUP_W = 64
FN_W = FN_GROUPS * FN_GROUP_W
N_BRANCH = 3
D_FF = 2816
EPS = 1e-6

T_CTX = BATCH * SEQ
T_LAT = DEC_BATCH * DEC_SEQ
T_ALL = T_CTX + T_LAT
N_GROUPS = 1 + DEC_BATCH
LANES = 128
SUBLANES = 8
VMEM_BYTES = 64 * 1024 * 1024
VMEM_LIMIT = VMEM_BYTES - 8 * 1024 * 1024

HEAD_SLOT = LANES
QK_W = N_HEADS * HEAD_SLOT
PAIR_W = 2 * V_DIM
V_EXT_W = N_HEADS * PAIR_W
POS_PAD = LANES
assert PAIR_W == LANES and NOPE_DIM + ROPE_DIM <= HEAD_SLOT

TM_FFN = 512
TF_FFN = 256
FFN_LOAD = 8
FFN_UP_ROWS = D_MODEL // FFN_LOAD
FFN_DN_ROWS = D_FF // FFN_LOAD
TM_FFN_WIDE = 1024
FFN_SUB_ROWS = 256
TM_MIX = 512
TM_OUT = 1024
OUT_SUB_ROWS = 256
ADA_TN = 2304
ATTN_TQ = 512
ATTN_GROUP = 4
PREP_ROWS = 1152

IN_KR = 3 * HY_W + Q_LORA + KV_LORA
IN_FN = IN_KR + ROPE_DIM
IN_GATE = IN_FN + FN_W
IN_COLS = IN_GATE + N_BRANCH * D_MODEL
GATE_OFF = IN_GATE - PREP_ROWS
GATE_EXT_ROWS = IN_COLS - PREP_ROWS

MIX_HY = 0
MIX_QA = 3 * HY_W
MIX_KVA = MIX_QA + Q_LORA
MIX_FN = MIX_KVA + KV_LORA
MIX_KR = MIX_FN + FN_W
MIX_W = MIX_KR + LANES
ROPE_HALF = ROPE_DIM // 4


def _dft_tables(L, half):
    k = np.arange(L, dtype=np.int64)
    period = 2 * L if half else L
    m = (k[:, None] * k[None, :]) % period
    ang = 2.0 * np.pi * m.astype(np.float64) / period
    return np.cos(ang).astype(np.float32), np.sin(ang).astype(np.float32)


def _filter_tables(L):
    t = np.arange(L, dtype=np.float64)
    t_norm = t / (L - 1)
    w = 2.0 * np.pi * t / L
    bands = np.linspace(1e-4, N_BANDS - 1, N_BANDS)
    ang = w[:, None] * bands[None, :]
    feats = np.concatenate([t_norm[:, None], np.cos(ang), -np.sin(ang)], axis=-1)
    feats = np.pad(feats, ((0, 0), (0, POS_PAD - POS_EMB)))
    deltas = np.linspace(math.log(1e-2) / 1.5, math.log(1e-2) / 0.3, HY_W)
    decay = np.exp(-t_norm[:, None] * np.abs(deltas)[None, :])
    decay = np.concatenate([decay, decay], axis=1)
    alt = np.where(np.arange(L) % 2 == 0, 1.0, -1.0)[:, None]
    return feats.astype(np.float32), decay.astype(np.float32), alt.astype(np.float32)


def _rope_tables():
    t = np.arange(DEC_SEQ)
    row = (t // GRID_W).astype(np.float64)
    col = (t % GRID_W).astype(np.float64)
    nf = ROPE_DIM // 4
    inv = ROPE_BASE ** (-np.arange(nf, dtype=np.float64) / nf)
    ar = row[:, None] * inv[None, :]
    ac = col[:, None] * inv[None, :]
    cos32 = np.concatenate([np.cos(ar), np.cos(ar), np.cos(ac), np.cos(ac)], axis=1)
    sin32 = np.concatenate([np.sin(ar), np.sin(ar), np.sin(ac), np.sin(ac)], axis=1)
    cos32 = np.concatenate([cos32, np.ones((TM_MIX, ROPE_DIM))], axis=0)
    sin32 = np.concatenate([sin32, np.zeros((TM_MIX, ROPE_DIM))], axis=0)
    n = cos32.shape[0]
    cosq = np.ones((n, HEAD_SLOT))
    sinq = np.zeros((n, HEAD_SLOT))
    cosq[:, NOPE_DIM:NOPE_DIM + ROPE_DIM] = cos32
    sinq[:, NOPE_DIM:NOPE_DIM + ROPE_DIM] = sin32
    cosk = np.zeros((n, HEAD_SLOT))
    sink = np.zeros((n, HEAD_SLOT))
    cosk[:, :ROPE_DIM] = cos32
    sink[:, :ROPE_DIM] = sin32
    return tuple(t.astype(np.float32) for t in (cosq, sinq, cosk, sink))


def _fnet_channel_table():
    j = np.arange(FN_GROUP_W)
    ang = 2.0 * np.pi * ((j[:, None] * j[None, :]) % FN_GROUP_W) / FN_GROUP_W
    out = np.zeros((FN_W, 2 * FN_W))
    for g in range(FN_GROUPS):
        sl = slice(g * FN_GROUP_W, (g + 1) * FN_GROUP_W)
        out[sl, sl] = np.cos(ang)
        out[sl, FN_W + g * FN_GROUP_W:FN_W + (g + 1) * FN_GROUP_W] = np.sin(ang)
    return out.astype(np.float32)


def _rope_expand_table():
    e = np.zeros((ROPE_DIM, N_HEADS, HEAD_SLOT), np.float32)
    for j in range(ROPE_DIM):
        e[j, :, NOPE_DIM + j] = 1.0
    return e.reshape(ROPE_DIM, QK_W)


_TABLES = {}


def _tables():
    if not _TABLES:
        for L in (SEQ, DEC_SEQ):
            _TABLES[("hy", L)] = _dft_tables(L, True)
            _TABLES[("fn", L)] = _dft_tables(L, False)
            _TABLES[("filt", L)] = _filter_tables(L)
        _TABLES["rope"] = _rope_tables()
        _TABLES["fnch"] = _fnet_channel_table()
        _TABLES["ropeexp"] = _rope_expand_table()
    return _TABLES


def _rms(x, g):
    ms = jnp.mean(x * x, axis=-1, keepdims=True)
    return x * lax.rsqrt(ms + EPS) * g


def _sigmoid(x):
    return 1.0 / (1.0 + jnp.exp(-x))


def _dot(a, b):
    return jnp.dot(a, b, preferred_element_type=F32)


def _dot_hi(a, b):
    return jnp.dot(a, b, precision=HIGHEST, preferred_element_type=F32)


def _dot_x3(a, b):
    a_hi = a.astype(BF16)
    b_hi = b.astype(BF16)
    a_lo = (a - a_hi.astype(F32)).astype(BF16)
    b_lo = (b - b_hi.astype(F32)).astype(BF16)
    return _dot(a_hi, b_hi) + (_dot(a_hi, b_lo) + _dot(a_lo, b_hi))


def _dot_nt(a, b):
    return lax.dot_general(a, b, (((1,), (1,)), ((), ())), preferred_element_type=F32)


def _store_values(v_ref, v):
    ones = jnp.ones((v.shape[0], PAIR_W), BF16)
    for pair in range(N_HEADS // 2):
        v_ref[:, 2 * pair * PAIR_W:(2 * pair + 1) * PAIR_W] = v[:, pair * PAIR_W:(pair + 1) * PAIR_W].astype(BF16)
        v_ref[:, (2 * pair + 1) * PAIR_W:(2 * pair + 2) * PAIR_W] = ones


def _group_of(tok0):
    return jnp.where(tok0 < T_CTX, 0, 1 + (tok0 - T_CTX) // DEC_SEQ)


def _slab(shape, *lead, single_buffer=False):
    tail = tuple(shape[len(lead):])
    mode = dict(pipeline_mode=pl.Buffered(1)) if single_buffer else {}
    return pl.BlockSpec((None,) * len(lead) + tail, lambda *_: tuple(lead) + (0,) * len(tail), **mode)


def _params(sem):
    return pltpu.CompilerParams(dimension_semantics=sem, vmem_limit_bytes=VMEM_LIMIT)


def _ada_kernel(c_ref, w_ref, b_ref, o_ref):
    x = c_ref[...]
    s = x * _sigmoid(x)
    w = w_ref[...]
    o_ref[...] = jnp.zeros_like(o_ref)
    for g in range(N_GROUPS):
        o_ref[g:g + 1, :] = jnp.sum(w * s[:, g:g + 1], axis=0, keepdims=True) + b_ref[...]


def _ada_call(cvec, w_ada, b_ada):
    tn = ADA_TN
    n_out = 9 * D_MODEL
    return pl.pallas_call(
        _ada_kernel,
        out_shape=jax.ShapeDtypeStruct((DEPTH, SUBLANES, n_out), F32),
        grid=(DEPTH, n_out // tn),
        in_specs=[
            pl.BlockSpec((D_MODEL, SUBLANES), lambda l, j: (0, 0)),
            pl.BlockSpec((None, D_MODEL, tn), lambda l, j: (l, 0, j)),
            pl.BlockSpec((None, 1, tn), lambda l, j: (l, 0, j)),
        ],
        out_specs=pl.BlockSpec((None, SUBLANES, tn), lambda l, j: (l, 0, j)),
        compiler_params=_params(("arbitrary", "arbitrary")),
        name="ada_modulation",
    )(cvec, w_ada, b_ada.reshape(DEPTH, 1, n_out))


def _ffn_kernel(*refs, j0, split_in, final, tm):
    refs = list(refs)
    x_refs = [refs.pop(0) for _ in range(2 if split_in else 1)]
    mod_ref, ng_ref, wup_ref, wd_ref = refs[:4]
    refs = refs[4:]
    fg_ref = refs.pop(0) if final else None
    o_refs = [refs.pop(0) for _ in range(2 if final else 1)]
    wup_sc, wd_sc, hid_sc = refs
    s = pl.program_id(0)
    n_ctx_tiles = T_CTX // tm

    @pl.when(s < FFN_LOAD)
    def _():
        wup_sc[pl.ds(pl.multiple_of(s * FFN_UP_ROWS, FFN_UP_ROWS), FFN_UP_ROWS), :] = wup_ref[...].astype(BF16)
        wd_sc[pl.ds(pl.multiple_of(s * FFN_DN_ROWS, FFN_DN_ROWS), FFN_DN_ROWS), :] = wd_ref[...].astype(BF16)

    @pl.when(s >= FFN_LOAD)
    def _():
        t = s - FFN_LOAD
        m = mod_ref[_group_of(t * tm)]
        ys = []
        for sub in range(tm // FFN_SUB_ROWS):
            rows = slice(sub * FFN_SUB_ROWS, (sub + 1) * FFN_SUB_ROWS)
            if split_in:
                x = jnp.where(t < n_ctx_tiles, x_refs[0][rows, :], x_refs[1][rows, :])
            else:
                x = x_refs[0][rows, :]
            n = (_rms(x, ng_ref[...]) * (1.0 + m[j0 + 1:j0 + 2]) + m[j0:j0 + 1]).astype(BF16)
            for c in range(D_FF // TF_FFN):
                cols = slice(c * TF_FFN, (c + 1) * TF_FFN)
                g = _dot(n, wup_sc[:, cols])
                u = _dot(n, wup_sc[:, D_FF + c * TF_FFN:D_FF + (c + 1) * TF_FFN])
                hid_sc[rows, cols] = (g * _sigmoid(g) * u).astype(BF16)
            y = x + 0.5 * m[j0 + 2:j0 + 3] * _dot(hid_sc[rows, :], wd_sc[...])
            if final:
                y = _rms(y, fg_ref[...])
            ys.append((rows, y))
        if final:
            @pl.when(t < n_ctx_tiles)
            def _():
                for rows, y in ys:
                    o_refs[0][rows, :] = y

            @pl.when(t >= n_ctx_tiles)
            def _():
                for rows, y in ys:
                    o_refs[1][rows, :] = y
        else:
            for rows, y in ys:
                o_refs[0][rows, :] = y


def _ffn_call(xs, mods, ng, w_up, w_down, l, f, j0, final_g=None):
    split_in = len(xs) == 2
    final = final_g is not None
    tm = TM_FFN if (split_in or final) else TM_FFN_WIDE
    n_ctx_tiles = T_CTX // tm
    tile = lambda s: jnp.maximum(s - FFN_LOAD, 0)
    chunk = lambda s: jnp.minimum(s, FFN_LOAD - 1)
    ctx_blk = lambda s: (jnp.minimum(tile(s), n_ctx_tiles - 1), 0)
    lat_blk = lambda s: (jnp.maximum(tile(s) - n_ctx_tiles, 0), 0)
    row = pl.BlockSpec((1, D_MODEL), lambda s: (0, 0))
    tok = lambda index_map: pl.BlockSpec((tm, D_MODEL), index_map)
    if split_in:
        in_specs = [tok(ctx_blk), tok(lat_blk)]
    else:
        in_specs = [tok(lambda s: (tile(s), 0))]
    in_specs += [
        _slab(mods.shape, l), _slab(ng.shape, l, j0 // 3),
        pl.BlockSpec((None, None, FFN_UP_ROWS, 2 * D_FF), lambda s: (l, f, chunk(s), 0)),
        pl.BlockSpec((None, None, FFN_DN_ROWS, D_MODEL), lambda s: (l, f, chunk(s), 0)),
    ]
    args = list(xs) + [mods, ng, w_up, w_down]
    if final:
        in_specs.append(row)
        args.append(final_g)
        out_shape = (jax.ShapeDtypeStruct((T_CTX, D_MODEL), F32), jax.ShapeDtypeStruct((T_LAT, D_MODEL), F32))
        out_specs = (tok(ctx_blk), tok(lat_blk))
    else:
        out_shape = jax.ShapeDtypeStruct((T_ALL, D_MODEL), F32)
        out_specs = tok(lambda s: (tile(s), 0))
    return pl.pallas_call(
        functools.partial(_ffn_kernel, j0=j0, split_in=split_in, final=final, tm=tm),
        out_shape=out_shape,
        grid=(FFN_LOAD + T_ALL // tm,),
        in_specs=in_specs,
        out_specs=out_specs,
        scratch_shapes=[pltpu.VMEM((D_MODEL, 2 * D_FF), BF16), pltpu.VMEM((D_FF, D_MODEL), BF16),
                        pltpu.VMEM((tm, D_FF), BF16)],
        compiler_params=_params(("arbitrary",)),
        name="swiglu_half_step",
    )(*args)


def _rope_partner(x):
    lane = lax.broadcasted_iota(jnp.int32, (1, HEAD_SLOT), 1)
    first = (lane % (2 * ROPE_HALF)) < ROPE_HALF
    return jnp.where(first, -pltpu.roll(x, HEAD_SLOT - ROPE_HALF, 1), pltpu.roll(x, ROPE_HALF, 1))


def _mixin_kernel(h_ref, mod_ref, ng_ref, w_ref, qg_ref, kvg_ref, wq_ref, wqs_ref, wk_ref, wv_ref,
                  fch_ref, cq_ref, sq_ref, ck_ref, sk_ref,
                  hy_ref, q_ref, k_ref, v_ref, ckv_ref, kr_ref, fcs_ref):
    i = pl.program_id(0)
    m = mod_ref[_group_of(i * TM_MIX)]
    n = (_rms(h_ref[...], ng_ref[...]) * (1.0 + m[4:5]) + m[3:4]).astype(BF16)
    proj = _dot_nt(n, w_ref[...])
    hy_ref[...] = proj[:, MIX_HY:MIX_QA]
    q_a = proj[:, MIX_QA:MIX_KVA]
    kv_a = proj[:, MIX_KVA:MIX_FN]
    fn = proj[:, MIX_FN:MIX_KR]
    k_r = proj[:, MIX_KR:MIX_W]

    qn = _rms(q_a, qg_ref[...]).astype(BF16)
    q = _dot(qn, wq_ref[...])
    q_partner = _dot(qn, wqs_ref[...])
    cos_q = cq_ref[...]
    sin_q = sq_ref[...]
    for h in range(N_HEADS):
        hs = slice(h * HEAD_SLOT, (h + 1) * HEAD_SLOT)
        q_ref[:, hs] = (q[:, hs] * cos_q + q_partner[:, hs] * sin_q).astype(BF16)

    ckv = _rms(kv_a, kvg_ref[...])

    @pl.when(i < T_CTX // TM_MIX)
    def _():
        ckv_ref[...] = ckv
        kr_ref[...] = k_r[:, :ROPE_DIM]

    ckv_b = ckv.astype(BF16)
    k_rot = k_r * ck_ref[...] + _rope_partner(k_r) * sk_ref[...]
    k_rope = pltpu.roll(k_rot, NOPE_DIM, 1)
    k_nope = _dot(ckv_b, wk_ref[...])
    for h in range(N_HEADS):
        hs = slice(h * HEAD_SLOT, (h + 1) * HEAD_SLOT)
        k_ref[:, hs] = (k_nope[:, hs] + k_rope).astype(BF16)
    _store_values(v_ref, _dot(ckv_b, wv_ref[...]))
    fcs_ref[...] = _dot(fn.astype(BF16), fch_ref[...]).astype(BF16)


def _mixin_call(h, mods, ng, w_mix, l, qg, kvg, wq, wqs, wk, wv, fch, cosq, sinq, cosk, sink):
    tm = TM_MIX

    def pos_block(i):
        tok0 = i * tm
        return jnp.where(tok0 < T_CTX, DEC_SEQ // tm, ((tok0 - T_CTX) % DEC_SEQ) // tm)

    def full(shape):
        return pl.BlockSpec(shape, lambda i: (0,) * len(shape))

    def tok(width):
        return pl.BlockSpec((tm, width), lambda i: (i, 0))

    def pos(width):
        return pl.BlockSpec((tm, width), lambda i: (pos_block(i), 0))

    out_shape = (
        jax.ShapeDtypeStruct((T_ALL, 3 * HY_W), F32),
        jax.ShapeDtypeStruct((T_ALL, QK_W), BF16),
        jax.ShapeDtypeStruct((T_ALL, QK_W), BF16),
        jax.ShapeDtypeStruct((T_ALL, V_EXT_W), BF16),
        jax.ShapeDtypeStruct((T_CTX, KV_LORA), F32),
        jax.ShapeDtypeStruct((T_CTX, ROPE_DIM), F32),
        jax.ShapeDtypeStruct((T_ALL, 2 * FN_W), BF16),
    )
    return pl.pallas_call(
        _mixin_kernel,
        out_shape=out_shape,
        grid=(T_ALL // tm,),
        in_specs=[
            tok(D_MODEL), _slab(mods.shape, l), _slab(ng.shape, l, 1), _slab(w_mix.shape, l),
            _slab(qg.shape, l), _slab(kvg.shape, l), _slab(wq.shape, l), _slab(wqs.shape, l),
            _slab(wk.shape, l), _slab(wv.shape, l),
            full((FN_W, 2 * FN_W)), pos(HEAD_SLOT), pos(HEAD_SLOT), pos(HEAD_SLOT), pos(HEAD_SLOT),
        ],
        out_specs=tuple(
            tok(s.shape[1]) if s.shape[0] == T_ALL else
            pl.BlockSpec((tm, s.shape[1]), lambda i: (jnp.minimum(i, T_CTX // tm - 1), 0))
            for s in out_shape),
        compiler_params=_params(("arbitrary",)),
        name="mixer_input_proj",
    )(h, mods, ng, w_mix, qg, kvg, wq, wqs, wk, wv, fch, cosq, sinq, cosk, sink)


def _attn_kernel(q_ref, k_ref, v_ref, *rest, has_cache, group, seq_len):
    if has_cache:
        cckv_ref, ckr_ref, wk_ref, wv_ref, ek_ref, o_ref, kc_sc, vc_sc = rest

        @pl.when(pl.program_id(1) == 0)
        def _():
            ckv_b = cckv_ref[...].astype(BF16)
            kc_sc[...] = (_dot(ckv_b, wk_ref[...]) + _dot(ckr_ref[...].astype(BF16), ek_ref[...])).astype(BF16)
            _store_values(vc_sc, _dot(ckv_b, wv_ref[...]))
    else:
        (o_ref,) = rest
    scale2 = math.log2(math.e) / math.sqrt(NOPE_DIM + ROPE_DIM)
    first_half = lax.broadcasted_iota(jnp.int32, (1, PAIR_W), 1) < V_DIM
    for sb in range(group):
        rows = slice(sb * seq_len, (sb + 1) * seq_len) if group > 1 else slice(None)
        for pair in range(N_HEADS // 2):
            vs = slice(2 * pair * PAIR_W, (2 * pair + 2) * PAIR_W)
            outs = []
            for e in range(2):
                hs = slice((2 * pair + e) * HEAD_SLOT, (2 * pair + e + 1) * HEAD_SLOT)
                qh = q_ref[rows, hs]
                s = _dot_nt(qh, k_ref[rows, hs])
                mx = jnp.max(s, axis=-1, keepdims=True)
                if has_cache:
                    sc = _dot_nt(qh, kc_sc[:, hs])
                    mx = jnp.maximum(mx, jnp.max(sc, axis=-1, keepdims=True))
                p = jnp.exp2((s - mx) * scale2)
                if has_cache:
                    o = _dot(p.astype(BF16), v_ref[rows, vs])
                    o = o + _dot(jnp.exp2((sc - mx) * scale2).astype(BF16), vc_sc[:, vs])
                    outs.append(o[:, :PAIR_W] / o[:, PAIR_W:PAIR_W + 1])
                else:
                    o = _dot(p.astype(BF16), v_ref[rows, vs.start:vs.start + PAIR_W])
                    outs.append(o / jnp.sum(p, axis=-1, keepdims=True))
            o_ref[rows, pair * PAIR_W:(pair + 1) * PAIR_W] = jnp.where(first_half, outs[0], outs[1]).astype(BF16)


def _attn_call(q, k, v, cache, *, n_seq, seq_len, tok_off, tq, group=1):
    has_cache = cache is not None
    assert group == 1 or (tq == seq_len and not has_cache)
    nq = seq_len // tq
    kv_rows = group * seq_len
    q_rows = group * tq
    kv_blk0 = tok_off // kv_rows
    q_blk0 = tok_off // q_rows
    vw = N_HEADS * V_DIM
    in_specs = [
        pl.BlockSpec((q_rows, QK_W), lambda b, j: (q_blk0 + b * nq + j, 0)),
        pl.BlockSpec((kv_rows, QK_W), lambda b, j: (kv_blk0 + b, 0)),
        pl.BlockSpec((kv_rows, V_EXT_W), lambda b, j: (kv_blk0 + b, 0)),
    ]
    args = [q, k, v]
    scratch = []
    if has_cache:
        cache_ckv, cache_krope, l, wk, wv, ek = cache
        full = lambda shape: pl.BlockSpec(shape, lambda b, j: (0,) * len(shape))
        in_specs += [pl.BlockSpec((None, None, PAST_LEN, KV_LORA), lambda b, j: (b, l, 0, 0)),
                     pl.BlockSpec((None, None, PAST_LEN, ROPE_DIM), lambda b, j: (b, l, 0, 0)),
                     _slab(wk.shape, l), _slab(wv.shape, l), full((ROPE_DIM, QK_W))]
        args += [cache_ckv, cache_krope, wk, wv, ek]
        scratch = [pltpu.VMEM((PAST_LEN, QK_W), BF16), pltpu.VMEM((PAST_LEN, V_EXT_W), BF16)]
    return pl.pallas_call(
        functools.partial(_attn_kernel, has_cache=has_cache, group=group, seq_len=seq_len),
        out_shape=jax.ShapeDtypeStruct((n_seq * seq_len, vw), BF16),
        grid=(n_seq // group, nq),
        in_specs=in_specs,
        out_specs=pl.BlockSpec((q_rows, vw), lambda b, j: (b * nq + j, 0)),
        scratch_shapes=scratch,
        compiler_params=_params(("arbitrary", "arbitrary")),
        name="mla_attention_cache" if has_cache else "mla_attention",
    )(*args)


def _hyfilt_kernel(feat_ref, w1_ref, b1_ref, w2_ref, b2_ref, w3_ref, dec_ref, alt_ref, c_ref, s_ref,
                   hre_ref, him_ref, nyq_ref, a_sc, b_sc, *, L):
    j = pl.program_id(0)
    half = HY_ORDER * HY_W

    @pl.when(j == 0)
    def _():
        w1 = w1_ref[...]
        w2 = w2_ref[...]
        w3 = w3_ref[...]
        zero = jnp.zeros_like(w2)
        w2_pair = jnp.concatenate([jnp.concatenate([w2, zero], axis=1), jnp.concatenate([zero, w2], axis=1)], axis=0)
        b1_pair = jnp.concatenate([b1_ref[...], b1_ref[...]], axis=1)
        b2_pair = jnp.concatenate([b2_ref[...], b2_ref[...]], axis=1)
        h = jnp.concatenate([_dot_hi(feat_ref[:L // 2, :], w1), _dot_hi(feat_ref[L // 2:, :], w1)], axis=1)
        h = jnp.sin(h + b1_pair)
        h = jnp.sin(_dot_hi(h, w2_pair) + b2_pair)
        h = jnp.concatenate([_dot_x3(h[:, :FILT_HID], w3), _dot_x3(h[:, FILT_HID:], w3)], axis=0)
        dec = dec_ref[...]
        rows = lax.broadcasted_iota(jnp.int32, (L, 1), 0)
        h_fwd = h[:, :half] * dec
        h_bwd = jnp.where(rows > 0, h[:, half:] * dec, 0.0)
        norm = (jnp.sum(jnp.abs(h_fwd), axis=0, keepdims=True)
                + jnp.sum(jnp.abs(h_bwd), axis=0, keepdims=True) + EPS)
        inv = 1.0 / norm
        a = (h_fwd + h_bwd) * inv
        b = (h_fwd - h_bwd) * inv
        a_sc[...] = a.astype(BF16)
        b_sc[...] = b.astype(BF16)
        nyq = jnp.sum(a * alt_ref[...], axis=0, keepdims=True)
        for o in range(HY_ORDER):
            nyq_ref[o] = nyq[:, o * HY_W:(o + 1) * HY_W]

    hre_ref[...] = _dot(c_ref[...], a_sc[...])
    him_ref[...] = -_dot(s_ref[...], b_sc[...])


def _hyfilt_call(L, l, feats, w1, b1, w2, b2, w3, decay, alt, ctab, stab):
    tk = min(L, 512)
    half = HY_ORDER * HY_W

    def full(shape):
        return pl.BlockSpec(shape, lambda j: (0,) * len(shape))

    return pl.pallas_call(
        functools.partial(_hyfilt_kernel, L=L),
        out_shape=(jax.ShapeDtypeStruct((L, half), F32), jax.ShapeDtypeStruct((L, half), F32),
                   jax.ShapeDtypeStruct((HY_ORDER, 1, HY_W), F32)),
        grid=(L // tk,),
        in_specs=[
            full((L, POS_PAD)), _slab(w1.shape, l), _slab(b1.shape, l), _slab(w2.shape, l),
            _slab(b2.shape, l), _slab(w3.shape, l), full((L, half)), full((L, 1)),
            pl.BlockSpec((tk, L), lambda j: (j, 0)), pl.BlockSpec((tk, L), lambda j: (j, 0)),
        ],
        out_specs=(pl.BlockSpec((tk, half), lambda j: (j, 0)), pl.BlockSpec((tk, half), lambda j: (j, 0)),
                   full((HY_ORDER, 1, HY_W))),
        scratch_shapes=[pltpu.VMEM((L, half), BF16), pltpu.VMEM((L, half), BF16)],
        compiler_params=_params(("arbitrary",)),
        name="hyena_filter_spectrum_%d" % L,
    )(feats, w1, b1, w2, b2, w3, decay, alt, ctab, stab)


def _hyconv_kernel(hy_ref, cw_ref, cb_ref, skip_ref, hre_ref, him_ref, nyq_ref, alt_ref, c_ref, s_ref, o_ref,
                   x1_sc, x2_sc, v_sc, vb_sc, yre_sc, yim_sc, nv_sc, *, n_seq, L, tk):
    p = pl.program_id(0)
    ph = pl.program_id(1)
    j = pl.program_id(2)
    cols = [slice(b * HY_W, (b + 1) * HY_W) for b in range(n_seq)]
    blk = pl.ds(pl.multiple_of(j * tk, tk), tk)

    @pl.when((p == 0) & (ph == 0) & (j == 0))
    def _():
        rows = lax.broadcasted_iota(jnp.int32, (L, 1), 0)
        w = cw_ref[...]
        for b in range(n_seq):
            x = hy_ref[b * L:(b + 1) * L, :]
            prev = jnp.where(rows > 0, pltpu.roll(x, 1, 0), 0.0)
            nxt = jnp.where(rows < L - 1, pltpu.roll(x, L - 1, 0), 0.0)
            u = prev * w[0:1] + x * w[1:2] + nxt * w[2:3] + cb_ref[...]
            x1_sc[:, cols[b]] = u[:, :HY_W]
            x2_sc[:, cols[b]] = u[:, HY_W:2 * HY_W]
            v_sc[:, cols[b]] = u[:, 2 * HY_W:]

    @pl.when((ph == 0) & (j == 0))
    def _():
        v = v_sc[...]
        vb_sc[...] = v.astype(BF16)
        nv_sc[...] = jnp.sum(v * alt_ref[...], axis=0, keepdims=True)

    @pl.when(ph == 0)
    def _():
        vb = vb_sc[...]
        v_re = _dot(c_ref[...], vb)
        v_im = -_dot(s_ref[...], vb)
        freq = j * tk + lax.broadcasted_iota(jnp.int32, (tk, 1), 0)
        wk = jnp.where(freq == 0, 0.5, 1.0)
        h_re = hre_ref[...] * wk
        h_im = him_ref[...] * wk
        for b in range(n_seq):
            yre_sc[blk, cols[b]] = (v_re[:, cols[b]] * h_re - v_im[:, cols[b]] * h_im).astype(BF16)
            yim_sc[blk, cols[b]] = (v_re[:, cols[b]] * h_im + v_im[:, cols[b]] * h_re).astype(BF16)

    @pl.when(ph == 1)
    def _():
        acc = _dot(c_ref[...], yre_sc[...]) - _dot(s_ref[...], yim_sc[...])
        alt = alt_ref[blk, :]
        for b in range(n_seq):
            v = v_sc[blk, cols[b]]
            nyq = nv_sc[:, cols[b]] * nyq_ref[...]
            y = acc[:, cols[b]] * (1.0 / L) + (0.5 / L) * alt * nyq + v * skip_ref[...]

            @pl.when(p == 0)
            def _():
                v_sc[blk, cols[b]] = x1_sc[blk, cols[b]] * y

            @pl.when(p == 1)
            def _():
                o_ref[pl.ds(pl.multiple_of(b * L + j * tk, tk), tk), :] = (x2_sc[blk, cols[b]] * y).astype(BF16)


def _hyconv_call(hy, l, conv_w, conv_b, skip, hre, him, nyq, alt, ctab, stab, *, n_seq, L, tok_off):
    tk = min(L, 512)
    nk = L // tk
    seg = tok_off // (n_seq * L)
    width = n_seq * HY_W

    def full(shape):
        return pl.BlockSpec(shape, lambda p, ph, j: (0,) * len(shape))

    def filt(p, ph, j):
        return (jnp.where(ph == 0, j, nk - 1), p)

    vm = lambda dtype: pltpu.VMEM((L, width), dtype)
    return pl.pallas_call(
        functools.partial(_hyconv_kernel, n_seq=n_seq, L=L, tk=tk),
        out_shape=jax.ShapeDtypeStruct((n_seq * L, HY_W), BF16),
        grid=(HY_ORDER, 2, nk),
        in_specs=[
            pl.BlockSpec((n_seq * L, 3 * HY_W), lambda p, ph, j: (seg, 0), pipeline_mode=pl.Buffered(1)),
            _slab(conv_w.shape, l), _slab(conv_b.shape, l),
            pl.BlockSpec((None, None, 1, HY_W), lambda p, ph, j: (l, p, 0, 0)),
            pl.BlockSpec((tk, HY_W), filt), pl.BlockSpec((tk, HY_W), filt),
            pl.BlockSpec((None, 1, HY_W), lambda p, ph, j: (p, 0, 0)),
            full((L, 1)),
            pl.BlockSpec((tk, L), lambda p, ph, j: (j, 0)), pl.BlockSpec((tk, L), lambda p, ph, j: (j, 0)),
        ],
        out_specs=pl.BlockSpec((n_seq * L, HY_W), lambda p, ph, j: (0, 0)),
        scratch_shapes=[vm(F32), vm(F32), vm(F32), vm(BF16), vm(BF16), vm(BF16), pltpu.VMEM((1, width), F32)],
        compiler_params=_params(("arbitrary", "arbitrary", "arbitrary")),
        name="hyena_long_conv_%d" % L,
    )(hy, conv_w, conv_b, skip, hre, him, nyq, alt, ctab, stab)


def _fnet_kernel(fcs_ref, c_ref, s_ref, o_ref, *, n_seq, L, tk, scale):
    j = pl.program_id(0)
    c = c_ref[...]
    s = s_ref[...]
    for b in range(n_seq):
        seq = slice(b * L, (b + 1) * L)
        y = (_dot(c, fcs_ref[seq, :FN_W]) - _dot(s, fcs_ref[seq, FN_W:])) * scale
        o_ref[pl.ds(pl.multiple_of(b * L + j * tk, tk), tk), :] = y.astype(BF16)


def _fnet_call(fcs, ctab, stab, *, n_seq, L, tok_off):
    tk = min(L, 512)
    seg = tok_off // (n_seq * L)
    return pl.pallas_call(
        functools.partial(_fnet_kernel, n_seq=n_seq, L=L, tk=tk, scale=1.0 / math.sqrt(L * FN_GROUP_W)),
        out_shape=jax.ShapeDtypeStruct((n_seq * L, FN_W), BF16),
        grid=(L // tk,),
        in_specs=[
            pl.BlockSpec((n_seq * L, 2 * FN_W), lambda j: (seg, 0)),
            pl.BlockSpec((tk, L), lambda j: (j, 0)), pl.BlockSpec((tk, L), lambda j: (j, 0)),
        ],
        out_specs=pl.BlockSpec((n_seq * L, FN_W), lambda j: (0, 0)),
        compiler_params=_params(("arbitrary",)),
        name="fnet_position_dft_%d" % L,
    )(fcs, ctab, stab)


def _mixout_kernel(h_ref, mod_ref, ng_ref, wg_ref, zc_ref, zl_ref, ac_ref, al_ref, fc_ref, fl_ref,
                   wa_ref, wb_ref, wc_ref, wo_ref, o_ref):
    i = pl.program_id(0)
    is_ctx = i < T_CTX // TM_OUT
    m = mod_ref[_group_of(i * TM_OUT)]
    for sub in range(TM_OUT // OUT_SUB_ROWS):
        rows = slice(sub * OUT_SUB_ROWS, (sub + 1) * OUT_SUB_ROWS)
        h = h_ref[rows, :]
        n = (_rms(h, ng_ref[...]) * (1.0 + m[4:5]) + m[3:4]).astype(BF16)
        z = jnp.where(is_ctx, zc_ref[rows, :], zl_ref[rows, :])
        a = jnp.where(is_ctx, ac_ref[rows, :], al_ref[rows, :])
        f = jnp.where(is_ctx, fc_ref[rows, :], fl_ref[rows, :])
        g0, g1, g2, g3 = (GATE_OFF + b * D_MODEL for b in range(N_BRANCH + 1))
        acc = _sigmoid(_dot_nt(n, wg_ref[g0:g1, :])) * _dot(z, wa_ref[...])
        acc = acc + _sigmoid(_dot_nt(n, wg_ref[g1:g2, :])) * _dot(a, wb_ref[...])
        acc = acc + _sigmoid(_dot_nt(n, wg_ref[g2:g3, :])) * _dot(f, wc_ref[...])
        y = _dot(acc.astype(BF16), wo_ref[...])
        o_ref[rows, :] = h + m[5:6] * y


def _mixout_call(h, mods, ng, w_gate, l, z, a, f, wa, wb, wc, wo):
    tm = TM_OUT
    n_ctx = T_CTX // tm
    weight = functools.partial(_slab, single_buffer=True)

    def full(shape):
        return pl.BlockSpec(shape, lambda i: (0,) * len(shape))

    def tok(width):
        return pl.BlockSpec((tm, width), lambda i: (i, 0))

    def pair(width):
        return [pl.BlockSpec((tm, width), lambda i: (jnp.minimum(i, n_ctx - 1), 0)),
                pl.BlockSpec((tm, width), lambda i: (jnp.maximum(i - n_ctx, 0), 0))]

    return pl.pallas_call(
        _mixout_kernel,
        out_shape=jax.ShapeDtypeStruct((T_ALL, D_MODEL), F32),
        grid=(T_ALL // tm,),
        in_specs=[
            tok(D_MODEL), _slab(mods.shape, l), _slab(ng.shape, l, 1), weight(w_gate.shape, l),
            *pair(HY_W), *pair(N_HEADS * V_DIM), *pair(FN_W),
            weight(wa.shape, l), weight(wb.shape, l), weight(wc.shape, l), weight(wo.shape, l),
        ],
        out_specs=tok(D_MODEL),
        compiler_params=_params(("parallel",)),
        name="gated_merge_out_proj",
    )(h, mods, ng, w_gate, *z, *a, *f, wa, wb, wc, wo)


def _wprep_kernel(w_ref, mix_ref, gate_ref):
    r = pl.program_id(1)
    last = GATE_EXT_ROWS - 2 * PREP_ROWS

    @pl.when(r == 0)
    def _():
        mix_ref[:MIX_FN, :] = w_ref[...].astype(BF16)

    @pl.when(r == 1)
    def _():
        mix_ref[MIX_FN:MIX_KR, :] = w_ref[ROPE_DIM:GATE_OFF, :].astype(BF16)
        mix_ref[MIX_KR:MIX_KR + ROPE_DIM, :] = w_ref[0:ROPE_DIM, :].astype(BF16)
        mix_ref[MIX_KR + ROPE_DIM:, :] = jnp.zeros((MIX_W - MIX_KR - ROPE_DIM, D_MODEL), BF16)

    @pl.when((r == 1) | (r == 2))
    def _():
        gate_ref[...] = w_ref[...].astype(BF16)

    @pl.when(r == 3)
    def _():
        gate_ref[:last, :] = w_ref[:last, :].astype(BF16)


def _wprep_call(w_in_t):
    assert IN_KR == PREP_ROWS and 4 * PREP_ROWS >= IN_COLS
    return pl.pallas_call(
        _wprep_kernel,
        out_shape=(jax.ShapeDtypeStruct((DEPTH, MIX_W, D_MODEL), BF16),
                   jax.ShapeDtypeStruct((DEPTH, GATE_EXT_ROWS, D_MODEL), BF16)),
        grid=(DEPTH, 4),
        in_specs=[pl.BlockSpec((None, PREP_ROWS, D_MODEL), lambda l, r: (l, r, 0))],
        out_specs=(pl.BlockSpec((None, MIX_W, D_MODEL), lambda l, r: (l, 0, 0)),
                   pl.BlockSpec((None, PREP_ROWS, D_MODEL), lambda l, r: (l, jnp.maximum(r - 1, 0), 0))),
        compiler_params=_params(("arbitrary", "arbitrary")),
        name="input_proj_weight_relayout",
    )(w_in_t)


def _head_slot_weights(w_qb, w_kvb):
    wq3 = w_qb.reshape(DEPTH, Q_LORA, N_HEADS, NOPE_DIM + ROPE_DIM)
    slot_pad = HEAD_SLOT - NOPE_DIM - ROPE_DIM
    no_pad = ((0, 0),) * 3
    wq = jnp.pad(wq3, no_pad + ((0, slot_pad),)).reshape(DEPTH, Q_LORA, QK_W).astype(BF16)
    rope = wq3[..., NOPE_DIM:].reshape(DEPTH, Q_LORA, N_HEADS, 2, 2, ROPE_HALF)
    partner = jnp.stack([-rope[..., 1, :], rope[..., 0, :]], axis=4).reshape(DEPTH, Q_LORA, N_HEADS, ROPE_DIM)
    wqs = jnp.pad(partner, no_pad + ((NOPE_DIM, slot_pad),)).reshape(DEPTH, Q_LORA, QK_W).astype(BF16)

    wkv3 = w_kvb.reshape(DEPTH, KV_LORA, N_HEADS, NOPE_DIM + V_DIM)
    wk = jnp.pad(wkv3[..., :NOPE_DIM], no_pad + ((0, HEAD_SLOT - NOPE_DIM),))
    wk = wk.reshape(DEPTH, KV_LORA, QK_W).astype(BF16)
    wv = wkv3[..., NOPE_DIM:].reshape(DEPTH, KV_LORA, N_HEADS * V_DIM).astype(BF16)
    return wq, wqs, wk, wv


def kernel(x_prompt, x_sample, cache_ckv, cache_krope, c, c_ctx, w_ada, b_ada, norm_g, w_ffn_up, w_ffn_down,
           w_in, hy_conv_w, hy_conv_b, hy_filt_w1, hy_filt_b1, hy_filt_w2, hy_filt_b2, hy_filt_w3, hy_skip,
           w_hy_out, q_norm_g, w_qb, kv_norm_g, w_kvb, w_mla_o, w_fnet, w_out, final_g):
    tabs = _tables()
    cosq, sinq, cosk, sink = (jnp.asarray(t) for t in tabs["rope"])
    fch = jnp.asarray(tabs["fnch"]).astype(BF16)
    ek = jnp.asarray(tabs["ropeexp"]).astype(BF16)
    dft = {}
    for L in (SEQ, DEC_SEQ):
        dft[("hy", L)] = tuple(jnp.asarray(t).astype(BF16) for t in tabs[("hy", L)])
        dft[("fn", L)] = tuple(jnp.asarray(t).astype(BF16) for t in tabs[("fn", L)])

    w_mix, w_gate = _wprep_call(jnp.swapaxes(w_in, 1, 2))

    cvec = jnp.concatenate([c_ctx[None, :], c, jnp.zeros((SUBLANES - N_GROUPS, D_MODEL), F32)], axis=0).T
    ada = _ada_call(cvec, w_ada, b_ada)
    mods = ada[:, :N_GROUPS].reshape(DEPTH, N_GROUPS, 9, D_MODEL)
    ng = norm_g.reshape(DEPTH, 3, 1, D_MODEL)
    qg = q_norm_g.reshape(DEPTH, 1, Q_LORA)
    kvg = kv_norm_g.reshape(DEPTH, 1, KV_LORA)
    wq, wqs, wk, wv = _head_slot_weights(w_qb, w_kvb)
    w1 = jnp.pad(hy_filt_w1, ((0, 0), (0, POS_PAD - POS_EMB), (0, 0)))
    b1 = hy_filt_b1.reshape(DEPTH, 1, FILT_HID)
    b2 = hy_filt_b2.reshape(DEPTH, 1, FILT_HID)
    conv_b = hy_conv_b.reshape(DEPTH, 1, 3 * HY_W)
    skip = hy_skip.reshape(DEPTH, HY_ORDER, 1, HY_W)
    w_hy_out, w_mla_o, w_fnet, w_out = (w.astype(BF16) for w in (w_hy_out, w_mla_o, w_fnet, w_out))
    hs = (x_prompt.reshape(T_CTX, D_MODEL), x_sample.reshape(T_LAT, D_MODEL))
    segs = ((BATCH, SEQ, 0), (DEC_BATCH, DEC_SEQ, T_CTX))
    ckv_out = []
    kr_out = []
    for l in range(DEPTH):
        h = _ffn_call(hs, mods, ng, w_ffn_up, w_ffn_down, l, 0, 0)

        hy, q, k, v, ckv, k_r, fcs = _mixin_call(
            h, mods, ng, w_mix, l, qg, kvg, wq, wqs, wk, wv, fch, cosq, sinq, cosk, sink)
        ckv_out.append(ckv.reshape(BATCH, SEQ, KV_LORA))
        kr_out.append(k_r.reshape(BATCH, SEQ, ROPE_DIM))

        z_parts, a_parts, f_parts = [], [], []
        for n_seq, L, off in segs:
            feats, decay, alt = (jnp.asarray(t) for t in tabs[("filt", L)])
            c_hy, s_hy = dft[("hy", L)]
            hre, him, nyq = _hyfilt_call(L, l, feats, w1, b1, hy_filt_w2, b2, hy_filt_w3, decay, alt, c_hy, s_hy)
            z_parts.append(_hyconv_call(hy, l, hy_conv_w, conv_b, skip, hre, him, nyq, alt, c_hy, s_hy,
                                        n_seq=n_seq, L=L, tok_off=off))
            if off:
                a_parts.append(_attn_call(q, k, v, (cache_ckv, cache_krope, l, wk, wv, ek),
                                          n_seq=n_seq, seq_len=L, tok_off=off, tq=ATTN_TQ))
            else:
                a_parts.append(_attn_call(q, k, v, None, n_seq=n_seq, seq_len=L, tok_off=off, tq=L,
                                          group=ATTN_GROUP))
            c_fn, s_fn = dft[("fn", L)]
            f_parts.append(_fnet_call(fcs, c_fn, s_fn, n_seq=n_seq, L=L, tok_off=off))
        h = _mixout_call(h, mods, ng, w_gate, l, z_parts, a_parts, f_parts, w_hy_out, w_mla_o, w_fnet, w_out)

        last = l == DEPTH - 1
        out = _ffn_call((h,), mods, ng, w_ffn_up, w_ffn_down, l, 1, 6, final_g[None, :] if last else None)
        hs = out if last else (out,)

    y_prompt = hs[0].reshape(BATCH, SEQ, D_MODEL)
    y_sample = hs[1].reshape(DEC_BATCH, DEC_SEQ, D_MODEL)
    return y_prompt, y_sample, jnp.stack(ckv_out, axis=1), jnp.stack(kr_out, axis=1)
```

```python
import functools
import math

import numpy as np
import jax
import jax.numpy as jnp
from jax import lax
from jax.experimental import pallas as pl
from jax.experimental.pallas import tpu as pltpu

F32 = jnp.float32
BF16 = jnp.bfloat16
HIGHEST = lax.Precision.HIGHEST

D_MODEL = 1024
BATCH = 16
SEQ = 256
DEPTH = 2
DEC_BATCH = 2
DEC_SEQ = 2048
PAST_LEN = 512
GRID_W = 64
HY_W = 256
HY_ORDER = 2
N_BANDS = 8
POS_EMB = 1 + 2 * N_BANDS
FILT_HID = 64
N_HEADS = 8
Q_LORA = 256
KV_LORA = 128
NOPE_DIM = 64
ROPE_DIM = 32
V_DIM = 64
ROPE_BASE = 10000.0
FN_GROUPS = 4
FN_GROUP_W = 64
FN_W = FN_GROUPS * FN_GROUP_W
N_BRANCH = 3
D_FF = 2816
EPS = 1e-6

T_CTX = BATCH * SEQ
T_LAT = DEC_BATCH * DEC_SEQ
T_ALL = T_CTX + T_LAT
N_GROUPS = 1 + DEC_BATCH
LANES = 128
SUBLANES = 8
VMEM_BYTES = 64 * 1024 * 1024
VMEM_LIMIT = VMEM_BYTES - 8 * 1024 * 1024

HEAD_SLOT = LANES
QK_W = N_HEADS * HEAD_SLOT
PAIR_W = 2 * V_DIM
V_EXT_W = N_HEADS * PAIR_W
POS_PAD = LANES
assert PAIR_W == LANES and NOPE_DIM + ROPE_DIM <= HEAD_SLOT

TM_FFN = 512
TF_FFN = 256
FFN_LOAD = 8
FFN_UP_ROWS = D_MODEL // FFN_LOAD
FFN_DN_ROWS = D_FF // FFN_LOAD
TM_FFN_WIDE = 1024
FFN_SUB_ROWS = 256
TM_MIX = 512
TM_OUT = 1024
OUT_SUB_ROWS = 256
ADA_TN = 2304
ATTN_TQ = 512
ATTN_GROUP = 4
GATE_LOAD = 3

IN_KR = 3 * HY_W + Q_LORA + KV_LORA
IN_FN = IN_KR + ROPE_DIM
IN_GATE = IN_FN + FN_W
IN_COLS = IN_GATE + N_BRANCH * D_MODEL
GATE_BLK = IN_KR
GATE_OFF = IN_GATE - GATE_BLK
assert (GATE_LOAD + 1) * GATE_BLK >= IN_COLS

MIX_HY = 0
MIX_QA = 3 * HY_W
MIX_KVA = MIX_QA + Q_LORA
MIX_FN = MIX_KVA + KV_LORA
MIX_KR = MIX_FN + FN_W
MIX_W = MIX_KR + LANES
ROPE_HALF = ROPE_DIM // 4


def _dft_tables(L, half):
    k = np.arange(L, dtype=np.int64)
    period = 2 * L if half else L
    m = (k[:, None] * k[None, :]) % period
    ang = 2.0 * np.pi * m.astype(np.float64) / period
    return np.cos(ang).astype(np.float32), np.sin(ang).astype(np.float32)


def _filter_tables(L):
    t = np.arange(L, dtype=np.float64)
    t_norm = t / (L - 1)
    w = 2.0 * np.pi * t / L
    bands = np.linspace(1e-4, N_BANDS - 1, N_BANDS)
    ang = w[:, None] * bands[None, :]
    feats = np.concatenate([t_norm[:, None], np.cos(ang), -np.sin(ang)], axis=-1)
    feats = np.pad(feats, ((0, 0), (0, POS_PAD - POS_EMB)))
    deltas = np.linspace(math.log(1e-2) / 1.5, math.log(1e-2) / 0.3, HY_W)
    decay = np.exp(-t_norm[:, None] * np.abs(deltas)[None, :])
    decay = np.concatenate([decay, decay], axis=1)
    alt = np.where(np.arange(L) % 2 == 0, 1.0, -1.0)[:, None]
    return feats.astype(np.float32), decay.astype(np.float32), alt.astype(np.float32)


def _rope_tables():
    t = np.arange(DEC_SEQ)
    row = (t // GRID_W).astype(np.float64)
    col = (t % GRID_W).astype(np.float64)
    nf = ROPE_DIM // 4
    inv = ROPE_BASE ** (-np.arange(nf, dtype=np.float64) / nf)
    ar = row[:, None] * inv[None, :]
    ac = col[:, None] * inv[None, :]
    cos32 = np.concatenate([np.cos(ar), np.cos(ar), np.cos(ac), np.cos(ac)], axis=1)
    sin32 = np.concatenate([np.sin(ar), np.sin(ar), np.sin(ac), np.sin(ac)], axis=1)
    cos32 = np.concatenate([cos32, np.ones((TM_MIX, ROPE_DIM))], axis=0)
    sin32 = np.concatenate([sin32, np.zeros((TM_MIX, ROPE_DIM))], axis=0)
    n = cos32.shape[0]
    cosq = np.ones((n, HEAD_SLOT))
    sinq = np.zeros((n, HEAD_SLOT))
    cosq[:, NOPE_DIM:NOPE_DIM + ROPE_DIM] = cos32
    sinq[:, NOPE_DIM:NOPE_DIM + ROPE_DIM] = sin32
    cosk = np.zeros((n, HEAD_SLOT))
    sink = np.zeros((n, HEAD_SLOT))
    cosk[:, :ROPE_DIM] = cos32
    sink[:, :ROPE_DIM] = sin32
    return tuple(t.astype(np.float32) for t in (cosq, sinq, cosk, sink))


def _fnet_channel_table():
    j = np.arange(FN_GROUP_W)
    ang = 2.0 * np.pi * ((j[:, None] * j[None, :]) % FN_GROUP_W) / FN_GROUP_W
    out = np.zeros((FN_W, 2 * FN_W))
    for g in range(FN_GROUPS):
        sl = slice(g * FN_GROUP_W, (g + 1) * FN_GROUP_W)
        out[sl, sl] = np.cos(ang)
        out[sl, FN_W + g * FN_GROUP_W:FN_W + (g + 1) * FN_GROUP_W] = np.sin(ang)
    return out.astype(np.float32)


def _rope_expand_table():
    e = np.zeros((ROPE_DIM, N_HEADS, HEAD_SLOT), np.float32)
    for j in range(ROPE_DIM):
        e[j, :, NOPE_DIM + j] = 1.0
    return e.reshape(ROPE_DIM, QK_W)


_TABLES = {}


def _tables():
    if not _TABLES:
        for L in (SEQ, DEC_SEQ):
            _TABLES[("hy", L)] = _dft_tables(L, True)
            _TABLES[("fn", L)] = _dft_tables(L, False)
            _TABLES[("filt", L)] = _filter_tables(L)
        _TABLES["rope"] = _rope_tables()
        _TABLES["fnch"] = _fnet_channel_table()
        _TABLES["ropeexp"] = _rope_expand_table()
    return _TABLES


def _rms(x, g):
    ms = jnp.mean(x * x, axis=-1, keepdims=True)
    return x * lax.rsqrt(ms + EPS) * g


def _sigmoid(x):
    return 1.0 / (1.0 + jnp.exp(-x))


def _dot(a, b):
    return jnp.dot(a, b, preferred_element_type=F32)


def _dot_hi(a, b):
    return jnp.dot(a, b, precision=HIGHEST, preferred_element_type=F32)


def _dot_x3(a, b):
    a_hi = a.astype(BF16)
    b_hi = b.astype(BF16)
    a_lo = (a - a_hi.astype(F32)).astype(BF16)
    b_lo = (b - b_hi.astype(F32)).astype(BF16)
    return _dot(a_hi, b_hi) + (_dot(a_hi, b_lo) + _dot(a_lo, b_hi))


def _dot_nt(a, b):
    return lax.dot_general(a, b, (((1,), (1,)), ((), ())), preferred_element_type=F32)


def _store_values(v_ref, v):
    ones = jnp.ones((v.shape[0], PAIR_W), BF16)
    for pair in range(N_HEADS // 2):
        v_ref[:, 2 * pair * PAIR_W:(2 * pair + 1) * PAIR_W] = v[:, pair * PAIR_W:(pair + 1) * PAIR_W].astype(BF16)
        v_ref[:, (2 * pair + 1) * PAIR_W:(2 * pair + 2) * PAIR_W] = ones


def _group_of(tok0):
    return jnp.where(tok0 < T_CTX, 0, 1 + (tok0 - T_CTX) // DEC_SEQ)


def _slab(shape, *lead, single_buffer=False):
    tail = tuple(shape[len(lead):])
    mode = dict(pipeline_mode=pl.Buffered(1)) if single_buffer else {}
    return pl.BlockSpec((None,) * len(lead) + tail, lambda *_: tuple(lead) + (0,) * len(tail), **mode)


def _params(sem):
    return pltpu.CompilerParams(dimension_semantics=sem, vmem_limit_bytes=VMEM_LIMIT)


def _ada_kernel(c_ref, w_ref, b_ref, o_ref):
    x = c_ref[...]
    s = x * _sigmoid(x)
    w = w_ref[...]
    o_ref[...] = jnp.zeros_like(o_ref)
    for g in range(N_GROUPS):
        o_ref[g:g + 1, :] = jnp.sum(w * s[:, g:g + 1], axis=0, keepdims=True) + b_ref[...]


def _ada_call(cvec, w_ada, b_ada):
    tn = ADA_TN
    n_out = 9 * D_MODEL
    return pl.pallas_call(
        _ada_kernel,
        out_shape=jax.ShapeDtypeStruct((DEPTH, SUBLANES, n_out), F32),
        grid=(DEPTH, n_out // tn),
        in_specs=[
            pl.BlockSpec((D_MODEL, SUBLANES), lambda l, j: (0, 0)),
            pl.BlockSpec((None, D_MODEL, tn), lambda l, j: (l, 0, j)),
            pl.BlockSpec((None, 1, tn), lambda l, j: (l, 0, j)),
        ],
        out_specs=pl.BlockSpec((None, SUBLANES, tn), lambda l, j: (l, 0, j)),
        compiler_params=_params(("arbitrary", "arbitrary")),
        name="ada_modulation",
    )(cvec, w_ada, b_ada.reshape(DEPTH, 1, n_out))


def _ffn_kernel(*refs, j0, split_in, final, tm):
    refs = list(refs)
    x_refs = [refs.pop(0) for _ in range(2 if split_in else 1)]
    mod_ref, ng_ref, wup_ref, wd_ref = refs[:4]
    refs = refs[4:]
    fg_ref = refs.pop(0) if final else None
    o_refs = [refs.pop(0) for _ in range(2 if final else 1)]
    wup_sc, wd_sc, hid_sc = refs
    s = pl.program_id(0)
    n_ctx_tiles = T_CTX // tm

    @pl.when(s < FFN_LOAD)
    def _():
        wup_sc[pl.ds(pl.multiple_of(s * FFN_UP_ROWS, FFN_UP_ROWS), FFN_UP_ROWS), :] = wup_ref[...].astype(BF16)
        wd_sc[pl.ds(pl.multiple_of(s * FFN_DN_ROWS, FFN_DN_ROWS), FFN_DN_ROWS), :] = wd_ref[...].astype(BF16)

    @pl.when(s >= FFN_LOAD)
    def _():
        t = s - FFN_LOAD
        m = mod_ref[_group_of(t * tm)]
        ys = []
        for sub in range(tm // FFN_SUB_ROWS):
            rows = slice(sub * FFN_SUB_ROWS, (sub + 1) * FFN_SUB_ROWS)
            if split_in:
                x = jnp.where(t < n_ctx_tiles, x_refs[0][rows, :], x_refs[1][rows, :])
            else:
                x = x_refs[0][rows, :]
            n = (_rms(x, ng_ref[...]) * (1.0 + m[j0 + 1:j0 + 2]) + m[j0:j0 + 1]).astype(BF16)
            for c in range(D_FF // TF_FFN):
                cols = slice(c * TF_FFN, (c + 1) * TF_FFN)
                g = _dot(n, wup_sc[:, cols])
                u = _dot(n, wup_sc[:, D_FF + c * TF_FFN:D_FF + (c + 1) * TF_FFN])
                hid_sc[rows, cols] = (g * _sigmoid(g) * u).astype(BF16)
            y = x + 0.5 * m[j0 + 2:j0 + 3] * _dot(hid_sc[rows, :], wd_sc[...])
            if final:
                y = _rms(y, fg_ref[...])
            ys.append((rows, y))
        if final:
            @pl.when(t < n_ctx_tiles)
            def _():
                for rows, y in ys:
                    o_refs[0][rows, :] = y

            @pl.when(t >= n_ctx_tiles)
            def _():
                for rows, y in ys:
                    o_refs[1][rows, :] = y
        else:
            for rows, y in ys:
                o_refs[0][rows, :] = y


def _ffn_call(xs, mods, ng, w_up, w_down, l, f, j0, final_g=None):
    split_in = len(xs) == 2
    final = final_g is not None
    tm = TM_FFN if (split_in or final) else TM_FFN_WIDE
    n_ctx_tiles = T_CTX // tm
    tile = lambda s: jnp.maximum(s - FFN_LOAD, 0)
    chunk = lambda s: jnp.minimum(s, FFN_LOAD - 1)
    ctx_blk = lambda s: (jnp.minimum(tile(s), n_ctx_tiles - 1), 0)
    lat_blk = lambda s: (jnp.maximum(tile(s) - n_ctx_tiles, 0), 0)
    row = pl.BlockSpec((1, D_MODEL), lambda s: (0, 0))
    tok = lambda index_map: pl.BlockSpec((tm, D_MODEL), index_map)
    if split_in:
        in_specs = [tok(ctx_blk), tok(lat_blk)]
    else:
        in_specs = [tok(lambda s: (tile(s), 0))]
    in_specs += [
        _slab(mods.shape, l), _slab(ng.shape, l, j0 // 3),
        pl.BlockSpec((None, None, FFN_UP_ROWS, 2 * D_FF), lambda s: (l, f, chunk(s), 0)),
        pl.BlockSpec((None, None, FFN_DN_ROWS, D_MODEL), lambda s: (l, f, chunk(s), 0)),
    ]
    args = list(xs) + [mods, ng, w_up, w_down]
    if final:
        in_specs.append(row)
        args.append(final_g)
        out_shape = (jax.ShapeDtypeStruct((T_CTX, D_MODEL), F32), jax.ShapeDtypeStruct((T_LAT, D_MODEL), F32))
        out_specs = (tok(ctx_blk), tok(lat_blk))
    else:
        out_shape = jax.ShapeDtypeStruct((T_ALL, D_MODEL), F32)
        out_specs = tok(lambda s: (tile(s), 0))
    return pl.pallas_call(
        functools.partial(_ffn_kernel, j0=j0, split_in=split_in, final=final, tm=tm),
        out_shape=out_shape,
        grid=(FFN_LOAD + T_ALL // tm,),
        in_specs=in_specs,
        out_specs=out_specs,
        scratch_shapes=[pltpu.VMEM((D_MODEL, 2 * D_FF), BF16), pltpu.VMEM((D_FF, D_MODEL), BF16),
                        pltpu.VMEM((tm, D_FF), BF16)],
        compiler_params=_params(("arbitrary",)),
        name="swiglu_half_step",
    )(*args)


def _rope_partner(x):
    lane = lax.broadcasted_iota(jnp.int32, (1, HEAD_SLOT), 1)
    first = (lane % (2 * ROPE_HALF)) < ROPE_HALF
    return jnp.where(first, -pltpu.roll(x, HEAD_SLOT - ROPE_HALF, 1), pltpu.roll(x, ROPE_HALF, 1))


def _mixin_kernel(h_ref, mod_ref, ng_ref, w_ref, qg_ref, kvg_ref, wq_ref, wqs_ref, wk_ref, wv_ref,
                  fch_ref, cq_ref, sq_ref, ck_ref, sk_ref,
                  hy_ref, q_ref, k_ref, v_ref, ckv_ref, kr_ref, fcs_ref, wmix_sc):
    i = pl.program_id(0)

    @pl.when(i == 0)
    def _():
        wmix_sc[:MIX_FN, :] = w_ref[:IN_KR, :].astype(BF16)
        wmix_sc[MIX_FN:MIX_KR, :] = w_ref[IN_FN:IN_GATE, :].astype(BF16)
        wmix_sc[MIX_KR:MIX_KR + ROPE_DIM, :] = w_ref[IN_KR:IN_FN, :].astype(BF16)
        wmix_sc[MIX_KR + ROPE_DIM:, :] = jnp.zeros((MIX_W - MIX_KR - ROPE_DIM, D_MODEL), BF16)

    m = mod_ref[_group_of(i * TM_MIX)]
    n = (_rms(h_ref[...], ng_ref[...]) * (1.0 + m[4:5]) + m[3:4]).astype(BF16)
    proj = _dot_nt(n, wmix_sc[...])
    hy_ref[...] = proj[:, MIX_HY:MIX_QA]
    q_a = proj[:, MIX_QA:MIX_KVA]
    kv_a = proj[:, MIX_KVA:MIX_FN]
    fn = proj[:, MIX_FN:MIX_KR]
    k_r = proj[:, MIX_KR:MIX_W]

    qn = _rms(q_a, qg_ref[...]).astype(BF16)
    q = _dot(qn, wq_ref[...])
    q_partner = _dot(qn, wqs_ref[...])
    cos_q = cq_ref[...]
    sin_q = sq_ref[...]
    for h in range(N_HEADS):
        hs = slice(h * HEAD_SLOT, (h + 1) * HEAD_SLOT)
        q_ref[:, hs] = (q[:, hs] * cos_q + q_partner[:, hs] * sin_q).astype(BF16)

    ckv = _rms(kv_a, kvg_ref[...])

    @pl.when(i < T_CTX // TM_MIX)
    def _():
        ckv_ref[...] = ckv
        kr_ref[...] = k_r[:, :ROPE_DIM]

    ckv_b = ckv.astype(BF16)
    k_rot = k_r * ck_ref[...] + _rope_partner(k_r) * sk_ref[...]
    k_rope = pltpu.roll(k_rot, NOPE_DIM, 1)
    k_nope = _dot(ckv_b, wk_ref[...])
    for h in range(N_HEADS):
        hs = slice(h * HEAD_SLOT, (h + 1) * HEAD_SLOT)
        k_ref[:, hs] = (k_nope[:, hs] + k_rope).astype(BF16)
    _store_values(v_ref, _dot(ckv_b, wv_ref[...]))
    fcs_ref[...] = _dot(fn.astype(BF16), fch_ref[...]).astype(BF16)


def _mixin_call(h, mods, ng, w_in_t, l, qg, kvg, wq, wqs, wk, wv, fch, cosq, sinq, cosk, sink):
    tm = TM_MIX

    def pos_block(i):
        tok0 = i * tm
        return jnp.where(tok0 < T_CTX, DEC_SEQ // tm, ((tok0 - T_CTX) % DEC_SEQ) // tm)

    def full(shape):
        return pl.BlockSpec(shape, lambda i: (0,) * len(shape))

    def tok(width):
        return pl.BlockSpec((tm, width), lambda i: (i, 0))

    def pos(width):
        return pl.BlockSpec((tm, width), lambda i: (pos_block(i), 0))

    out_shape = (
        jax.ShapeDtypeStruct((T_ALL, 3 * HY_W), F32),
        jax.ShapeDtypeStruct((T_ALL, QK_W), BF16),
        jax.ShapeDtypeStruct((T_ALL, QK_W), BF16),
        jax.ShapeDtypeStruct((T_ALL, V_EXT_W), BF16),
        jax.ShapeDtypeStruct((T_CTX, KV_LORA), F32),
        jax.ShapeDtypeStruct((T_CTX, ROPE_DIM), F32),
        jax.ShapeDtypeStruct((T_ALL, 2 * FN_W), BF16),
    )
    return pl.pallas_call(
        _mixin_kernel,
        out_shape=out_shape,
        grid=(T_ALL // tm,),
        in_specs=[
            tok(D_MODEL), _slab(mods.shape, l), _slab(ng.shape, l, 1),
            pl.BlockSpec((None, IN_GATE, D_MODEL), lambda i: (l, 0, 0), pipeline_mode=pl.Buffered(1)),
            _slab(qg.shape, l), _slab(kvg.shape, l), _slab(wq.shape, l), _slab(wqs.shape, l),
            _slab(wk.shape, l), _slab(wv.shape, l),
            full((FN_W, 2 * FN_W)), pos(HEAD_SLOT), pos(HEAD_SLOT), pos(HEAD_SLOT), pos(HEAD_SLOT),
        ],
        out_specs=tuple(
            tok(s.shape[1]) if s.shape[0] == T_ALL else
            pl.BlockSpec((tm, s.shape[1]), lambda i: (jnp.minimum(i, T_CTX // tm - 1), 0))
            for s in out_shape),
        scratch_shapes=[pltpu.VMEM((MIX_W, D_MODEL), BF16)],
        compiler_params=_params(("arbitrary",)),
        name="mixer_input_proj",
    )(h, mods, ng, w_in_t, qg, kvg, wq, wqs, wk, wv, fch, cosq, sinq, cosk, sink)


def _attn_kernel(q_ref, k_ref, v_ref, *rest, has_cache, group, seq_len):
    if has_cache:
        cckv_ref, ckr_ref, wk_ref, wv_ref, ek_ref, o_ref, kc_sc, vc_sc = rest

        @pl.when(pl.program_id(1) == 0)
        def _():
            ckv_b = cckv_ref[...].astype(BF16)
            kc_sc[...] = (_dot(ckv_b, wk_ref[...]) + _dot(ckr_ref[...].astype(BF16), ek_ref[...])).astype(BF16)
            _store_values(vc_sc, _dot(ckv_b, wv_ref[...]))
    else:
        (o_ref,) = rest
    scale2 = math.log2(math.e) / math.sqrt(NOPE_DIM + ROPE_DIM)
    first_half = lax.broadcasted_iota(jnp.int32, (1, PAIR_W), 1) < V_DIM
    for sb in range(group):
        rows = slice(sb * seq_len, (sb + 1) * seq_len) if group > 1 else slice(None)
        for pair in range(N_HEADS // 2):
            vs = slice(2 * pair * PAIR_W, (2 * pair + 2) * PAIR_W)
            outs = []
            for e in range(2):
                hs = slice((2 * pair + e) * HEAD_SLOT, (2 * pair + e + 1) * HEAD_SLOT)
                qh = q_ref[rows, hs]
                s = _dot_nt(qh, k_ref[rows, hs])
                mx = jnp.max(s, axis=-1, keepdims=True)
                if has_cache:
                    sc = _dot_nt(qh, kc_sc[:, hs])
                    mx = jnp.maximum(mx, jnp.max(sc, axis=-1, keepdims=True))
                p = jnp.exp2((s - mx) * scale2)
                if has_cache:
                    o = _dot(p.astype(BF16), v_ref[rows, vs])
                    o = o + _dot(jnp.exp2((sc - mx) * scale2).astype(BF16), vc_sc[:, vs])
                    outs.append(o[:, :PAIR_W] / o[:, PAIR_W:PAIR_W + 1])
                else:
                    o = _dot(p.astype(BF16), v_ref[rows, vs.start:vs.start + PAIR_W])
                    outs.append(o / jnp.sum(p, axis=-1, keepdims=True))
            o_ref[rows, pair * PAIR_W:(pair + 1) * PAIR_W] = jnp.where(first_half, outs[0], outs[1]).astype(BF16)


def _attn_call(q, k, v, cache, *, n_seq, seq_len, tok_off, tq, group=1):
    has_cache = cache is not None
    assert group == 1 or (tq == seq_len and not has_cache)
    nq = seq_len // tq
    kv_rows = group * seq_len
    q_rows = group * tq
    kv_blk0 = tok_off // kv_rows
    q_blk0 = tok_off // q_rows
    vw = N_HEADS * V_DIM
    in_specs = [
        pl.BlockSpec((q_rows, QK_W), lambda b, j: (q_blk0 + b * nq + j, 0)),
        pl.BlockSpec((kv_rows, QK_W), lambda b, j: (kv_blk0 + b, 0)),
        pl.BlockSpec((kv_rows, V_EXT_W), lambda b, j: (kv_blk0 + b, 0)),
    ]
    args = [q, k, v]
    scratch = []
    if has_cache:
        cache_ckv, cache_krope, l, wk, wv, ek = cache
        full = lambda shape: pl.BlockSpec(shape, lambda b, j: (0,) * len(shape))
        in_specs += [pl.BlockSpec((None, None, PAST_LEN, KV_LORA), lambda b, j: (b, l, 0, 0)),
                     pl.BlockSpec((None, None, PAST_LEN, ROPE_DIM), lambda b, j: (b, l, 0, 0)),
                     _slab(wk.shape, l), _slab(wv.shape, l), full((ROPE_DIM, QK_W))]
        args += [cache_ckv, cache_krope, wk, wv, ek]
        scratch = [pltpu.VMEM((PAST_LEN, QK_W), BF16), pltpu.VMEM((PAST_LEN, V_EXT_W), BF16)]
    return pl.pallas_call(
        functools.partial(_attn_kernel, has_cache=has_cache, group=group, seq_len=seq_len),
        out_shape=jax.ShapeDtypeStruct((n_seq * seq_len, vw), BF16),
        grid=(n_seq // group, nq),
        in_specs=in_specs,
        out_specs=pl.BlockSpec((q_rows, vw), lambda b, j: (b * nq + j, 0)),
        scratch_shapes=scratch,
        compiler_params=_params(("arbitrary", "arbitrary")),
        name="mla_attention_cache" if has_cache else "mla_attention",
    )(*args)


def _hyfilt_kernel(feat_ref, w1_ref, b1_ref, w2_ref, b2_ref, w3_ref, dec_ref, alt_ref, c_ref, s_ref,
                   hre_ref, him_ref, nyq_ref, a_sc, b_sc, *, L):
    j = pl.program_id(0)
    half = HY_ORDER * HY_W

    @pl.when(j == 0)
    def _():
        h = jnp.sin(_dot_hi(feat_ref[...], w1_ref[...]) + b1_ref[...])
        h = jnp.sin(_dot_hi(h, w2_ref[...]) + b2_ref[...])
        h = _dot_x3(h, w3_ref[...])
        dec = dec_ref[...]
        rows = lax.broadcasted_iota(jnp.int32, (L, 1), 0)
        h_fwd = h[:, :half] * dec
        h_bwd = jnp.where(rows > 0, h[:, half:] * dec, 0.0)
        norm = (jnp.sum(jnp.abs(h_fwd), axis=0, keepdims=True)
                + jnp.sum(jnp.abs(h_bwd), axis=0, keepdims=True) + EPS)
        inv = 1.0 / norm
        a = (h_fwd + h_bwd) * inv
        b = (h_fwd - h_bwd) * inv
        a_sc[...] = a.astype(BF16)
        b_sc[...] = b.astype(BF16)
        nyq = jnp.sum(a * alt_ref[...], axis=0, keepdims=True)
        for o in range(HY_ORDER):
            nyq_ref[o] = nyq[:, o * HY_W:(o + 1) * HY_W]

    hre_ref[...] = _dot(c_ref[...], a_sc[...])
    him_ref[...] = -_dot(s_ref[...], b_sc[...])


def _hyfilt_call(L, l, feats, w1, b1, w2, b2, w3, decay, alt, ctab, stab):
    tk = min(L, 512)
    half = HY_ORDER * HY_W

    def full(shape):
        return pl.BlockSpec(shape, lambda j: (0,) * len(shape))

    return pl.pallas_call(
        functools.partial(_hyfilt_kernel, L=L),
        out_shape=(jax.ShapeDtypeStruct((L, half), F32), jax.ShapeDtypeStruct((L, half), F32),
                   jax.ShapeDtypeStruct((HY_ORDER, 1, HY_W), F32)),
        grid=(L // tk,),
        in_specs=[
            full((L, POS_PAD)), _slab(w1.shape, l), _slab(b1.shape, l), _slab(w2.shape, l),
            _slab(b2.shape, l), _slab(w3.shape, l), full((L, half)), full((L, 1)),
            pl.BlockSpec((tk, L), lambda j: (j, 0)), pl.BlockSpec((tk, L), lambda j: (j, 0)),
        ],
        out_specs=(pl.BlockSpec((tk, half), lambda j: (j, 0)), pl.BlockSpec((tk, half), lambda j: (j, 0)),
                   full((HY_ORDER, 1, HY_W))),
        scratch_shapes=[pltpu.VMEM((L, half), BF16), pltpu.VMEM((L, half), BF16)],
        compiler_params=_params(("arbitrary",)),
        name="hyena_filter_spectrum_%d" % L,
    )(feats, w1, b1, w2, b2, w3, decay, alt, ctab, stab)


def _hyconv_kernel(hy_ref, cw_ref, cb_ref, skip_ref, hre_ref, him_ref, nyq_ref, alt_ref, c_ref, s_ref, o_ref,
                   x1_sc, x2_sc, v_sc, vb_sc, yre_sc, yim_sc, nv_sc, *, n_seq, L, tk):
    p = pl.program_id(0)
    ph = pl.program_id(1)
    j = pl.program_id(2)
    cols = [slice(b * HY_W, (b + 1) * HY_W) for b in range(n_seq)]
    blk = pl.ds(pl.multiple_of(j * tk, tk), tk)

    @pl.when((p == 0) & (ph == 0) & (j == 0))
    def _():
        rows = lax.broadcasted_iota(jnp.int32, (L, 1), 0)
        w = cw_ref[...]
        for b in range(n_seq):
            x = hy_ref[b * L:(b + 1) * L, :]
            prev = jnp.where(rows > 0, pltpu.roll(x, 1, 0), 0.0)
            nxt = jnp.where(rows < L - 1, pltpu.roll(x, L - 1, 0), 0.0)
            u = prev * w[0:1] + x * w[1:2] + nxt * w[2:3] + cb_ref[...]
            x1_sc[:, cols[b]] = u[:, :HY_W]
            x2_sc[:, cols[b]] = u[:, HY_W:2 * HY_W]
            v_sc[:, cols[b]] = u[:, 2 * HY_W:]

    @pl.when((ph == 0) & (j == 0))
    def _():
        v = v_sc[...]
        vb_sc[...] = v.astype(BF16)
        nv_sc[...] = jnp.sum(v * alt_ref[...], axis=0, keepdims=True)

    @pl.when(ph == 0)
    def _():
        vb = vb_sc[...]
        v_re = _dot(c_ref[...], vb)
        v_im = -_dot(s_ref[...], vb)
        freq = j * tk + lax.broadcasted_iota(jnp.int32, (tk, 1), 0)
        wk = jnp.where(freq == 0, 0.5, 1.0)
        h_re = hre_ref[...] * wk
        h_im = him_ref[...] * wk
        for b in range(n_seq):
            yre_sc[blk, cols[b]] = (v_re[:, cols[b]] * h_re - v_im[:, cols[b]] * h_im).astype(BF16)
            yim_sc[blk, cols[b]] = (v_re[:, cols[b]] * h_im + v_im[:, cols[b]] * h_re).astype(BF16)

    @pl.when(ph == 1)
    def _():
        acc = _dot(c_ref[...], yre_sc[...]) - _dot(s_ref[...], yim_sc[...])
        alt = alt_ref[blk, :]
        for b in range(n_seq):
            v = v_sc[blk, cols[b]]
            nyq = nv_sc[:, cols[b]] * nyq_ref[...]
            y = acc[:, cols[b]] * (1.0 / L) + (0.5 / L) * alt * nyq + v * skip_ref[...]

            @pl.when(p == 0)
            def _():
                v_sc[blk, cols[b]] = x1_sc[blk, cols[b]] * y

            @pl.when(p == 1)
            def _():
                o_ref[pl.ds(pl.multiple_of(b * L + j * tk, tk), tk), :] = (x2_sc[blk, cols[b]] * y).astype(BF16)


def _hyconv_call(hy, l, conv_w, conv_b, skip, hre, him, nyq, alt, ctab, stab, *, n_seq, L, tok_off):
    tk = min(L, 512)
    nk = L // tk
    seg = tok_off // (n_seq * L)
    width = n_seq * HY_W

    def full(shape):
        return pl.BlockSpec(shape, lambda p, ph, j: (0,) * len(shape))

    def filt(p, ph, j):
        return (jnp.where(ph == 0, j, nk - 1), p)

    vm = lambda dtype: pltpu.VMEM((L, width), dtype)
    return pl.pallas_call(
        functools.partial(_hyconv_kernel, n_seq=n_seq, L=L, tk=tk),
        out_shape=jax.ShapeDtypeStruct((n_seq * L, HY_W), BF16),
        grid=(HY_ORDER, 2, nk),
        in_specs=[
            pl.BlockSpec((n_seq * L, 3 * HY_W), lambda p, ph, j: (seg, 0), pipeline_mode=pl.Buffered(1)),
            _slab(conv_w.shape, l), _slab(conv_b.shape, l),
            pl.BlockSpec((None, None, 1, HY_W), lambda p, ph, j: (l, p, 0, 0)),
            pl.BlockSpec((tk, HY_W), filt), pl.BlockSpec((tk, HY_W), filt),
            pl.BlockSpec((None, 1, HY_W), lambda p, ph, j: (p, 0, 0)),
            full((L, 1)),
            pl.BlockSpec((tk, L), lambda p, ph, j: (j, 0)), pl.BlockSpec((tk, L), lambda p, ph, j: (j, 0)),
        ],
        out_specs=pl.BlockSpec((n_seq * L, HY_W), lambda p, ph, j: (0, 0)),
        scratch_shapes=[vm(F32), vm(F32), vm(F32), vm(BF16), vm(BF16), vm(BF16), pltpu.VMEM((1, width), F32)],
        compiler_params=_params(("arbitrary", "arbitrary", "arbitrary")),
        name="hyena_long_conv_%d" % L,
    )(hy, conv_w, conv_b, skip, hre, him, nyq, alt, ctab, stab)


def _fnet_kernel(fcs_ref, c_ref, s_ref, o_ref, *, n_seq, L, tk, scale):
    j = pl.program_id(0)
    c = c_ref[...]
    s = s_ref[...]
    for b in range(n_seq):
        seq = slice(b * L, (b + 1) * L)
        y = (_dot(c, fcs_ref[seq, :FN_W]) - _dot(s, fcs_ref[seq, FN_W:])) * scale
        o_ref[pl.ds(pl.multiple_of(b * L + j * tk, tk), tk), :] = y.astype(BF16)


def _fnet_call(fcs, ctab, stab, *, n_seq, L, tok_off):
    tk = min(L, 512)
    seg = tok_off // (n_seq * L)
    return pl.pallas_call(
        functools.partial(_fnet_kernel, n_seq=n_seq, L=L, tk=tk, scale=1.0 / math.sqrt(L * FN_GROUP_W)),
        out_shape=jax.ShapeDtypeStruct((n_seq * L, FN_W), BF16),
        grid=(L // tk,),
        in_specs=[
            pl.BlockSpec((n_seq * L, 2 * FN_W), lambda j: (seg, 0)),
            pl.BlockSpec((tk, L), lambda j: (j, 0)), pl.BlockSpec((tk, L), lambda j: (j, 0)),
        ],
        out_specs=pl.BlockSpec((n_seq * L, FN_W), lambda j: (0, 0)),
        compiler_params=_params(("arbitrary",)),
        name="fnet_position_dft_%d" % L,
    )(fcs, ctab, stab)


def _mixout_kernel(h_ref, mod_ref, ng_ref, wg_ref, zc_ref, zl_ref, ac_ref, al_ref, fc_ref, fl_ref,
                   wa_ref, wb_ref, wc_ref, wo_ref, o_ref, wg_sc):
    s = pl.program_id(0)
    tail = IN_COLS - GATE_LOAD * GATE_BLK

    @pl.when(s < GATE_LOAD - 1)
    def _():
        wg_sc[pl.ds(pl.multiple_of(s * GATE_BLK, GATE_BLK), GATE_BLK), :] = wg_ref[...].astype(BF16)

    @pl.when(s == GATE_LOAD - 1)
    def _():
        wg_sc[(GATE_LOAD - 1) * GATE_BLK:(GATE_LOAD - 1) * GATE_BLK + tail, :] = wg_ref[:tail, :].astype(BF16)

    @pl.when(s >= GATE_LOAD)
    def _():
        i = s - GATE_LOAD
        is_ctx = i < T_CTX // TM_OUT
        m = mod_ref[_group_of(i * TM_OUT)]
        g0, g1, g2, g3 = (GATE_OFF + b * D_MODEL for b in range(N_BRANCH + 1))
        for sub in range(TM_OUT // OUT_SUB_ROWS):
            rows = slice(sub * OUT_SUB_ROWS, (sub + 1) * OUT_SUB_ROWS)
            h = h_ref[rows, :]
            n = (_rms(h, ng_ref[...]) * (1.0 + m[4:5]) + m[3:4]).astype(BF16)
            z = jnp.where(is_ctx, zc_ref[rows, :], zl_ref[rows, :])
            a = jnp.where(is_ctx, ac_ref[rows, :], al_ref[rows, :])
            f = jnp.where(is_ctx, fc_ref[rows, :], fl_ref[rows, :])
            acc = _sigmoid(_dot_nt(n, wg_sc[g0:g1, :])) * _dot(z, wa_ref[...])
            acc = acc + _sigmoid(_dot_nt(n, wg_sc[g1:g2, :])) * _dot(a, wb_ref[...])
            acc = acc + _sigmoid(_dot_nt(n, wg_sc[g2:g3, :])) * _dot(f, wc_ref[...])
            y = _dot(acc.astype(BF16), wo_ref[...])
            o_ref[rows, :] = h + m[5:6] * y


def _mixout_call(h, mods, ng, w_in_t, l, z, a, f, wa, wb, wc, wo):
    tm = TM_OUT
    n_ctx = T_CTX // tm
    weight = functools.partial(_slab, single_buffer=True)
    tile = lambda s: jnp.maximum(s - GATE_LOAD, 0)

    def tok(width):
        return pl.BlockSpec((tm, width), lambda s: (tile(s), 0))

    def pair(width):
        return [pl.BlockSpec((tm, width), lambda s: (jnp.minimum(tile(s), n_ctx - 1), 0)),
                pl.BlockSpec((tm, width), lambda s: (jnp.maximum(tile(s) - n_ctx, 0), 0))]

    return pl.pallas_call(
        _mixout_kernel,
        out_shape=jax.ShapeDtypeStruct((T_ALL, D_MODEL), F32),
        grid=(GATE_LOAD + T_ALL // tm,),
        in_specs=[
            tok(D_MODEL), _slab(mods.shape, l), _slab(ng.shape, l, 1),
            pl.BlockSpec((None, GATE_BLK, D_MODEL), lambda s: (l, jnp.minimum(s, GATE_LOAD - 1) + 1, 0),
                         pipeline_mode=pl.Buffered(1)),
            *pair(HY_W), *pair(N_HEADS * V_DIM), *pair(FN_W),
            weight(wa.shape, l), weight(wb.shape, l), weight(wc.shape, l), weight(wo.shape, l),
        ],
        out_specs=tok(D_MODEL),
        scratch_shapes=[pltpu.VMEM((GATE_LOAD * GATE_BLK, D_MODEL), BF16)],
        compiler_params=_params(("arbitrary",)),
        name="gated_merge_out_proj",
    )(h, mods, ng, w_in_t, *z, *a, *f, wa, wb, wc, wo)


def _head_slot_weights(w_qb, w_kvb):
    wq3 = w_qb.reshape(DEPTH, Q_LORA, N_HEADS, NOPE_DIM + ROPE_DIM)
    slot_pad = HEAD_SLOT - NOPE_DIM - ROPE_DIM
    no_pad = ((0, 0),) * 3
    wq = jnp.pad(wq3, no_pad + ((0, slot_pad),)).reshape(DEPTH, Q_LORA, QK_W).astype(BF16)
    rope = wq3[..., NOPE_DIM:].reshape(DEPTH, Q_LORA, N_HEADS, 2, 2, ROPE_HALF)
    partner = jnp.stack([-rope[..., 1, :], rope[..., 0, :]], axis=4).reshape(DEPTH, Q_LORA, N_HEADS, ROPE_DIM)
    wqs = jnp.pad(partner, no_pad + ((NOPE_DIM, slot_pad),)).reshape(DEPTH, Q_LORA, QK_W).astype(BF16)

    wkv3 = w_kvb.reshape(DEPTH, KV_LORA, N_HEADS, NOPE_DIM + V_DIM)
    wk = jnp.pad(wkv3[..., :NOPE_DIM], no_pad + ((0, HEAD_SLOT - NOPE_DIM),))
    wk = wk.reshape(DEPTH, KV_LORA, QK_W).astype(BF16)
    wv = wkv3[..., NOPE_DIM:].reshape(DEPTH, KV_LORA, N_HEADS * V_DIM).astype(BF16)
    return wq, wqs, wk, wv


def kernel(x_prompt, x_sample, cache_ckv, cache_krope, c, c_ctx, w_ada, b_ada, norm_g, w_ffn_up, w_ffn_down,
           w_in, hy_conv_w, hy_conv_b, hy_filt_w1, hy_filt_b1, hy_filt_w2, hy_filt_b2, hy_filt_w3, hy_skip,
           w_hy_out, q_norm_g, w_qb, kv_norm_g, w_kvb, w_mla_o, w_fnet, w_out, final_g):
    tabs = _tables()
    cosq, sinq, cosk, sink = (jnp.asarray(t) for t in tabs["rope"])
    fch = jnp.asarray(tabs["fnch"]).astype(BF16)
    ek = jnp.asarray(tabs["ropeexp"]).astype(BF16)
    dft = {}
    for L in (SEQ, DEC_SEQ):
        dft[("hy", L)] = tuple(jnp.asarray(t).astype(BF16) for t in tabs[("hy", L)])
        dft[("fn", L)] = tuple(jnp.asarray(t).astype(BF16) for t in tabs[("fn", L)])

    cvec = jnp.concatenate([c_ctx[None, :], c, jnp.zeros((SUBLANES - N_GROUPS, D_MODEL), F32)], axis=0).T
    ada = _ada_call(cvec, w_ada, b_ada)
    mods = ada[:, :N_GROUPS].reshape(DEPTH, N_GROUPS, 9, D_MODEL)
    ng = norm_g.reshape(DEPTH, 3, 1, D_MODEL)
    qg = q_norm_g.reshape(DEPTH, 1, Q_LORA)
    kvg = kv_norm_g.reshape(DEPTH, 1, KV_LORA)
    wq, wqs, wk, wv = _head_slot_weights(w_qb, w_kvb)
    w1 = jnp.pad(hy_filt_w1, ((0, 0), (0, POS_PAD - POS_EMB), (0, 0)))
    b1 = hy_filt_b1.reshape(DEPTH, 1, FILT_HID)
    b2 = hy_filt_b2.reshape(DEPTH, 1, FILT_HID)
    conv_b = hy_conv_b.reshape(DEPTH, 1, 3 * HY_W)
    skip = hy_skip.reshape(DEPTH, HY_ORDER, 1, HY_W)

    w_in_t = jnp.swapaxes(w_in, 1, 2)
    w_hy_out, w_mla_o, w_fnet, w_out = (w.astype(BF16) for w in (w_hy_out, w_mla_o, w_fnet, w_out))
    hs = (x_prompt.reshape(T_CTX, D_MODEL), x_sample.reshape(T_LAT, D_MODEL))
    segs = ((BATCH, SEQ, 0), (DEC_BATCH, DEC_SEQ, T_CTX))
    ckv_out = []
    kr_out = []
    for l in range(DEPTH):
        h = _ffn_call(hs, mods, ng, w_ffn_up, w_ffn_down, l, 0, 0)

        hy, q, k, v, ckv, k_r, fcs = _mixin_call(
            h, mods, ng, w_in_t, l, qg, kvg, wq, wqs, wk, wv, fch, cosq, sinq, cosk, sink)
        ckv_out.append(ckv.reshape(BATCH, SEQ, KV_LORA))
        kr_out.append(k_r.reshape(BATCH, SEQ, ROPE_DIM))

        z_parts, a_parts, f_parts = [], [], []
        for n_seq, L, off in segs:
            feats, decay, alt = (jnp.asarray(t) for t in tabs[("filt", L)])
            c_hy, s_hy = dft[("hy", L)]
            hre, him, nyq = _hyfilt_call(L, l, feats, w1, b1, hy_filt_w2, b2, hy_filt_w3, decay, alt, c_hy, s_hy)
            z_parts.append(_hyconv_call(hy, l, hy_conv_w, conv_b, skip, hre, him, nyq, alt, c_hy, s_hy,
                                        n_seq=n_seq, L=L, tok_off=off))
            if off:
                a_parts.append(_attn_call(q, k, v, (cache_ckv, cache_krope, l, wk, wv, ek),
                                          n_seq=n_seq, seq_len=L, tok_off=off, tq=ATTN_TQ))
            else:
                a_parts.append(_attn_call(q, k, v, None, n_seq=n_seq, seq_len=L, tok_off=off, tq=L,
                                          group=ATTN_GROUP))
            c_fn, s_fn = dft[("fn", L)]
            f_parts.append(_fnet_call(fcs, c_fn, s_fn, n_seq=n_seq, L=L, tok_off=off))
        h = _mixout_call(h, mods, ng, w_in_t, l, z_parts, a_parts, f_parts, w_hy_out, w_mla_o, w_fnet, w_out)

        last = l == DEPTH - 1
        out = _ffn_call((h,), mods, ng, w_ffn_up, w_ffn_down, l, 1, 6, final_g[None, :] if last else None)
        hs = out if last else (out,)

    y_prompt = hs[0].reshape(BATCH, SEQ, D_MODEL)
    y_sample = hs[1].reshape(DEC_BATCH, DEC_SEQ, D_MODEL)
    return y_prompt, y_sample, jnp.stack(ckv_out, axis=1), jnp.stack(kr_out, axis=1)
```

```python
import functools
import math

import numpy as np
import jax
import jax.numpy as jnp
from jax import lax
from jax.experimental import pallas as pl
from jax.experimental.pallas import tpu as pltpu

F32 = jnp.float32
BF16 = jnp.bfloat16
HIGHEST = lax.Precision.HIGHEST

D_MODEL = 1024
BATCH = 16
SEQ = 256
DEPTH = 2
DEC_BATCH = 2
DEC_SEQ = 2048
PAST_LEN = 512
GRID_W = 64
HY_W = 256
HY_ORDER = 2
N_BANDS = 8
POS_EMB = 1 + 2 * N_BANDS
FILT_HID = 64
N_HEADS = 8
Q_LORA = 256
KV_LORA = 128
NOPE_DIM = 64
ROPE_DIM = 32
V_DIM = 64
ROPE_BASE = 10000.0
FN_GROUPS = 4
FN_GROUP_W = 64
FN_W = FN_GROUPS * FN_GROUP_W
N_BRANCH = 3
D_FF = 2816
EPS = 1e-6

T_CTX = BATCH * SEQ
T_LAT = DEC_BATCH * DEC_SEQ
T_ALL = T_CTX + T_LAT
N_GROUPS = 1 + DEC_BATCH
LANES = 128
SUBLANES = 8
VMEM_BYTES = 64 * 1024 * 1024
VMEM_LIMIT = VMEM_BYTES - 8 * 1024 * 1024

HEAD_SLOT = LANES
QK_W = N_HEADS * HEAD_SLOT
PAIR_W = 2 * V_DIM
V_EXT_W = N_HEADS * PAIR_W
POS_PAD = LANES
assert PAIR_W == LANES and NOPE_DIM + ROPE_DIM <= HEAD_SLOT

TM_FFN = 512
TF_FFN = 256
FFN_LOAD = 8
FFN_UP_ROWS = D_MODEL // FFN_LOAD
FFN_DN_ROWS = D_FF // FFN_LOAD
TM_FFN_WIDE = 1024
FFN_SUB_ROWS = 256
TM_MIX = 512
TM_OUT = 1024
OUT_SUB_ROWS = 256
ADA_TN = 2304
ATTN_TQ = 512
ATTN_GROUP = 4
GATE_LOAD = 3

IN_KR = 3 * HY_W + Q_LORA + KV_LORA
IN_FN = IN_KR + ROPE_DIM
IN_GATE = IN_FN + FN_W
IN_COLS = IN_GATE + N_BRANCH * D_MODEL
GATE_BLK = IN_KR
GATE_OFF = IN_GATE - GATE_BLK
assert (GATE_LOAD + 1) * GATE_BLK >= IN_COLS

MIX_HY = 0
MIX_QA = 3 * HY_W
MIX_KVA = MIX_QA + Q_LORA
MIX_FN = MIX_KVA + KV_LORA
MIX_KR = MIX_FN + FN_W
MIX_W = MIX_KR + LANES
ROPE_HALF = ROPE_DIM // 4


def _dft_tables(L, half):
    k = np.arange(L, dtype=np.int64)
    period = 2 * L if half else L
    m = (k[:, None] * k[None, :]) % period
    ang = 2.0 * np.pi * m.astype(np.float64) / period
    return np.cos(ang).astype(np.float32), np.sin(ang).astype(np.float32)


def _filter_tables(L):
    t = np.arange(L, dtype=np.float64)
    t_norm = t / (L - 1)
    w = 2.0 * np.pi * t / L
    bands = np.linspace(1e-4, N_BANDS - 1, N_BANDS)
    ang = w[:, None] * bands[None, :]
    feats = np.concatenate([t_norm[:, None], np.cos(ang), -np.sin(ang)], axis=-1)
    feats = np.pad(feats, ((0, 0), (0, POS_PAD - POS_EMB)))
    deltas = np.linspace(math.log(1e-2) / 1.5, math.log(1e-2) / 0.3, HY_W)
    decay = np.exp(-t_norm[:, None] * np.abs(deltas)[None, :])
    decay = np.concatenate([decay, decay], axis=1)
    alt = np.where(np.arange(L) % 2 == 0, 1.0, -1.0)[:, None]
    return feats.astype(np.float32), decay.astype(np.float32), alt.astype(np.float32)


def _rope_tables():
    t = np.arange(DEC_SEQ)
    row = (t // GRID_W).astype(np.float64)
    col = (t % GRID_W).astype(np.float64)
    nf = ROPE_DIM // 4
    inv = ROPE_BASE ** (-np.arange(nf, dtype=np.float64) / nf)
    ar = row[:, None] * inv[None, :]
    ac = col[:, None] * inv[None, :]
    cos32 = np.concatenate([np.cos(ar), np.cos(ar), np.cos(ac), np.cos(ac)], axis=1)
    sin32 = np.concatenate([np.sin(ar), np.sin(ar), np.sin(ac), np.sin(ac)], axis=1)
    cos32 = np.concatenate([cos32, np.ones((TM_MIX, ROPE_DIM))], axis=0)
    sin32 = np.concatenate([sin32, np.zeros((TM_MIX, ROPE_DIM))], axis=0)
    n = cos32.shape[0]
    cosq = np.ones((n, HEAD_SLOT))
    sinq = np.zeros((n, HEAD_SLOT))
    cosq[:, NOPE_DIM:NOPE_DIM + ROPE_DIM] = cos32
    sinq[:, NOPE_DIM:NOPE_DIM + ROPE_DIM] = sin32
    cosk = np.zeros((n, HEAD_SLOT))
    sink = np.zeros((n, HEAD_SLOT))
    cosk[:, :ROPE_DIM] = cos32
    sink[:, :ROPE_DIM] = sin32
    return tuple(t.astype(np.float32) for t in (cosq, sinq, cosk, sink))


def _fnet_channel_table():
    j = np.arange(FN_GROUP_W)
    ang = 2.0 * np.pi * ((j[:, None] * j[None, :]) % FN_GROUP_W) / FN_GROUP_W
    out = np.zeros((FN_W, 2 * FN_W))
    for g in range(FN_GROUPS):
        sl = slice(g * FN_GROUP_W, (g + 1) * FN_GROUP_W)
        out[sl, sl] = np.cos(ang)
        out[sl, FN_W + g * FN_GROUP_W:FN_W + (g + 1) * FN_GROUP_W] = np.sin(ang)
    return out.astype(np.float32)


def _rope_expand_table():
    e = np.zeros((ROPE_DIM, N_HEADS, HEAD_SLOT), np.float32)
    for j in range(ROPE_DIM):
        e[j, :, NOPE_DIM + j] = 1.0
    return e.reshape(ROPE_DIM, QK_W)


_TABLES = {}


def _tables():
    if not _TABLES:
        for L in (SEQ, DEC_SEQ):
            _TABLES[("hy", L)] = _dft_tables(L, True)
            _TABLES[("fn", L)] = _dft_tables(L, False)
            _TABLES[("filt", L)] = _filter_tables(L)
        _TABLES["rope"] = _rope_tables()
        _TABLES["fnch"] = _fnet_channel_table()
        _TABLES["ropeexp"] = _rope_expand_table()
    return _TABLES


def _rms(x, g):
    ms = jnp.mean(x * x, axis=-1, keepdims=True)
    return x * lax.rsqrt(ms + EPS) * g


def _sigmoid(x):
    return 1.0 / (1.0 + jnp.exp(-x))


def _dot(a, b):
    return jnp.dot(a, b, preferred_element_type=F32)


def _dot_hi(a, b):
    return jnp.dot(a, b, precision=HIGHEST, preferred_element_type=F32)


def _dot_x3(a, b):
    a_hi = a.astype(BF16)
    b_hi = b.astype(BF16)
    a_lo = (a - a_hi.astype(F32)).astype(BF16)
    b_lo = (b - b_hi.astype(F32)).astype(BF16)
    return _dot(a_hi, b_hi) + (_dot(a_hi, b_lo) + _dot(a_lo, b_hi))


def _dot_nt(a, b):
    return lax.dot_general(a, b, (((1,), (1,)), ((), ())), preferred_element_type=F32)


def _store_values(v_ref, v):
    ones = jnp.ones((v.shape[0], PAIR_W), BF16)
    for pair in range(N_HEADS // 2):
        v_ref[:, 2 * pair * PAIR_W:(2 * pair + 1) * PAIR_W] = v[:, pair * PAIR_W:(pair + 1) * PAIR_W].astype(BF16)
        v_ref[:, (2 * pair + 1) * PAIR_W:(2 * pair + 2) * PAIR_W] = ones


def _group_of(tok0):
    return jnp.where(tok0 < T_CTX, 0, 1 + (tok0 - T_CTX) // DEC_SEQ)


def _slab(shape, *lead, single_buffer=False):
    tail = tuple(shape[len(lead):])
    mode = dict(pipeline_mode=pl.Buffered(1)) if single_buffer else {}
    return pl.BlockSpec((None,) * len(lead) + tail, lambda *_: tuple(lead) + (0,) * len(tail), **mode)


def _params(sem):
    return pltpu.CompilerParams(dimension_semantics=sem, vmem_limit_bytes=VMEM_LIMIT)


def _ada_kernel(c_ref, w_ref, b_ref, o_ref):
    x = c_ref[...]
    s = x * _sigmoid(x)
    w = w_ref[...]
    o_ref[...] = jnp.zeros_like(o_ref)
    for g in range(N_GROUPS):
        o_ref[g:g + 1, :] = jnp.sum(w * s[:, g:g + 1], axis=0, keepdims=True) + b_ref[...]


def _ada_call(cvec, w_ada, b_ada):
    tn = ADA_TN
    n_out = 9 * D_MODEL
    return pl.pallas_call(
        _ada_kernel,
        out_shape=jax.ShapeDtypeStruct((DEPTH, SUBLANES, n_out), F32),
        grid=(DEPTH, n_out // tn),
        in_specs=[
            pl.BlockSpec((D_MODEL, SUBLANES), lambda l, j: (0, 0)),
            pl.BlockSpec((None, D_MODEL, tn), lambda l, j: (l, 0, j)),
            pl.BlockSpec((None, 1, tn), lambda l, j: (l, 0, j)),
        ],
        out_specs=pl.BlockSpec((None, SUBLANES, tn), lambda l, j: (l, 0, j)),
        compiler_params=_params(("arbitrary", "arbitrary")),
        name="ada_modulation",
    )(cvec, w_ada, b_ada.reshape(DEPTH, 1, n_out))


def _ffn_kernel(*refs, j0, split_in, final, tm):
    refs = list(refs)
    x_refs = [refs.pop(0) for _ in range(2 if split_in else 1)]
    mod_ref, ng_ref, wup_ref, wd_ref = refs[:4]
    refs = refs[4:]
    fg_ref = refs.pop(0) if final else None
    o_refs = [refs.pop(0) for _ in range(2 if final else 1)]
    wup_sc, wd_sc, hid_sc = refs
    s = pl.program_id(0)
    n_ctx_tiles = T_CTX // tm

    @pl.when(s < FFN_LOAD)
    def _():
        wup_sc[pl.ds(pl.multiple_of(s * FFN_UP_ROWS, FFN_UP_ROWS), FFN_UP_ROWS), :] = wup_ref[...].astype(BF16)
        wd_sc[pl.ds(pl.multiple_of(s * FFN_DN_ROWS, FFN_DN_ROWS), FFN_DN_ROWS), :] = wd_ref[...].astype(BF16)

    @pl.when(s >= FFN_LOAD)
    def _():
        t = s - FFN_LOAD
        m = mod_ref[_group_of(t * tm)]
        ys = []
        for sub in range(tm // FFN_SUB_ROWS):
            rows = slice(sub * FFN_SUB_ROWS, (sub + 1) * FFN_SUB_ROWS)
            if split_in:
                x = jnp.where(t < n_ctx_tiles, x_refs[0][rows, :], x_refs[1][rows, :])
            else:
                x = x_refs[0][rows, :]
            n = (_rms(x, ng_ref[...]) * (1.0 + m[j0 + 1:j0 + 2]) + m[j0:j0 + 1]).astype(BF16)
            for c in range(D_FF // TF_FFN):
                cols = slice(c * TF_FFN, (c + 1) * TF_FFN)
                g = _dot(n, wup_sc[:, cols])
                u = _dot(n, wup_sc[:, D_FF + c * TF_FFN:D_FF + (c + 1) * TF_FFN])
                hid_sc[rows, cols] = (g * _sigmoid(g) * u).astype(BF16)
            y = x + 0.5 * m[j0 + 2:j0 + 3] * _dot(hid_sc[rows, :], wd_sc[...])
            if final:
                y = _rms(y, fg_ref[...])
            ys.append((rows, y))
        if final:
            @pl.when(t < n_ctx_tiles)
            def _():
                for rows, y in ys:
                    o_refs[0][rows, :] = y

            @pl.when(t >= n_ctx_tiles)
            def _():
                for rows, y in ys:
                    o_refs[1][rows, :] = y
        else:
            for rows, y in ys:
                o_refs[0][rows, :] = y


def _ffn_call(xs, mods, ng, w_up, w_down, l, f, j0, final_g=None):
    split_in = len(xs) == 2
    final = final_g is not None
    tm = TM_FFN if (split_in or final) else TM_FFN_WIDE
    n_ctx_tiles = T_CTX // tm
    tile = lambda s: jnp.maximum(s - FFN_LOAD, 0)
    chunk = lambda s: jnp.minimum(s, FFN_LOAD - 1)
    ctx_blk = lambda s: (jnp.minimum(tile(s), n_ctx_tiles - 1), 0)
    lat_blk = lambda s: (jnp.maximum(tile(s) - n_ctx_tiles, 0), 0)
    row = pl.BlockSpec((1, D_MODEL), lambda s: (0, 0))
    tok = lambda index_map: pl.BlockSpec((tm, D_MODEL), index_map)
    if split_in:
        in_specs = [tok(ctx_blk), tok(lat_blk)]
    else:
        in_specs = [tok(lambda s: (tile(s), 0))]
    in_specs += [
        _slab(mods.shape, l), _slab(ng.shape, l, j0 // 3),
        pl.BlockSpec((None, None, FFN_UP_ROWS, 2 * D_FF), lambda s: (l, f, chunk(s), 0)),
        pl.BlockSpec((None, None, FFN_DN_ROWS, D_MODEL), lambda s: (l, f, chunk(s), 0)),
    ]
    args = list(xs) + [mods, ng, w_up, w_down]
    if final:
        in_specs.append(row)
        args.append(final_g)
        out_shape = (jax.ShapeDtypeStruct((T_CTX, D_MODEL), F32), jax.ShapeDtypeStruct((T_LAT, D_MODEL), F32))
        out_specs = (tok(ctx_blk), tok(lat_blk))
    else:
        out_shape = jax.ShapeDtypeStruct((T_ALL, D_MODEL), F32)
        out_specs = tok(lambda s: (tile(s), 0))
    return pl.pallas_call(
        functools.partial(_ffn_kernel, j0=j0, split_in=split_in, final=final, tm=tm),
        out_shape=out_shape,
        grid=(FFN_LOAD + T_ALL // tm,),
        in_specs=in_specs,
        out_specs=out_specs,
        scratch_shapes=[pltpu.VMEM((D_MODEL, 2 * D_FF), BF16), pltpu.VMEM((D_FF, D_MODEL), BF16),
                        pltpu.VMEM((tm, D_FF), BF16)],
        compiler_params=_params(("arbitrary",)),
        name="swiglu_half_step",
    )(*args)


def _rope_partner(x):
    lane = lax.broadcasted_iota(jnp.int32, (1, HEAD_SLOT), 1)
    first = (lane % (2 * ROPE_HALF)) < ROPE_HALF
    return jnp.where(first, -pltpu.roll(x, HEAD_SLOT - ROPE_HALF, 1), pltpu.roll(x, ROPE_HALF, 1))


def _mixin_kernel(h_ref, mod_ref, ng_ref, w_ref, qg_ref, kvg_ref, wq_ref, wqs_ref, wk_ref, wv_ref,
                  fch_ref, cq_ref, sq_ref, ck_ref, sk_ref,
                  hy_ref, q_ref, k_ref, v_ref, ckv_ref, kr_ref, fcs_ref, wmix_sc):
    i = pl.program_id(0)

    @pl.when(i == 0)
    def _():
        wmix_sc[:MIX_FN, :] = w_ref[:IN_KR, :].astype(BF16)
        wmix_sc[MIX_FN:MIX_KR, :] = w_ref[IN_FN:IN_GATE, :].astype(BF16)
        wmix_sc[MIX_KR:MIX_KR + ROPE_DIM, :] = w_ref[IN_KR:IN_FN, :].astype(BF16)
        wmix_sc[MIX_KR + ROPE_DIM:, :] = jnp.zeros((MIX_W - MIX_KR - ROPE_DIM, D_MODEL), BF16)

    m = mod_ref[_group_of(i * TM_MIX)]
    n = (_rms(h_ref[...], ng_ref[...]) * (1.0 + m[4:5]) + m[3:4]).astype(BF16)
    proj = _dot_nt(n, wmix_sc[...])
    hy_ref[...] = proj[:, MIX_HY:MIX_QA]
    q_a = proj[:, MIX_QA:MIX_KVA]
    kv_a = proj[:, MIX_KVA:MIX_FN]
    fn = proj[:, MIX_FN:MIX_KR]
    k_r = proj[:, MIX_KR:MIX_W]

    qn = _rms(q_a, qg_ref[...]).astype(BF16)
    q = _dot(qn, wq_ref[...])
    q_partner = _dot(qn, wqs_ref[...])
    cos_q = cq_ref[...]
    sin_q = sq_ref[...]
    for h in range(N_HEADS):
        hs = slice(h * HEAD_SLOT, (h + 1) * HEAD_SLOT)
        q_ref[:, hs] = (q[:, hs] * cos_q + q_partner[:, hs] * sin_q).astype(BF16)

    ckv = _rms(kv_a, kvg_ref[...])

    @pl.when(i < T_CTX // TM_MIX)
    def _():
        ckv_ref[...] = ckv
        kr_ref[...] = k_r[:, :ROPE_DIM]

    ckv_b = ckv.astype(BF16)
    k_rot = k_r * ck_ref[...] + _rope_partner(k_r) * sk_ref[...]
    k_rope = pltpu.roll(k_rot, NOPE_DIM, 1)
    k_nope = _dot(ckv_b, wk_ref[...])
    for h in range(N_HEADS):
        hs = slice(h * HEAD_SLOT, (h + 1) * HEAD_SLOT)
        k_ref[:, hs] = (k_nope[:, hs] + k_rope).astype(BF16)
    _store_values(v_ref, _dot(ckv_b, wv_ref[...]))
    fcs_ref[...] = _dot(fn.astype(BF16), fch_ref[...]).astype(BF16)


def _mixin_call(h, mods, ng, w_in_t, l, qg, kvg, wq, wqs, wk, wv, fch, cosq, sinq, cosk, sink):
    tm = TM_MIX

    def pos_block(i):
        tok0 = i * tm
        return jnp.where(tok0 < T_CTX, DEC_SEQ // tm, ((tok0 - T_CTX) % DEC_SEQ) // tm)

    def full(shape):
        return pl.BlockSpec(shape, lambda i: (0,) * len(shape))

    def tok(width):
        return pl.BlockSpec((tm, width), lambda i: (i, 0))

    def pos(width):
        return pl.BlockSpec((tm, width), lambda i: (pos_block(i), 0))

    out_shape = (
        jax.ShapeDtypeStruct((T_ALL, 3 * HY_W), F32),
        jax.ShapeDtypeStruct((T_ALL, QK_W), BF16),
        jax.ShapeDtypeStruct((T_ALL, QK_W), BF16),
        jax.ShapeDtypeStruct((T_ALL, V_EXT_W), BF16),
        jax.ShapeDtypeStruct((T_CTX, KV_LORA), F32),
        jax.ShapeDtypeStruct((T_CTX, ROPE_DIM), F32),
        jax.ShapeDtypeStruct((T_ALL, 2 * FN_W), BF16),
    )
    return pl.pallas_call(
        _mixin_kernel,
        out_shape=out_shape,
        grid=(T_ALL // tm,),
        in_specs=[
            tok(D_MODEL), _slab(mods.shape, l), _slab(ng.shape, l, 1),
            pl.BlockSpec((None, IN_GATE, D_MODEL), lambda i: (l, 0, 0), pipeline_mode=pl.Buffered(1)),
            _slab(qg.shape, l), _slab(kvg.shape, l), _slab(wq.shape, l), _slab(wqs.shape, l),
            _slab(wk.shape, l), _slab(wv.shape, l),
            full((FN_W, 2 * FN_W)), pos(HEAD_SLOT), pos(HEAD_SLOT), pos(HEAD_SLOT), pos(HEAD_SLOT),
        ],
        out_specs=tuple(
            tok(s.shape[1]) if s.shape[0] == T_ALL else
            pl.BlockSpec((tm, s.shape[1]), lambda i: (jnp.minimum(i, T_CTX // tm - 1), 0))
            for s in out_shape),
        scratch_shapes=[pltpu.VMEM((MIX_W, D_MODEL), BF16)],
        compiler_params=_params(("arbitrary",)),
        name="mixer_input_proj",
    )(h, mods, ng, w_in_t, qg, kvg, wq, wqs, wk, wv, fch, cosq, sinq, cosk, sink)


def _attn_kernel(q_ref, k_ref, v_ref, *rest, has_cache, group, seq_len):
    if has_cache:
        cckv_ref, ckr_ref, wk_ref, wv_ref, ek_ref, o_ref, kc_sc, vc_sc = rest

        @pl.when(pl.program_id(1) == 0)
        def _():
            ckv_b = cckv_ref[...].astype(BF16)
            kc_sc[...] = (_dot(ckv_b, wk_ref[...]) + _dot(ckr_ref[...].astype(BF16), ek_ref[...])).astype(BF16)
            _store_values(vc_sc, _dot(ckv_b, wv_ref[...]))
    else:
        (o_ref,) = rest
    scale2 = math.log2(math.e) / math.sqrt(NOPE_DIM + ROPE_DIM)
    first_half = lax.broadcasted_iota(jnp.int32, (1, PAIR_W), 1) < V_DIM
    for sb in range(group):
        rows = slice(sb * seq_len, (sb + 1) * seq_len) if group > 1 else slice(None)
        for pair in range(N_HEADS // 2):
            vs = slice(2 * pair * PAIR_W, (2 * pair + 2) * PAIR_W)
            outs = []
            for e in range(2):
                hs = slice((2 * pair + e) * HEAD_SLOT, (2 * pair + e + 1) * HEAD_SLOT)
                qh = q_ref[rows, hs]
                s = _dot_nt(qh, k_ref[rows, hs])
                mx = jnp.max(s, axis=-1, keepdims=True)
                if has_cache:
                    sc = _dot_nt(qh, kc_sc[:, hs])
                    mx = jnp.maximum(mx, jnp.max(sc, axis=-1, keepdims=True))
                p = jnp.exp2((s - mx) * scale2)
                if has_cache:
                    o = _dot(p.astype(BF16), v_ref[rows, vs])
                    o = o + _dot(jnp.exp2((sc - mx) * scale2).astype(BF16), vc_sc[:, vs])
                    outs.append(o[:, :PAIR_W] / o[:, PAIR_W:PAIR_W + 1])
                else:
                    o = _dot(p.astype(BF16), v_ref[rows, vs.start:vs.start + PAIR_W])
                    outs.append(o / jnp.sum(p, axis=-1, keepdims=True))
            o_ref[rows, pair * PAIR_W:(pair + 1) * PAIR_W] = jnp.where(first_half, outs[0], outs[1]).astype(BF16)


def _attn_call(q, k, v, cache, *, n_seq, seq_len, tok_off, tq, group=1):
    has_cache = cache is not None
    assert group == 1 or (tq == seq_len and not has_cache)
    nq = seq_len // tq
    kv_rows = group * seq_len
    q_rows = group * tq
    kv_blk0 = tok_off // kv_rows
    q_blk0 = tok_off // q_rows
    vw = N_HEADS * V_DIM
    in_specs = [
        pl.BlockSpec((q_rows, QK_W), lambda b, j: (q_blk0 + b * nq + j, 0)),
        pl.BlockSpec((kv_rows, QK_W), lambda b, j: (kv_blk0 + b, 0)),
        pl.BlockSpec((kv_rows, V_EXT_W), lambda b, j: (kv_blk0 + b, 0)),
    ]
    args = [q, k, v]
    scratch = []
    if has_cache:
        cache_ckv, cache_krope, l, wk, wv, ek = cache
        full = lambda shape: pl.BlockSpec(shape, lambda b, j: (0,) * len(shape))
        in_specs += [pl.BlockSpec((None, None, PAST_LEN, KV_LORA), lambda b, j: (b, l, 0, 0)),
                     pl.BlockSpec((None, None, PAST_LEN, ROPE_DIM), lambda b, j: (b, l, 0, 0)),
                     _slab(wk.shape, l), _slab(wv.shape, l), full((ROPE_DIM, QK_W))]
        args += [cache_ckv, cache_krope, wk, wv, ek]
        scratch = [pltpu.VMEM((PAST_LEN, QK_W), BF16), pltpu.VMEM((PAST_LEN, V_EXT_W), BF16)]
    return pl.pallas_call(
        functools.partial(_attn_kernel, has_cache=has_cache, group=group, seq_len=seq_len),
        out_shape=jax.ShapeDtypeStruct((n_seq * seq_len, vw), BF16),
        grid=(n_seq // group, nq),
        in_specs=in_specs,
        out_specs=pl.BlockSpec((q_rows, vw), lambda b, j: (b * nq + j, 0)),
        scratch_shapes=scratch,
        compiler_params=_params(("arbitrary", "arbitrary")),
        name="mla_attention_cache" if has_cache else "mla_attention",
    )(*args)


def _hyfilt_kernel(feat_ref, w1_ref, b1_ref, w2_ref, b2_ref, w3_ref, dec_ref, alt_ref, c_ref, s_ref,
                   hre_ref, him_ref, nyq_ref, a_sc, b_sc, *, L):
    j = pl.program_id(0)
    half = HY_ORDER * HY_W

    @pl.when(j == 0)
    def _():
        w1 = w1_ref[...]
        w2 = w2_ref[...]
        w3 = w3_ref[...]
        zero = jnp.zeros_like(w2)
        w2_pair = jnp.concatenate([jnp.concatenate([w2, zero], axis=1), jnp.concatenate([zero, w2], axis=1)], axis=0)
        b1_pair = jnp.concatenate([b1_ref[...], b1_ref[...]], axis=1)
        b2_pair = jnp.concatenate([b2_ref[...], b2_ref[...]], axis=1)
        h = jnp.concatenate([_dot_hi(feat_ref[:L // 2, :], w1), _dot_hi(feat_ref[L // 2:, :], w1)], axis=1)
        h = jnp.sin(h + b1_pair)
        h = jnp.sin(_dot_hi(h, w2_pair) + b2_pair)
        h = jnp.concatenate([_dot_x3(h[:, :FILT_HID], w3), _dot_x3(h[:, FILT_HID:], w3)], axis=0)
        dec = dec_ref[...]
        rows = lax.broadcasted_iota(jnp.int32, (L, 1), 0)
        h_fwd = h[:, :half] * dec
        h_bwd = jnp.where(rows > 0, h[:, half:] * dec, 0.0)
        norm = (jnp.sum(jnp.abs(h_fwd), axis=0, keepdims=True)
                + jnp.sum(jnp.abs(h_bwd), axis=0, keepdims=True) + EPS)
        inv = 1.0 / norm
        a = (h_fwd + h_bwd) * inv
        b = (h_fwd - h_bwd) * inv
        a_sc[...] = a.astype(BF16)
        b_sc[...] = b.astype(BF16)
        nyq = jnp.sum(a * alt_ref[...], axis=0, keepdims=True)
        for o in range(HY_ORDER):
            nyq_ref[o] = nyq[:, o * HY_W:(o + 1) * HY_W]

    hre_ref[...] = _dot(c_ref[...], a_sc[...])
    him_ref[...] = -_dot(s_ref[...], b_sc[...])


def _hyfilt_call(L, l, feats, w1, b1, w2, b2, w3, decay, alt, ctab, stab):
    tk = min(L, 512)
    half = HY_ORDER * HY_W

    def full(shape):
        return pl.BlockSpec(shape, lambda j: (0,) * len(shape))

    return pl.pallas_call(
        functools.partial(_hyfilt_kernel, L=L),
        out_shape=(jax.ShapeDtypeStruct((L, half), F32), jax.ShapeDtypeStruct((L, half), F32),
                   jax.ShapeDtypeStruct((HY_ORDER, 1, HY_W), F32)),
        grid=(L // tk,),
        in_specs=[
            full((L, POS_PAD)), _slab(w1.shape, l), _slab(b1.shape, l), _slab(w2.shape, l),
            _slab(b2.shape, l), _slab(w3.shape, l), full((L, half)), full((L, 1)),
            pl.BlockSpec((tk, L), lambda j: (j, 0)), pl.BlockSpec((tk, L), lambda j: (j, 0)),
        ],
        out_specs=(pl.BlockSpec((tk, half), lambda j: (j, 0)), pl.BlockSpec((tk, half), lambda j: (j, 0)),
                   full((HY_ORDER, 1, HY_W))),
        scratch_shapes=[pltpu.VMEM((L, half), BF16), pltpu.VMEM((L, half), BF16)],
        compiler_params=_params(("arbitrary",)),
        name="hyena_filter_spectrum_%d" % L,
    )(feats, w1, b1, w2, b2, w3, decay, alt, ctab, stab)


def _hyconv_kernel(hy_ref, cw_ref, cb_ref, skip_ref, hre_ref, him_ref, nyq_ref, alt_ref, c_ref, s_ref, o_ref,
                   x1_sc, x2_sc, v_sc, vb_sc, yre_sc, yim_sc, nv_sc, *, n_seq, L, tk):
    p = pl.program_id(0)
    ph = pl.program_id(1)
    j = pl.program_id(2)
    cols = [slice(b * HY_W, (b + 1) * HY_W) for b in range(n_seq)]
    blk = pl.ds(pl.multiple_of(j * tk, tk), tk)

    @pl.when((p == 0) & (ph == 0) & (j == 0))
    def _():
        rows = lax.broadcasted_iota(jnp.int32, (L, 1), 0)
        w = cw_ref[...]
        for b in range(n_seq):
            x = hy_ref[b * L:(b + 1) * L, :]
            prev = jnp.where(rows > 0, pltpu.roll(x, 1, 0), 0.0)
            nxt = jnp.where(rows < L - 1, pltpu.roll(x, L - 1, 0), 0.0)
            u = prev * w[0:1] + x * w[1:2] + nxt * w[2:3] + cb_ref[...]
            x1_sc[:, cols[b]] = u[:, :HY_W]
            x2_sc[:, cols[b]] = u[:, HY_W:2 * HY_W]
            v_sc[:, cols[b]] = u[:, 2 * HY_W:]

    @pl.when((ph == 0) & (j == 0))
    def _():
        v = v_sc[...]
        vb_sc[...] = v.astype(BF16)
        nv_sc[...] = jnp.sum(v * alt_ref[...], axis=0, keepdims=True)

    @pl.when(ph == 0)
    def _():
        vb = vb_sc[...]
        v_re = _dot(c_ref[...], vb)
        v_im = -_dot(s_ref[...], vb)
        freq = j * tk + lax.broadcasted_iota(jnp.int32, (tk, 1), 0)
        wk = jnp.where(freq == 0, 0.5, 1.0)
        h_re = hre_ref[...] * wk
        h_im = him_ref[...] * wk
        for b in range(n_seq):
            yre_sc[blk, cols[b]] = (v_re[:, cols[b]] * h_re - v_im[:, cols[b]] * h_im).astype(BF16)
            yim_sc[blk, cols[b]] = (v_re[:, cols[b]] * h_im + v_im[:, cols[b]] * h_re).astype(BF16)

    @pl.when(ph == 1)
    def _():
        acc = _dot(c_ref[...], yre_sc[...]) - _dot(s_ref[...], yim_sc[...])
        alt = alt_ref[blk, :]
        for b in range(n_seq):
            v = v_sc[blk, cols[b]]
            nyq = nv_sc[:, cols[b]] * nyq_ref[...]
            y = acc[:, cols[b]] * (1.0 / L) + (0.5 / L) * alt * nyq + v * skip_ref[...]

            @pl.when(p == 0)
            def _():
                v_sc[blk, cols[b]] = x1_sc[blk, cols[b]] * y

            @pl.when(p == 1)
            def _():
                o_ref[pl.ds(pl.multiple_of(b * L + j * tk, tk), tk), :] = (x2_sc[blk, cols[b]] * y).astype(BF16)


def _hyconv_call(hy, l, conv_w, conv_b, skip, hre, him, nyq, alt, ctab, stab, *, n_seq, L, tok_off):
    tk = min(L, 512)
    nk = L // tk
    seg = tok_off // (n_seq * L)
    width = n_seq * HY_W

    def full(shape):
        return pl.BlockSpec(shape, lambda p, ph, j: (0,) * len(shape))

    def filt(p, ph, j):
        return (jnp.where(ph == 0, j, nk - 1), p)

    vm = lambda dtype: pltpu.VMEM((L, width), dtype)
    return pl.pallas_call(
        functools.partial(_hyconv_kernel, n_seq=n_seq, L=L, tk=tk),
        out_shape=jax.ShapeDtypeStruct((n_seq * L, HY_W), BF16),
        grid=(HY_ORDER, 2, nk),
        in_specs=[
            pl.BlockSpec((n_seq * L, 3 * HY_W), lambda p, ph, j: (seg, 0), pipeline_mode=pl.Buffered(1)),
            _slab(conv_w.shape, l), _slab(conv_b.shape, l),
            pl.BlockSpec((None, None, 1, HY_W), lambda p, ph, j: (l, p, 0, 0)),
            pl.BlockSpec((tk, HY_W), filt), pl.BlockSpec((tk, HY_W), filt),
            pl.BlockSpec((None, 1, HY_W), lambda p, ph, j: (p, 0, 0)),
            full((L, 1)),
            pl.BlockSpec((tk, L), lambda p, ph, j: (j, 0)), pl.BlockSpec((tk, L), lambda p, ph, j: (j, 0)),
        ],
        out_specs=pl.BlockSpec((n_seq * L, HY_W), lambda p, ph, j: (0, 0)),
        scratch_shapes=[vm(F32), vm(F32), vm(F32), vm(BF16), vm(BF16), vm(BF16), pltpu.VMEM((1, width), F32)],
        compiler_params=_params(("arbitrary", "arbitrary", "arbitrary")),
        name="hyena_long_conv_%d" % L,
    )(hy, conv_w, conv_b, skip, hre, him, nyq, alt, ctab, stab)


def _fnet_kernel(fcs_ref, c_ref, s_ref, o_ref, *, n_seq, L, tk, scale):
    j = pl.program_id(0)
    c = c_ref[...]
    s = s_ref[...]
    for b in range(n_seq):
        seq = slice(b * L, (b + 1) * L)
        y = (_dot(c, fcs_ref[seq, :FN_W]) - _dot(s, fcs_ref[seq, FN_W:])) * scale
        o_ref[pl.ds(pl.multiple_of(b * L + j * tk, tk), tk), :] = y.astype(BF16)


def _fnet_call(fcs, ctab, stab, *, n_seq, L, tok_off):
    tk = min(L, 512)
    seg = tok_off // (n_seq * L)
    return pl.pallas_call(
        functools.partial(_fnet_kernel, n_seq=n_seq, L=L, tk=tk, scale=1.0 / math.sqrt(L * FN_GROUP_W)),
        out_shape=jax.ShapeDtypeStruct((n_seq * L, FN_W), BF16),
        grid=(L // tk,),
        in_specs=[
            pl.BlockSpec((n_seq * L, 2 * FN_W), lambda j: (seg, 0)),
            pl.BlockSpec((tk, L), lambda j: (j, 0)), pl.BlockSpec((tk, L), lambda j: (j, 0)),
        ],
        out_specs=pl.BlockSpec((n_seq * L, FN_W), lambda j: (0, 0)),
        compiler_params=_params(("arbitrary",)),
        name="fnet_position_dft_%d" % L,
    )(fcs, ctab, stab)


def _mixout_kernel(h_ref, mod_ref, ng_ref, wg_ref, zc_ref, zl_ref, ac_ref, al_ref, fc_ref, fl_ref,
                   wa_ref, wb_ref, wc_ref, wo_ref, o_ref, wg_sc):
    s = pl.program_id(0)
    tail = IN_COLS - GATE_LOAD * GATE_BLK

    @pl.when(s < GATE_LOAD - 1)
    def _():
        wg_sc[pl.ds(pl.multiple_of(s * GATE_BLK, GATE_BLK), GATE_BLK), :] = wg_ref[...].astype(BF16)

    @pl.when(s == GATE_LOAD - 1)
    def _():
        wg_sc[(GATE_LOAD - 1) * GATE_BLK:(GATE_LOAD - 1) * GATE_BLK + tail, :] = wg_ref[:tail, :].astype(BF16)

    @pl.when(s >= GATE_LOAD)
    def _():
        i = s - GATE_LOAD
        is_ctx = i < T_CTX // TM_OUT
        m = mod_ref[_group_of(i * TM_OUT)]
        g0, g1, g2, g3 = (GATE_OFF + b * D_MODEL for b in range(N_BRANCH + 1))
        for sub in range(TM_OUT // OUT_SUB_ROWS):
            rows = slice(sub * OUT_SUB_ROWS, (sub + 1) * OUT_SUB_ROWS)
            h = h_ref[rows, :]
            n = (_rms(h, ng_ref[...]) * (1.0 + m[4:5]) + m[3:4]).astype(BF16)
            z = jnp.where(is_ctx, zc_ref[rows, :], zl_ref[rows, :])
            a = jnp.where(is_ctx, ac_ref[rows, :], al_ref[rows, :])
            f = jnp.where(is_ctx, fc_ref[rows, :], fl_ref[rows, :])
            acc = _sigmoid(_dot_nt(n, wg_sc[g0:g1, :])) * _dot(z, wa_ref[...])
            acc = acc + _sigmoid(_dot_nt(n, wg_sc[g1:g2, :])) * _dot(a, wb_ref[...])
            acc = acc + _sigmoid(_dot_nt(n, wg_sc[g2:g3, :])) * _dot(f, wc_ref[...])
            y = _dot(acc.astype(BF16), wo_ref[...])
            o_ref[rows, :] = h + m[5:6] * y


def _mixout_call(h, mods, ng, w_in_t, l, z, a, f, wa, wb, wc, wo):
    tm = TM_OUT
    n_ctx = T_CTX // tm
    weight = functools.partial(_slab, single_buffer=True)
    tile = lambda s: jnp.maximum(s - GATE_LOAD, 0)

    def tok(width):
        return pl.BlockSpec((tm, width), lambda s: (tile(s), 0))

    def pair(width):
        return [pl.BlockSpec((tm, width), lambda s: (jnp.minimum(tile(s), n_ctx - 1), 0)),
                pl.BlockSpec((tm, width), lambda s: (jnp.maximum(tile(s) - n_ctx, 0), 0))]

    return pl.pallas_call(
        _mixout_kernel,
        out_shape=jax.ShapeDtypeStruct((T_ALL, D_MODEL), F32),
        grid=(GATE_LOAD + T_ALL // tm,),
        in_specs=[
            tok(D_MODEL), _slab(mods.shape, l), _slab(ng.shape, l, 1),
            pl.BlockSpec((None, GATE_BLK, D_MODEL), lambda s: (l, jnp.minimum(s, GATE_LOAD - 1) + 1, 0),
                         pipeline_mode=pl.Buffered(1)),
            *pair(HY_W), *pair(N_HEADS * V_DIM), *pair(FN_W),
            weight(wa.shape, l), weight(wb.shape, l), weight(wc.shape, l), weight(wo.shape, l),
        ],
        out_specs=tok(D_MODEL),
        scratch_shapes=[pltpu.VMEM((GATE_LOAD * GATE_BLK, D_MODEL), BF16)],
        compiler_params=_params(("arbitrary",)),
        name="gated_merge_out_proj",
    )(h, mods, ng, w_in_t, *z, *a, *f, wa, wb, wc, wo)


def _head_slot_weights(w_qb, w_kvb):
    wq3 = w_qb.reshape(DEPTH, Q_LORA, N_HEADS, NOPE_DIM + ROPE_DIM)
    slot_pad = HEAD_SLOT - NOPE_DIM - ROPE_DIM
    no_pad = ((0, 0),) * 3
    wq = jnp.pad(wq3, no_pad + ((0, slot_pad),)).reshape(DEPTH, Q_LORA, QK_W).astype(BF16)
    rope = wq3[..., NOPE_DIM:].reshape(DEPTH, Q_LORA, N_HEADS, 2, 2, ROPE_HALF)
    partner = jnp.stack([-rope[..., 1, :], rope[..., 0, :]], axis=4).reshape(DEPTH, Q_LORA, N_HEADS, ROPE_DIM)
    wqs = jnp.pad(partner, no_pad + ((NOPE_DIM, slot_pad),)).reshape(DEPTH, Q_LORA, QK_W).astype(BF16)

    wkv3 = w_kvb.reshape(DEPTH, KV_LORA, N_HEADS, NOPE_DIM + V_DIM)
    wk = jnp.pad(wkv3[..., :NOPE_DIM], no_pad + ((0, HEAD_SLOT - NOPE_DIM),))
    wk = wk.reshape(DEPTH, KV_LORA, QK_W).astype(BF16)
    wv = wkv3[..., NOPE_DIM:].reshape(DEPTH, KV_LORA, N_HEADS * V_DIM).astype(BF16)
    return wq, wqs, wk, wv


def kernel(x_prompt, x_sample, cache_ckv, cache_krope, c, c_ctx, w_ada, b_ada, norm_g, w_ffn_up, w_ffn_down,
           w_in, hy_conv_w, hy_conv_b, hy_filt_w1, hy_filt_b1, hy_filt_w2, hy_filt_b2, hy_filt_w3, hy_skip,
           w_hy_out, q_norm_g, w_qb, kv_norm_g, w_kvb, w_mla_o, w_fnet, w_out, final_g):
    tabs = _tables()
    cosq, sinq, cosk, sink = (jnp.asarray(t) for t in tabs["rope"])
    fch = jnp.asarray(tabs["fnch"]).astype(BF16)
    ek = jnp.asarray(tabs["ropeexp"]).astype(BF16)
    dft = {}
    for L in (SEQ, DEC_SEQ):
        dft[("hy", L)] = tuple(jnp.asarray(t).astype(BF16) for t in tabs[("hy", L)])
        dft[("fn", L)] = tuple(jnp.asarray(t).astype(BF16) for t in tabs[("fn", L)])

    cvec = jnp.concatenate([c_ctx[None, :], c, jnp.zeros((SUBLANES - N_GROUPS, D_MODEL), F32)], axis=0).T
    ada = _ada_call(cvec, w_ada, b_ada)
    mods = ada[:, :N_GROUPS].reshape(DEPTH, N_GROUPS, 9, D_MODEL)
    ng = norm_g.reshape(DEPTH, 3, 1, D_MODEL)
    qg = q_norm_g.reshape(DEPTH, 1, Q_LORA)
    kvg = kv_norm_g.reshape(DEPTH, 1, KV_LORA)
    wq, wqs, wk, wv = _head_slot_weights(w_qb, w_kvb)
    w1 = jnp.pad(hy_filt_w1, ((0, 0), (0, POS_PAD - POS_EMB), (0, 0)))
    b1 = hy_filt_b1.reshape(DEPTH, 1, FILT_HID)
    b2 = hy_filt_b2.reshape(DEPTH, 1, FILT_HID)
    conv_b = hy_conv_b.reshape(DEPTH, 1, 3 * HY_W)
    skip = hy_skip.reshape(DEPTH, HY_ORDER, 1, HY_W)

    w_in_t = jnp.swapaxes(w_in, 1, 2)
    w_hy_out, w_mla_o, w_fnet, w_out = (w.astype(BF16) for w in (w_hy_out, w_mla_o, w_fnet, w_out))
    hs = (x_prompt.reshape(T_CTX, D_MODEL), x_sample.reshape(T_LAT, D_MODEL))
    segs = ((BATCH, SEQ, 0), (DEC_BATCH, DEC_SEQ, T_CTX))
    ckv_out = []
    kr_out = []
    for l in range(DEPTH):
        h = _ffn_call(hs, mods, ng, w_ffn_up, w_ffn_down, l, 0, 0)

        hy, q, k, v, ckv, k_r, fcs = _mixin_call(
            h, mods, ng, w_in_t, l, qg, kvg, wq, wqs, wk, wv, fch, cosq, sinq, cosk, sink)
        ckv_out.append(ckv.reshape(BATCH, SEQ, KV_LORA))
        kr_out.append(k_r.reshape(BATCH, SEQ, ROPE_DIM))

        z_parts, a_parts, f_parts = [], [], []
        for n_seq, L, off in segs:
            feats, decay, alt = (jnp.asarray(t) for t in tabs[("filt", L)])
            c_hy, s_hy = dft[("hy", L)]
            hre, him, nyq = _hyfilt_call(L, l, feats, w1, b1, hy_filt_w2, b2, hy_filt_w3, decay, alt, c_hy, s_hy)
            z_parts.append(_hyconv_call(hy, l, hy_conv_w, conv_b, skip, hre, him, nyq, alt, c_hy, s_hy,
                                        n_seq=n_seq, L=L, tok_off=off))
            if off:
                a_parts.append(_attn_call(q, k, v, (cache_ckv, cache_krope, l, wk, wv, ek),
                                          n_seq=n_seq, seq_len=L, tok_off=off, tq=ATTN_TQ))
            else:
                a_parts.append(_attn_call(q, k, v, None, n_seq=n_seq, seq_len=L, tok_off=off, tq=L,
                                          group=ATTN_GROUP))
            c_fn, s_fn = dft[("fn", L)]
            f_parts.append(_fnet_call(fcs, c_fn, s_fn, n_seq=n_seq, L=L, tok_off=off))
        h = _mixout_call(h, mods, ng, w_in_t, l, z_parts, a_parts, f_parts, w_hy_out, w_mla_o, w_fnet, w_out)

        last = l == DEPTH - 1
        out = _ffn_call((h,), mods, ng, w_ffn_up, w_ffn_down, l, 1, 6, final_g[None, :] if last else None)
        hs = out if last else (out,)

    y_prompt = hs[0].reshape(BATCH, SEQ, D_MODEL)
    y_sample = hs[1].reshape(DEC_BATCH, DEC_SEQ, D_MODEL)
    return y_prompt, y_sample, jnp.stack(ckv_out, axis=1), jnp.stack(kr_out, axis=1)
```
